```python
import math
import jax
import jax.numpy as jnp
from jax import lax
import numpy as np

D_MODEL = 1024
BATCH = 4
SEQ = 4096
DEPTH = 4
DEC_BATCH = 32
DEC_SEQ = 4
PAST_LEN = 8192
PAGE_SIZE = 128

N_EVEN = (DEPTH + 1) // 2
N_ODD = DEPTH // 2
A_HEADS = 4
A_QK_DIM = 64
A_V_DIM = 2 * A_QK_DIM
A_Q_W = A_HEADS * 2 * A_QK_DIM
A_V_W = A_HEADS * A_V_DIM
B_CH = 512
CONV_W = 31
EVEN_IN = 2 * A_Q_W + A_V_W + 2 * B_CH
EVEN_MIX = A_V_W + B_CH
C_WIDTH = D_MODEL
GROUP_CH = 16
N_GROUPS = C_WIDTH // GROUP_CH
STATE_P = 64
D_FF = 4 * D_MODEL
Q_BLOCK = 128
RMS_EPS = 1e-6
SUBLN_EPS = 1e-5
LN_EPS = 1e-5
DT_MIN = 1e-3
DT_MAX = 1e-1

kernel_name = "hybrid_diffattn_conformer_s5_step"


def rms_norm(x, g, eps=RMS_EPS):
    xf = x.astype(jnp.float32)
    y = xf * lax.rsqrt(jnp.mean(xf * xf, axis=-1, keepdims=True) + eps)
    return (y * g.astype(jnp.float32)).astype(x.dtype)


def layer_norm(x, g, b, eps=LN_EPS):
    xf = x.astype(jnp.float32)
    mu = jnp.mean(xf, axis=-1, keepdims=True)
    xc = xf - mu
    y = xc * lax.rsqrt(jnp.mean(xc * xc, axis=-1, keepdims=True) + eps)
    return y * g.astype(jnp.float32) + b.astype(jnp.float32)


def lambda_init_fn(layer):
    return 0.8 - 0.6 * math.exp(-0.3 * layer)


def diff_attention(q, k, v, q_pos, k_pos, lam):
    scale = A_QK_DIM ** -0.5
    s = jnp.einsum('bqhcd,bkhcd->bhcqk', q.astype(jnp.float32), k.astype(jnp.float32)) * scale
    visible = k_pos[None, :] <= q_pos[:, None]
    s = jnp.where(visible, s, -jnp.inf)
    p = jax.nn.softmax(s, axis=-1)
    w = p[:, :, 0] - lam * p[:, :, 1]
    return jnp.einsum('bhqk,bkhe->bqhe', w, v.astype(jnp.float32))


def diff_attention_blocked(q, k, v, lam):
    bt, L = q.shape[0], q.shape[1]
    nb = L // Q_BLOCK
    qb = q.reshape(bt, nb, Q_BLOCK, A_HEADS, 2, A_QK_DIM).transpose(1, 0, 2, 3, 4, 5)
    pos = jnp.arange(L, dtype=jnp.int32)
    pb = pos.reshape(nb, Q_BLOCK)
    out = lax.map(lambda a: diff_attention(a[0], k, v, a[1], pos, lam), (qb, pb))
    return out.transpose(1, 0, 2, 3, 4).reshape(bt, L, A_HEADS, A_V_DIM)


def causal_depthwise_conv(u, buf, w, b):
    full = jnp.concatenate([buf.astype(u.dtype), u], axis=1)
    y = lax.conv_general_dilated(full, w.astype(u.dtype)[:, None, :], window_strides=(1,), padding='VALID',
                                 dimension_numbers=('NWC', 'WIO', 'NWC'), feature_group_count=B_CH)
    return y + b.astype(u.dtype), full[:, -(CONV_W - 1):]


def even_mixer(xn, w_in, lam_qk, subln_g, conv_w, conv_b, ln_g, ln_b, w_out, layer, k_past, v_past, conv_buf):
    bt, L, _ = xn.shape
    proj = xn @ w_in
    q, k, v, ug = jnp.split(proj, [A_Q_W, 2 * A_Q_W, 2 * A_Q_W + A_V_W], axis=-1)
    q = q.reshape(bt, L, A_HEADS, 2, A_QK_DIM)
    k = k.reshape(bt, L, A_HEADS, 2, A_QK_DIM)
    v = v.reshape(bt, L, A_HEADS, A_V_DIM)
    lam_init = lambda_init_fn(layer)
    lq = lam_qk.astype(jnp.float32)
    lam = jnp.exp(jnp.sum(lq[0] * lq[1])) - jnp.exp(jnp.sum(lq[2] * lq[3])) + lam_init
    if k_past is None:
        attn = diff_attention_blocked(q, k, v, lam)
        conv_buf = jnp.zeros((bt, CONV_W - 1, B_CH), proj.dtype)
    else:
        past_len = k_past.shape[1]
        k_all = jnp.concatenate([k_past.astype(k.dtype), k], axis=1)
        v_all = jnp.concatenate([v_past.astype(v.dtype), v], axis=1)
        q_pos = past_len + jnp.arange(L, dtype=jnp.int32)
        k_pos = jnp.arange(past_len + L, dtype=jnp.int32)
        attn = diff_attention(q, k_all, v_all, q_pos, k_pos, lam)
    attn = rms_norm(attn, subln_g, SUBLN_EPS) * (1.0 - lam_init)
    attn = attn.reshape(bt, L, A_V_W)
    a, g = jnp.split(ug, 2, axis=-1)
    u = a * jax.nn.sigmoid(g)
    c, new_buf = causal_depthwise_conv(u, conv_buf, conv_w, conv_b)
    c = jax.nn.silu(layer_norm(c, ln_g, ln_b))
    mixed = jnp.concatenate([attn.astype(xn.dtype), c.astype(xn.dtype)], axis=-1)
    return mixed @ w_out, k, v, new_buf


def _complex_affine_combine(e1, e2):
    a1r, a1i, b1r, b1i = e1
    a2r, a2i, b2r, b2i = e2
    return (a2r * a1r - a2i * a1i,
            a2r * a1i + a2i * a1r,
            a2r * b1r - a2i * b1i + b2r,
            a2r * b1i + a2i * b1r + b2i)


def s5_mixer(xn, w_in, a_re, a_im, b_re, b_im, c_re, c_im, d_skip, log_dt, w_gate, w_out, h0_re, h0_im):
    f32 = jnp.float32
    bt, L, _ = xn.shape
    u = (xn @ w_in).astype(f32)
    dt = jnp.exp(log_dt.astype(f32))[:, None]
    lr, li = a_re.astype(f32), a_im.astype(f32)
    mag = jnp.exp(lr * dt)
    ang = li * dt
    ab_re, ab_im = mag * jnp.cos(ang), mag * jnp.sin(ang)
    den = lr * lr + li * li
    nr, ni = ab_re - 1.0, ab_im
    f_re = (nr * lr + ni * li) / den
    f_im = (ni * lr - nr * li) / den
    br, bi = b_re.astype(f32), b_im.astype(f32)
    bb_re = f_re[..., None] * br - f_im[..., None] * bi
    bb_im = f_re[..., None] * bi + f_im[..., None] * br
    ug = u.reshape(bt, L, N_GROUPS, GROUP_CH)
    bu_re = jnp.einsum('blgc,gpc->blgp', ug, bb_re)
    bu_im = jnp.einsum('blgc,gpc->blgp', ug, bb_im)
    h0r, h0i = h0_re.astype(f32), h0_im.astype(f32)
    bu_re = bu_re.at[:, 0].add(ab_re * h0r - ab_im * h0i)
    bu_im = bu_im.at[:, 0].add(ab_re * h0i + ab_im * h0r)
    a_re_b = jnp.broadcast_to(ab_re, bu_re.shape)
    a_im_b = jnp.broadcast_to(ab_im, bu_im.shape)
    _, _, h_re, h_im = lax.associative_scan(_complex_affine_combine, (a_re_b, a_im_b, bu_re, bu_im), axis=1)
    y = (jnp.einsum('blgp,gcp->blgc', h_re, c_re.astype(f32))
         - jnp.einsum('blgp,gcp->blgc', h_im, c_im.astype(f32)))
    y = y.reshape(bt, L, C_WIDTH) + d_skip.astype(f32) * u
    z = jax.nn.gelu(y, approximate=False)
    z = z * jax.nn.sigmoid(z @ w_gate.astype(f32))
    out = (z @ w_out.astype(f32)).astype(xn.dtype)
    return out, h_re[:, -1], h_im[:, -1]


def sq_relu_mlp(xn, w_up, w_down):
    return jnp.square(jax.nn.relu(xn @ w_up)) @ w_down


def setup_inputs(seed: int = 0) -> dict:
    key = jax.random.key(seed)
    ks = list(jax.random.split(key, 40))
    f32 = jnp.float32
    nrm = lambda k, shape, s: jax.random.normal(k, shape, f32) * s
    n_pages = PAST_LEN // PAGE_SIZE
    n_used = DEC_BATCH * n_pages
    n_phys = n_used + max(1, n_used // 4)
    page_table = jax.random.permutation(ks[0], n_phys)[:n_used].reshape(DEC_BATCH, n_pages).astype(jnp.int32)
    gain = lambda k, shape: 1.0 + nrm(k, shape, 0.01)
    a_im = math.pi * jnp.arange(STATE_P, dtype=f32)[None, None, :] + nrm(ks[20], (N_ODD, N_GROUPS, STATE_P), 0.01)
    return {
        "x_prompt": nrm(ks[1], (BATCH, SEQ, D_MODEL), 1.0),
        "x_sample": nrm(ks[2], (DEC_BATCH, DEC_SEQ, D_MODEL), 1.0),
        "cache_k": nrm(ks[3], (N_EVEN, n_phys, PAGE_SIZE, A_HEADS, 2, A_QK_DIM), 1.0),
        "cache_v": nrm(ks[4], (N_EVEN, n_phys, PAGE_SIZE, A_HEADS, A_V_DIM), 1.0),
        "page_table": page_table,
        "state_conv": nrm(ks[5], (N_EVEN, DEC_BATCH, CONV_W - 1, B_CH), 0.5),
        "state_ssm_re": nrm(ks[6], (N_ODD, DEC_BATCH, N_GROUPS, STATE_P), 0.1),
        "state_ssm_im": nrm(ks[7], (N_ODD, DEC_BATCH, N_GROUPS, STATE_P), 0.1),
        "g_mix_pre": gain(ks[8], (DEPTH, D_MODEL)),
        "g_mix_post": gain(ks[9], (DEPTH, D_MODEL)),
        "g_ffn_pre": gain(ks[10], (DEPTH, D_MODEL)),
        "g_ffn_post": gain(ks[11], (DEPTH, D_MODEL)),
        "w_in_even": nrm(ks[12], (N_EVEN, D_MODEL, EVEN_IN), D_MODEL ** -0.5),
        "lambda_qk": nrm(ks[13], (N_EVEN, 4, A_QK_DIM), 0.1),
        "subln_g": gain(ks[14], (N_EVEN, A_V_DIM)),
        "conv_w": nrm(ks[15], (N_EVEN, CONV_W, B_CH), CONV_W ** -0.5),
        "conv_b": nrm(ks[16], (N_EVEN, B_CH), 0.01),
        "conv_ln_g": gain(ks[17], (N_EVEN, B_CH)),
        "conv_ln_b": nrm(ks[18], (N_EVEN, B_CH), 0.01),
        "w_out_even": nrm(ks[19], (N_EVEN, EVEN_MIX, D_MODEL), EVEN_MIX ** -0.5),
        "w_in_odd": nrm(ks[21], (N_ODD, D_MODEL, C_WIDTH), D_MODEL ** -0.5),
        "ssm_a_re": -0.5 + nrm(ks[22], (N_ODD, N_GROUPS, STATE_P), 0.01),
        "ssm_a_im": a_im,
        "ssm_b_re": nrm(ks[23], (N_ODD, N_GROUPS, STATE_P, GROUP_CH), (2 * GROUP_CH) ** -0.5),
        "ssm_b_im": nrm(ks[24], (N_ODD, N_GROUPS, STATE_P, GROUP_CH), (2 * GROUP_CH) ** -0.5),
        "ssm_c_re": nrm(ks[25], (N_ODD, N_GROUPS, GROUP_CH, STATE_P), STATE_P ** -0.5),
        "ssm_c_im": nrm(ks[26], (N_ODD, N_GROUPS, GROUP_CH, STATE_P), STATE_P ** -0.5),
        "ssm_d": nrm(ks[27], (N_ODD, C_WIDTH), 1.0),
        "ssm_log_dt": jax.random.uniform(ks[28], (N_ODD, N_GROUPS), f32, math.log(DT_MIN), math.log(DT_MAX)),
        "w_gate_odd": nrm(ks[29], (N_ODD, C_WIDTH, C_WIDTH), C_WIDTH ** -0.5),
        "w_out_odd": nrm(ks[30], (N_ODD, C_WIDTH, D_MODEL), C_WIDTH ** -0.5),
        "w_ffn_up": nrm(ks[31], (DEPTH, D_MODEL, D_FF), D_MODEL ** -0.5),
        "w_ffn_down": nrm(ks[32], (DEPTH, D_FF, D_MODEL), D_FF ** -0.5),
    }


def reference(x_prompt, x_sample, cache_k, cache_v, page_table, state_conv, state_ssm_re, state_ssm_im,
              g_mix_pre, g_mix_post, g_ffn_pre, g_ffn_post,
              w_in_even, lambda_qk, subln_g, conv_w, conv_b, conv_ln_g, conv_ln_b, w_out_even,
              w_in_odd, ssm_a_re, ssm_a_im, ssm_b_re, ssm_b_im, ssm_c_re, ssm_c_im, ssm_d, ssm_log_dt,
              w_gate_odd, w_out_odd, w_ffn_up, w_ffn_down):
    hp, hs = x_prompt, x_sample
    dec_b = page_table.shape[0]
    kp_l, vp_l, ks_l, vs_l, cp_l, cs_l = [], [], [], [], [], []
    srp_l, sip_l, srs_l, sis_l = [], [], [], []
    for l in range(DEPTH):
        i = l // 2
        if l % 2 == 0:
            ew = (w_in_even[i], lambda_qk[i], subln_g[i], conv_w[i], conv_b[i], conv_ln_g[i], conv_ln_b[i], w_out_even[i])
            mix, k_new, v_new, buf_new = even_mixer(rms_norm(hp, g_mix_pre[l]), *ew, l, None, None, None)
            hp = hp + rms_norm(mix, g_mix_post[l])
            kp_l.append(k_new); vp_l.append(v_new); cp_l.append(buf_new)
            k_past = cache_k[i, page_table].reshape(dec_b, -1, A_HEADS, 2, A_QK_DIM)
            v_past = cache_v[i, page_table].reshape(dec_b, -1, A_HEADS, A_V_DIM)
            mix, k_new, v_new, buf_new = even_mixer(rms_norm(hs, g_mix_pre[l]), *ew, l, k_past, v_past, state_conv[i])
            hs = hs + rms_norm(mix, g_mix_post[l])
            ks_l.append(k_new); vs_l.append(v_new); cs_l.append(buf_new)
        else:
            ow = (w_in_odd[i], ssm_a_re[i], ssm_a_im[i], ssm_b_re[i], ssm_b_im[i], ssm_c_re[i], ssm_c_im[i],
                  ssm_d[i], ssm_log_dt[i], w_gate_odd[i], w_out_odd[i])
            zeros = jnp.zeros((hp.shape[0], N_GROUPS, STATE_P), jnp.float32)
            mix, sr, si = s5_mixer(rms_norm(hp, g_mix_pre[l]), *ow, zeros, zeros)
            hp = hp + rms_norm(mix, g_mix_post[l])
            srp_l.append(sr); sip_l.append(si)
            mix, sr, si = s5_mixer(rms_norm(hs, g_mix_pre[l]), *ow, state_ssm_re[i], state_ssm_im[i])
            hs = hs + rms_norm(mix, g_mix_post[l])
            srs_l.append(sr); sis_l.append(si)
        hp = hp + rms_norm(sq_relu_mlp(rms_norm(hp, g_ffn_pre[l]), w_ffn_up[l], w_ffn_down[l]), g_ffn_post[l])
        hs = hs + rms_norm(sq_relu_mlp(rms_norm(hs, g_ffn_pre[l]), w_ffn_up[l], w_ffn_down[l]), g_ffn_post[l])
    new_k_prompt = jnp.stack(kp_l)
    new_v_prompt = jnp.stack(vp_l)
    new_k_sample = jnp.stack(ks_l)
    new_v_sample = jnp.stack(vs_l)
    new_conv_prompt = jnp.stack(cp_l)
    new_conv_sample = jnp.stack(cs_l)
    new_ssm_re_prompt = jnp.stack(srp_l)
    new_ssm_im_prompt = jnp.stack(sip_l)
    new_ssm_re_sample = jnp.stack(srs_l)
    new_ssm_im_sample = jnp.stack(sis_l)
    return (hp, hs, new_k_prompt, new_v_prompt, new_k_sample, new_v_sample, new_conv_prompt, new_conv_sample,
            new_ssm_re_prompt, new_ssm_im_prompt, new_ssm_re_sample, new_ssm_im_sample)
```

```python
import functools
import math

import jax
import jax.numpy as jnp
import numpy as np
from jax import lax
from jax.experimental import pallas as pl
from jax.experimental.pallas import tpu as pltpu

F32 = jnp.float32
BF16 = jnp.bfloat16

RMS_EPS = 1e-6
SUBLN_EPS = 1e-5
LN_EPS = 1e-5

LANES = 128
SUBLANES = 8
N_HEADS = 4
QK_DIM = 64
HEAD_W = 2 * QK_DIM
ATT_W = N_HEADS * HEAD_W
CONV_CH = 512
CONV_TAPS = 31
CONV_HIST = CONV_TAPS - 1
GROUP_CH = 16
STATE_P = 64
GROUPS_PER_TILE = LANES // GROUP_CH
STATE_TILE = GROUPS_PER_TILE * STATE_P
VMEM_LIMIT = 52 * 1024 * 1024


def _cparams(semantics):
    return pltpu.CompilerParams(dimension_semantics=semantics, vmem_limit_bytes=VMEM_LIMIT)


def _rms(x, g, eps):
    ms = jnp.mean(x * x, axis=-1, keepdims=True)
    return (x * lax.rsqrt(ms + eps)) * g


def _sigmoid(x):
    return 1.0 / (1.0 + jnp.exp(-x))


def _lambda_full(lq, lam_init):
    s1 = jnp.sum(lq[0:1, :] * lq[1:2, :], axis=-1, keepdims=True)
    s2 = jnp.sum(lq[2:3, :] * lq[3:4, :], axis=-1, keepdims=True)
    return jnp.exp(s1) - jnp.exp(s2) + lam_init


def _lambda_init(layer):
    return 0.8 - 0.6 * math.exp(-0.3 * layer)


def _even_inproj_kernel(x_ref, g_ref, w_ref, wkt_ref, q_ref, kt_ref, ktb_ref, v_ref, vb_ref, ug_ref):
    xn = _rms(x_ref[...], g_ref[...], RMS_EPS).astype(BF16)

    def proj(c0, width):
        return jnp.dot(xn, w_ref[:, c0:c0 + width], preferred_element_type=F32)

    q_ref[...] = (proj(0, ATT_W) * (QK_DIM ** -0.5)).astype(BF16)
    kt = lax.dot_general(wkt_ref[...], xn, (((1,), (1,)), ((), ())), preferred_element_type=F32)
    kt_ref[...] = kt
    ktb_ref[...] = kt.astype(BF16)
    v = proj(2 * ATT_W, ATT_W)
    v_ref[...] = v
    vb_ref[...] = v.astype(BF16)
    ug_ref[...] = proj(3 * ATT_W, 2 * CONV_CH)


def _even_inproj(x3, g, w_bf, wkt_bf, tm):
    bx, sx, d = x3.shape
    n_in = w_bf.shape[1]
    row = lambda b, i: (b, i, 0)
    col = lambda b, i: (b, 0, i)
    const = lambda b, i: (0, 0)
    return pl.pallas_call(
        _even_inproj_kernel,
        grid=(bx, sx // tm),
        in_specs=[pl.BlockSpec((None, tm, d), row), pl.BlockSpec((1, d), const),
                  pl.BlockSpec((d, n_in), const), pl.BlockSpec((ATT_W, d), const)],
        out_specs=[pl.BlockSpec((None, tm, ATT_W), row), pl.BlockSpec((None, ATT_W, tm), col),
                   pl.BlockSpec((None, ATT_W, tm), col), pl.BlockSpec((None, tm, ATT_W), row),
                   pl.BlockSpec((None, tm, ATT_W), row), pl.BlockSpec((None, tm, 2 * CONV_CH), row)],
        out_shape=[jax.ShapeDtypeStruct((bx, sx, ATT_W), BF16), jax.ShapeDtypeStruct((bx, ATT_W, sx), F32),
                   jax.ShapeDtypeStruct((bx, ATT_W, sx), BF16), jax.ShapeDtypeStruct((bx, sx, ATT_W), F32),
                   jax.ShapeDtypeStruct((bx, sx, ATT_W), BF16), jax.ShapeDtypeStruct((bx, sx, 2 * CONV_CH), F32)],
        compiler_params=_cparams(("parallel", "parallel")),
        name="even_inproj",
    )(x3, g, w_bf, wkt_bf)


def _flash_kernel(it_ref, jt_ref, lq_ref, g_ref, q_ref, kt_ref, v_ref, o_ref, qm_ref, m_ref, l_ref, acc_ref,
                  *, lam_init):
    t = pl.program_id(2)
    i = it_ref[t]
    j = jt_ref[t]

    @pl.when(j == 0)
    def _init():
        q = q_ref[...]
        lane = lax.broadcasted_iota(jnp.int32, q.shape, 1)
        first = jnp.where(lane < QK_DIM, 1.0, 0.0).astype(BF16)
        qm_ref[0] = q * first
        qm_ref[1] = q * (1.0 - first)
        m_ref[...] = jnp.full(m_ref.shape, -jnp.inf, F32)
        l_ref[...] = jnp.zeros(l_ref.shape, F32)
        acc_ref[...] = jnp.zeros(acc_ref.shape, F32)

    def step(masked):
        kt = kt_ref[...]
        v = v_ref[...]
        for c in range(2):
            s = jnp.dot(qm_ref[c], kt, preferred_element_type=F32)
            if masked:
                row = lax.broadcasted_iota(jnp.int32, s.shape, 0)
                col = lax.broadcasted_iota(jnp.int32, s.shape, 1)
                s = jnp.where(col <= row, s, -jnp.inf)
            m_prev = m_ref[c]
            m_new = jnp.maximum(m_prev, jnp.max(s, axis=-1, keepdims=True))
            alpha = jnp.exp(m_prev - m_new)
            p = jnp.exp(s - m_new)
            l_ref[c] = alpha * l_ref[c] + jnp.sum(p, axis=-1, keepdims=True)
            acc_ref[c] = alpha * acc_ref[c] + jnp.dot(p.astype(BF16), v, preferred_element_type=F32)
            m_ref[c] = m_new

    @pl.when(j < i)
    def _full():
        step(False)

    @pl.when(j == i)
    def _diag():
        step(True)
        lam = _lambda_full(lq_ref[...], lam_init)
        o = acc_ref[0] / l_ref[0] - lam * (acc_ref[1] / l_ref[1])
        o_ref[...] = (_rms(o, g_ref[...], SUBLN_EPS) * (1.0 - lam_init)).astype(BF16)


def _flash_attention(q, kt, v, lq, subln_g, lam_init, tq):
    b, s, _ = q.shape
    nq = s // tq
    pairs = [(i, j) for i in range(nq) for j in range(i + 1)]
    it = jnp.asarray(np.array([p[0] for p in pairs], np.int32))
    jt = jnp.asarray(np.array([p[1] for p in pairs], np.int32))
    grid_spec = pltpu.PrefetchScalarGridSpec(
        num_scalar_prefetch=2,
        grid=(b, N_HEADS, len(pairs)),
        in_specs=[pl.BlockSpec((4, QK_DIM), lambda b_, h, t, it_, jt_: (0, 0)),
                  pl.BlockSpec((1, HEAD_W), lambda b_, h, t, it_, jt_: (0, 0)),
                  pl.BlockSpec((None, tq, HEAD_W), lambda b_, h, t, it_, jt_: (b_, it_[t], h)),
                  pl.BlockSpec((None, HEAD_W, tq), lambda b_, h, t, it_, jt_: (b_, h, jt_[t])),
                  pl.BlockSpec((None, tq, HEAD_W), lambda b_, h, t, it_, jt_: (b_, jt_[t], h))],
        out_specs=pl.BlockSpec((None, tq, HEAD_W), lambda b_, h, t, it_, jt_: (b_, it_[t], h)),
        scratch_shapes=[pltpu.VMEM((2, tq, HEAD_W), BF16), pltpu.VMEM((2, tq, 1), F32),
                        pltpu.VMEM((2, tq, 1), F32), pltpu.VMEM((2, tq, HEAD_W), F32)],
    )
    return pl.pallas_call(
        functools.partial(_flash_kernel, lam_init=lam_init),
        grid_spec=grid_spec,
        out_shape=jax.ShapeDtypeStruct((b, s, ATT_W), BF16),
        compiler_params=_cparams(("parallel", "parallel", "arbitrary")),
        name="flash_diff_attn",
    )(it, jt, lq, subln_g, q, kt, v)


PAGES_PER_STEP = 8
Q_ROWS = N_HEADS * 2 * SUBLANES


def _paged_kernel(pt_ref, lq_ref, g_ref, qbd_ref, ktn_ref, vn_ref, *rest, n_steps, n_new, lam_init):
    k_refs = rest[:PAGES_PER_STEP]
    v_refs = rest[PAGES_PER_STEP:2 * PAGES_PER_STEP]
    o_ref, m_ref, l_ref, acc_ref = rest[2 * PAGES_PER_STEP:]
    del pt_ref
    step = pl.program_id(1)
    rows_per_head = 2 * SUBLANES

    @pl.when(step == 0)
    def _init():
        m_ref[...] = jnp.full(m_ref.shape, -jnp.inf, F32)
        l_ref[...] = jnp.zeros(l_ref.shape, F32)
        acc_ref[...] = jnp.zeros(acc_ref.shape, F32)

    def update(scores, values):
        m_prev = m_ref[...]
        m_new = m_prev
        for s in scores:
            m_new = jnp.maximum(m_new, jnp.max(s, axis=-1, keepdims=True))
        alpha = jnp.exp(m_prev - m_new)
        l_new = alpha * l_ref[...]
        acc = alpha * acc_ref[...]
        parts = [acc[h * rows_per_head:(h + 1) * rows_per_head] for h in range(N_HEADS)]
        for s, vh in zip(scores, values):
            p = jnp.exp(s - m_new)
            l_new = l_new + jnp.sum(p, axis=-1, keepdims=True)
            pb = p.astype(BF16)
            for h in range(N_HEADS):
                parts[h] = parts[h] + jnp.dot(pb[h * rows_per_head:(h + 1) * rows_per_head], vh[h],
                                              preferred_element_type=F32)
        m_ref[...] = m_new
        l_ref[...] = l_new
        acc_ref[...] = jnp.concatenate(parts, axis=0)

    qbd = qbd_ref[...]

    @pl.when(step < n_steps)
    def _pages():
        scores, values = [], []
        for kr, vr in zip(k_refs, v_refs):
            scores.append(jnp.dot(qbd, kr[...].astype(BF16), preferred_element_type=F32))
            values.append([vr[pl.ds(h, LANES, stride=N_HEADS), :].astype(BF16) for h in range(N_HEADS)])
        update(scores, values)

    @pl.when(step == n_steps)
    def _new_tokens():
        s = jnp.dot(qbd, ktn_ref[...], preferred_element_type=F32)
        tok = lax.broadcasted_iota(jnp.int32, s.shape, 0) % SUBLANES
        col = lax.broadcasted_iota(jnp.int32, s.shape, 1)
        s = jnp.where((col <= tok) & (col < n_new), s, -jnp.inf)
        update([s], [[vn_ref[h] for h in range(N_HEADS)]])
        lam = _lambda_full(lq_ref[...], lam_init)
        acc = acc_ref[...]
        l = l_ref[...]
        g = g_ref[...]
        for h in range(N_HEADS):
            r0 = h * rows_per_head
            o1 = acc[r0:r0 + SUBLANES] / l[r0:r0 + SUBLANES]
            o2 = acc[r0 + SUBLANES:r0 + 2 * SUBLANES] / l[r0 + SUBLANES:r0 + 2 * SUBLANES]
            o = o1 - lam * o2
            o_ref[:, h * HEAD_W:(h + 1) * HEAD_W] = _rms(o, g, SUBLN_EPS) * (1.0 - lam_init)


def _paged_attention(page_table, lq, subln_g, qbd, ktn, vn, kcache, vcache, layer, n_new, lam_init):
    nb, n_pages = page_table.shape
    n_steps = n_pages // PAGES_PER_STEP
    page_rows = kcache.shape[2]

    def page_map(jj):
        def index_map(b, s, pt):
            return (layer, pt[b, jnp.minimum(s, n_steps - 1) * PAGES_PER_STEP + jj], 0, 0)
        return index_map

    per_b3 = lambda b, s, pt: (b, 0, 0)
    per_b4 = lambda b, s, pt: (b, 0, 0, 0)
    const = lambda b, s, pt: (0, 0)
    page_specs = [pl.BlockSpec((None, None, page_rows, LANES), page_map(jj)) for jj in range(PAGES_PER_STEP)]
    grid_spec = pltpu.PrefetchScalarGridSpec(
        num_scalar_prefetch=1,
        grid=(nb, n_steps + 1),
        in_specs=[pl.BlockSpec((4, QK_DIM), const), pl.BlockSpec((1, HEAD_W), const),
                  pl.BlockSpec((None, Q_ROWS, ATT_W), per_b3), pl.BlockSpec((None, ATT_W, LANES), per_b3),
                  pl.BlockSpec((None, N_HEADS, LANES, HEAD_W), per_b4)] + page_specs + page_specs,
        out_specs=pl.BlockSpec((None, SUBLANES, ATT_W), per_b3),
        scratch_shapes=[pltpu.VMEM((Q_ROWS, 1), F32), pltpu.VMEM((Q_ROWS, 1), F32),
                        pltpu.VMEM((Q_ROWS, HEAD_W), F32)],
    )
    return pl.pallas_call(
        functools.partial(_paged_kernel, n_steps=n_steps, n_new=n_new, lam_init=lam_init),
        grid_spec=grid_spec,
        out_shape=jax.ShapeDtypeStruct((nb, SUBLANES, ATT_W), F32),
        compiler_params=_cparams(("parallel", "arbitrary")),
        name="paged_diff_attn",
    )(page_table, lq, subln_g, qbd, ktn, vn, *([kcache] * PAGES_PER_STEP), *([vcache] * PAGES_PER_STEP))


CONV_CHUNK = 32
CONV_PAD = 32


def _conv_kernel(ug_ref, hist_ref, w_ref, b_ref, lg_ref, lb_ref, c_ref, nb_ref, full_ref):
    t = pl.program_id(1)
    tt = ug_ref.shape[0]
    h0 = CONV_PAD - CONV_HIST

    @pl.when(t == 0)
    def _first():
        full_ref[h0:CONV_PAD, :] = hist_ref[...]

    @pl.when(t > 0)
    def _carry():
        full_ref[h0:CONV_PAD, :] = full_ref[tt + h0:tt + CONV_PAD, :]

    ug = ug_ref[...]
    full_ref[CONV_PAD:CONV_PAD + tt, :] = ug[:, :CONV_CH] * _sigmoid(ug[:, CONV_CH:])
    chunk = min(tt, CONV_CHUNK)
    for r0 in range(0, tt, chunk):
        acc = jnp.zeros((chunk, CONV_CH), F32)
        for tap in range(CONV_TAPS):
            acc = acc + w_ref[tap:tap + 1, :] * full_ref[h0 + tap + r0:h0 + tap + r0 + chunk, :]
        c = acc + b_ref[...]
        mu = jnp.mean(c, axis=-1, keepdims=True)
        xc = c - mu
        y = xc * lax.rsqrt(jnp.mean(xc * xc, axis=-1, keepdims=True) + LN_EPS)
        y = y * lg_ref[...] + lb_ref[...]
        c_ref[r0:r0 + chunk, :] = (y * _sigmoid(y)).astype(BF16)

    @pl.when(t == pl.num_programs(1) - 1)
    def _tail():
        nb_ref[...] = full_ref[tt + h0:tt + CONV_PAD, :]


def _conv_branch(ug, hist, w, b, lg, lb, tt):
    bx, sx, _ = ug.shape
    row = lambda b_, t: (b_, t, 0)
    per_b = lambda b_, t: (b_, 0, 0)
    const = lambda b_, t: (0, 0)
    return pl.pallas_call(
        _conv_kernel,
        grid=(bx, sx // tt),
        in_specs=[pl.BlockSpec((None, tt, 2 * CONV_CH), row), pl.BlockSpec((None, CONV_HIST, CONV_CH), per_b),
                  pl.BlockSpec((CONV_TAPS, CONV_CH), const), pl.BlockSpec((1, CONV_CH), const),
                  pl.BlockSpec((1, CONV_CH), const), pl.BlockSpec((1, CONV_CH), const)],
        out_specs=[pl.BlockSpec((None, tt, CONV_CH), row), pl.BlockSpec((None, CONV_HIST, CONV_CH), per_b)],
        out_shape=[jax.ShapeDtypeStruct((bx, sx, CONV_CH), BF16), jax.ShapeDtypeStruct((bx, CONV_HIST, CONV_CH), F32)],
        scratch_shapes=[pltpu.VMEM((CONV_PAD + tt, CONV_CH), F32)],
        compiler_params=_cparams(("parallel", "arbitrary")),
        name="conv_branch",
    )(ug, hist, w, b, lg, lb)


def _outproj_kernel(*refs, widths):
    a_refs = refs[:len(widths)]
    w_ref, g_ref, h_ref, o_ref = refs[len(widths):]
    acc = None
    r0 = 0
    for a_ref, width in zip(a_refs, widths):
        part = jnp.dot(a_ref[...].astype(BF16), w_ref[r0:r0 + width, :], preferred_element_type=F32)
        acc = part if acc is None else acc + part
        r0 += width
    o_ref[...] = h_ref[...] + _rms(acc, g_ref[...], RMS_EPS)


def _outproj(a_list, a_specs, w_bf, g, h3, tm):
    bx, sx, d = h3.shape
    widths = tuple(spec.block_shape[-1] for spec in a_specs)
    row = lambda b, i: (b, i, 0)
    const = lambda b, i: (0, 0)
    return pl.pallas_call(
        functools.partial(_outproj_kernel, widths=widths),
        grid=(bx, sx // tm),
        in_specs=list(a_specs) + [pl.BlockSpec(w_bf.shape, const), pl.BlockSpec((1, d), const),
                                  pl.BlockSpec((None, tm, d), row)],
        out_specs=pl.BlockSpec((None, tm, d), row),
        out_shape=jax.ShapeDtypeStruct((bx, sx, d), F32),
        compiler_params=_cparams(("parallel", "parallel")),
        name="mixer_outproj",
    )(*a_list, w_bf, g, h3)


def _mlp_kernel(x_ref, gpre_ref, wu_ref, wd_ref, gpost_ref, o_ref, xn_ref, acc_ref):
    k = pl.program_id(1)

    @pl.when(k == 0)
    def _init():
        xn_ref[...] = _rms(x_ref[...], gpre_ref[...], RMS_EPS).astype(BF16)
        acc_ref[...] = jnp.zeros(acc_ref.shape, F32)

    up = jnp.dot(xn_ref[...], wu_ref[...], preferred_element_type=F32)
    act = jnp.square(jnp.maximum(up, 0.0)).astype(BF16)
    acc_ref[...] += jnp.dot(act, wd_ref[...], preferred_element_type=F32)

    @pl.when(k == pl.num_programs(1) - 1)
    def _done():
        o_ref[...] = x_ref[...] + _rms(acc_ref[...], gpost_ref[...], RMS_EPS)


def _mlp(x2, gpre, wu_bf, wd_bf, gpost, tm, tk):
    m, d = x2.shape
    dff = wu_bf.shape[1]
    return pl.pallas_call(
        _mlp_kernel,
        grid=(m // tm, dff // tk),
        in_specs=[pl.BlockSpec((tm, d), lambda i, k: (i, 0)), pl.BlockSpec((1, d), lambda i, k: (0, 0)),
                  pl.BlockSpec((d, tk), lambda i, k: (0, k)), pl.BlockSpec((tk, d), lambda i, k: (k, 0)),
                  pl.BlockSpec((1, d), lambda i, k: (0, 0))],
        out_specs=pl.BlockSpec((tm, d), lambda i, k: (i, 0)),
        out_shape=jax.ShapeDtypeStruct((m, d), F32),
        scratch_shapes=[pltpu.VMEM((tm, d), BF16), pltpu.VMEM((tm, d), F32)],
        compiler_params=_cparams(("parallel", "arbitrary")),
        name="sq_relu_mlp",
    )(x2, gpre, wu_bf, wd_bf, gpost)


def _odd_inproj_kernel(x_ref, g_ref, w_ref, u_ref):
    xn = _rms(x_ref[...], g_ref[...], RMS_EPS).astype(BF16)
    u_ref[...] = jnp.dot(xn, w_ref[...], preferred_element_type=F32)


def _odd_inproj(x3, g, w_bf, tm):
    bx, sx, d = x3.shape
    c = w_bf.shape[1]
    out = pl.pallas_call(
        _odd_inproj_kernel,
        grid=(bx, sx // tm),
        in_specs=[pl.BlockSpec((None, tm, d), lambda b, i: (b, i, 0)), pl.BlockSpec((1, d), lambda b, i: (0, 0)),
                  pl.BlockSpec((d, c), lambda b, i: (0, 0))],
        out_specs=pl.BlockSpec((tm, c), lambda b, i: (i, b)),
        out_shape=jax.ShapeDtypeStruct((sx, bx * c), F32),
        compiler_params=_cparams(("parallel", "parallel")),
        name="odd_inproj",
    )(x3, g, w_bf)
    return out.reshape(sx * bx, c)


def _s5_prep_kernel(lr_ref, li_ref, ldt_ref, br_ref, bi_ref, abr_ref, abi_ref, wre_ref, wim_ref):
    lr = lr_ref[...]
    li = li_ref[...]
    dt = jnp.exp(ldt_ref[...])
    mag = jnp.exp(lr * dt)
    ang = li * dt
    abr = mag * jnp.cos(ang)
    abi = mag * jnp.sin(ang)
    den = lr * lr + li * li
    nr = abr - 1.0
    ni = abi
    fr = (nr * lr + ni * li) / den
    fi = (ni * lr - nr * li) / den
    abr_ref[...] = abr
    abi_ref[...] = abi
    for j in range(lr.shape[0]):
        frj = fr[j:j + 1, :]
        fij = fi[j:j + 1, :]
        br = br_ref[j]
        bi = bi_ref[j]
        wre_ref[j] = (frj * br - fij * bi).astype(BF16)
        wim_ref[j] = (frj * bi + fij * br).astype(BF16)


def _s5_prep(lr, li, ldt, br_bd, bi_bd):
    nt = lr.shape[0]
    vm = pl.BlockSpec(memory_space=pltpu.VMEM)
    return pl.pallas_call(
        _s5_prep_kernel,
        in_specs=[vm] * 5,
        out_specs=[vm] * 4,
        out_shape=[jax.ShapeDtypeStruct((nt, STATE_TILE), F32), jax.ShapeDtypeStruct((nt, STATE_TILE), F32),
                   jax.ShapeDtypeStruct((nt, LANES, STATE_TILE), BF16),
                   jax.ShapeDtypeStruct((nt, LANES, STATE_TILE), BF16)],
        name="s5_discretise",
    )(lr, li, ldt, br_bd, bi_bd)


SCAN_LANES = 512


def _scan_interleaved(bur_ref, bui_ref, cr_ref, ci_ref, abr_ref, abi_ref):
    rc, n_state = bur_ref.shape
    upper = lax.broadcasted_iota(jnp.int32, (SUBLANES, SCAN_LANES), 0) >= 4
    for g in range(n_state // SCAN_LANES):
        sl = slice(g * SCAN_LANES, (g + 1) * SCAN_LANES)
        ar = jnp.broadcast_to(abr_ref[:, sl], (SUBLANES, SCAN_LANES))
        ai = jnp.broadcast_to(abi_ref[:, sl], (SUBLANES, SCAN_LANES))
        a2r = ar * ar - ai * ai
        a2i = ar * ai + ai * ar
        m1r = jnp.where(upper, ar, 0.0)
        m1i = jnp.where(upper, ai, 0.0)
        acr = jnp.where(upper, a2r, ar)
        aci = jnp.where(upper, a2i, ai)

        def body(r, carry, sl=sl, m1r=m1r, m1i=m1i, acr=acr, aci=aci):
            pr, pi = carry
            rows = pl.ds(pl.multiple_of(r * SUBLANES, SUBLANES), SUBLANES)
            xr = bur_ref[rows, sl]
            xi = bui_ref[rows, sl]
            sr = pltpu.roll(xr, 4, 0)
            si = pltpu.roll(xi, 4, 0)
            x1r = xr + (m1r * sr - m1i * si)
            x1i = xi + (m1r * si + m1i * sr)
            x2r = x1r + (acr * pr - aci * pi)
            x2i = x1i + (acr * pi + aci * pr)
            bur_ref[rows, sl] = x2r
            bui_ref[rows, sl] = x2i
            npr = jnp.where(upper, x2r, pltpu.roll(x2r, 4, 0))
            npi = jnp.where(upper, x2i, pltpu.roll(x2i, 4, 0))
            return npr, npi

        pr, pi = lax.fori_loop(0, rc // SUBLANES, body, (cr_ref[:, sl], ci_ref[:, sl]))
        cr_ref[:, sl] = pr
        ci_ref[:, sl] = pi


def _scan_dense(bur_ref, bui_ref, cr_ref, ci_ref, abr_ref, abi_ref, bsz):
    rc, n_state = bur_ref.shape
    for g in range(n_state // SCAN_LANES):
        sl = slice(g * SCAN_LANES, (g + 1) * SCAN_LANES)
        ar = abr_ref[:, sl]
        ai = abi_ref[:, sl]
        pr = cr_ref[:, sl]
        pi = ci_ref[:, sl]
        for t in range(rc // bsz):
            rows = slice(t * bsz, (t + 1) * bsz)
            nr = bur_ref[rows, sl] + (ar * pr - ai * pi)
            ni = bui_ref[rows, sl] + (ar * pi + ai * pr)
            bur_ref[rows, sl] = nr
            bui_ref[rows, sl] = ni
            pr, pi = nr, ni
        cr_ref[:, sl] = pr
        ci_ref[:, sl] = pi


def _s5_kernel(u_ref, h0r_ref, h0i_ref, abr_ref, abi_ref, wre_ref, wim_ref, cre_ref, cim_ref, d_ref, wg_ref,
               z_ref, hr_ref, hi_ref, bur_ref, bui_ref, cr_ref, ci_ref, *, bsz):
    step = pl.program_id(0)
    n_tiles = wre_ref.shape[0]

    @pl.when(step == 0)
    def _init():
        cr_ref[...] = h0r_ref[...]
        ci_ref[...] = h0i_ref[...]

    u = u_ref[...]
    ub = u.astype(BF16)
    for j in range(n_tiles):
        uj = ub[:, j * LANES:(j + 1) * LANES]
        bur_ref[:, j * STATE_TILE:(j + 1) * STATE_TILE] = jnp.dot(uj, wre_ref[j], preferred_element_type=F32)
        bui_ref[:, j * STATE_TILE:(j + 1) * STATE_TILE] = jnp.dot(uj, wim_ref[j], preferred_element_type=F32)

    if bsz == 4:
        _scan_interleaved(bur_ref, bui_ref, cr_ref, ci_ref, abr_ref, abi_ref)
    else:
        _scan_dense(bur_ref, bui_ref, cr_ref, ci_ref, abr_ref, abi_ref, bsz)

    ys = []
    for j in range(n_tiles):
        hr = bur_ref[:, j * STATE_TILE:(j + 1) * STATE_TILE].astype(BF16)
        hi = bui_ref[:, j * STATE_TILE:(j + 1) * STATE_TILE].astype(BF16)
        ys.append(jnp.dot(hr, cre_ref[j], preferred_element_type=F32)
                  - jnp.dot(hi, cim_ref[j], preferred_element_type=F32))
    y = jnp.concatenate(ys, axis=-1) + d_ref[...] * u
    z = 0.5 * y * (1.0 + lax.erf(y * math.sqrt(0.5)))
    gate = jnp.dot(z.astype(BF16), wg_ref[...], preferred_element_type=F32)
    z_ref[...] = (z * _sigmoid(gate)).astype(BF16)

    @pl.when(step == pl.num_programs(0) - 1)
    def _state():
        hr_ref[...] = cr_ref[...]
        hi_ref[...] = ci_ref[...]


def _s5_core(u_tm, h0r, h0i, abr, abi, wre, wim, cre, cim, d_skip, wg_bf, bsz, rc):
    rows, c = u_tm.shape
    n_state = abr.shape[1]
    carry_rows = h0r.shape[0]
    const2 = lambda s: (0, 0)
    const3 = lambda s: (0, 0, 0)
    return pl.pallas_call(
        functools.partial(_s5_kernel, bsz=bsz),
        grid=(rows // rc,),
        in_specs=[pl.BlockSpec((rc, c), lambda s: (s, 0)),
                  pl.BlockSpec((carry_rows, n_state), const2), pl.BlockSpec((carry_rows, n_state), const2),
                  pl.BlockSpec((1, n_state), const2), pl.BlockSpec((1, n_state), const2),
                  pl.BlockSpec(wre.shape, const3), pl.BlockSpec(wim.shape, const3),
                  pl.BlockSpec(cre.shape, const3), pl.BlockSpec(cim.shape, const3),
                  pl.BlockSpec((1, c), const2), pl.BlockSpec(wg_bf.shape, const2)],
        out_specs=[pl.BlockSpec((rc, c), lambda s: (s, 0)),
                   pl.BlockSpec((carry_rows, n_state), const2), pl.BlockSpec((carry_rows, n_state), const2)],
        out_shape=[jax.ShapeDtypeStruct((rows, c), BF16), jax.ShapeDtypeStruct((carry_rows, n_state), F32),
                   jax.ShapeDtypeStruct((carry_rows, n_state), F32)],
        scratch_shapes=[pltpu.VMEM((rc, n_state), F32), pltpu.VMEM((rc, n_state), F32),
                        pltpu.VMEM((carry_rows, n_state), F32), pltpu.VMEM((carry_rows, n_state), F32)],
        compiler_params=_cparams(("arbitrary",)),
        name="s5_core",
    )(u_tm, h0r, h0i, abr, abi, wre, wim, cre, cim, d_skip, wg_bf)


def _block_diag_tiles(x, rows_first):
    nt, ng, a, b = x.shape
    eye = jnp.eye(ng, dtype=jnp.bool_)
    out = jnp.where(eye[None, :, None, :, None], x[:, :, :, None, :], jnp.zeros((), x.dtype))
    del rows_first
    return out.reshape(nt, ng * a, ng * b)


def _tile(n):
    for cand in (512, 256, 128, 64, 32, 16, 8):
        if n % cand == 0:
            return cand
    return n


def _even_layer_prompt(hp, layer, ew):
    b, s, d = hp.shape
    q, kt, ktb, v, vb, ug = _even_inproj(hp, ew["g_pre"], ew["w_in"], ew["wkt"], _tile(s))
    lam_init = _lambda_init(layer)
    attn = _flash_attention(q, ktb, vb, ew["lq"], ew["subln_g"], lam_init, _tile(s))
    hist = jnp.zeros((b, CONV_HIST, CONV_CH), F32)
    conv, new_buf = _conv_branch(ug, hist, ew["conv_w"], ew["conv_b"], ew["ln_g"], ew["ln_b"], _tile(s))
    tm = _tile(s)
    row = lambda b_, i: (b_, i, 0)
    specs = [pl.BlockSpec((None, tm, ATT_W), row), pl.BlockSpec((None, tm, CONV_CH), row)]
    hp = _outproj([attn, conv], specs, ew["w_out"], ew["g_post"], hp, tm)
    k_new = kt.reshape(b, N_HEADS, 2, QK_DIM, s).transpose(0, 4, 1, 2, 3)
    v_new = v.reshape(b, s, N_HEADS, HEAD_W)
    return hp, k_new, v_new, new_buf


def _even_layer_sample(hs, layer, idx, ew, kcache, vcache, page_table, state_conv):
    nb, n_new, d = hs.shape
    m = nb * n_new
    q, kt, _, v, _, ug = _even_inproj(hs.reshape(1, m, d), ew["g_pre"], ew["w_in"], ew["wkt"], _tile(m))
    lam_init = _lambda_init(layer)
    q5 = q.reshape(nb, n_new, N_HEADS, 2, QK_DIM).transpose(0, 2, 3, 1, 4)
    q5 = jnp.pad(q5, ((0, 0), (0, 0), (0, 0), (0, SUBLANES - n_new), (0, 0)))
    eye = jnp.eye(2 * N_HEADS, dtype=jnp.bool_).reshape(N_HEADS, 2, 1, N_HEADS, 2, 1)
    qbd = jnp.where(eye[None], q5[:, :, :, :, None, None, :], jnp.zeros((), BF16)).reshape(nb, Q_ROWS, ATT_W)
    ktn = kt.reshape(ATT_W, nb, n_new).transpose(1, 0, 2)
    ktn = jnp.pad(ktn, ((0, 0), (0, 0), (0, LANES - n_new))).astype(BF16)
    vn = v.reshape(nb, n_new, N_HEADS, HEAD_W).transpose(0, 2, 1, 3)
    vn = jnp.pad(vn, ((0, 0), (0, 0), (0, LANES - n_new), (0, 0))).astype(BF16)
    attn = _paged_attention(page_table, ew["lq"], ew["subln_g"], qbd, ktn, vn, kcache, vcache, idx, n_new, lam_init)
    attn = attn[:, :n_new, :]
    conv, new_buf = _conv_branch(ug.reshape(nb, n_new, 2 * CONV_CH), state_conv, ew["conv_w"], ew["conv_b"],
                                 ew["ln_g"], ew["ln_b"], n_new)
    tm = _tile(m)
    row = lambda b_, i: (b_, i, 0)
    specs = [pl.BlockSpec((None, tm, ATT_W), row), pl.BlockSpec((None, tm, CONV_CH), row)]
    out = _outproj([attn.reshape(1, m, ATT_W), conv.reshape(1, m, CONV_CH)], specs, ew["w_out"], ew["g_post"],
                   hs.reshape(1, m, d), tm)
    k_new = kt.reshape(N_HEADS, 2, QK_DIM, nb, n_new).transpose(3, 4, 0, 1, 2)
    v_new = v.reshape(nb, n_new, N_HEADS, HEAD_W)
    return out.reshape(nb, n_new, d), k_new, v_new, new_buf


def _odd_layer_prompt(hp, ow):
    b, s, d = hp.shape
    tm = _tile(s)
    u_tm = _odd_inproj(hp, ow["g_pre"], ow["w_in"], tm)
    n_state = ow["abr"].shape[1]
    zeros = jnp.zeros((SUBLANES, n_state), F32)
    z, hr, hi = _s5_core(u_tm, zeros, zeros, ow["abr"], ow["abi"], ow["wre"], ow["wim"], ow["cre"], ow["cim"],
                         ow["d"], ow["w_gate"], b, _tile(s * b))
    z2 = z.reshape(s, b * d)
    specs = [pl.BlockSpec((tm, d), lambda b_, i: (i, b_))]
    hp = _outproj([z2], specs, ow["w_out"], ow["g_post"], hp, tm)
    return hp, hr[:b], hi[:b]


def _odd_layer_sample(hs, ow, h0r, h0i):
    nb, n_new, d = hs.shape
    m = nb * n_new
    u_tm = _odd_inproj(hs.transpose(1, 0, 2).reshape(1, m, d), ow["g_pre"], ow["w_in"], _tile(m))
    z, hr, hi = _s5_core(u_tm, h0r, h0i, ow["abr"], ow["abi"], ow["wre"], ow["wim"], ow["cre"], ow["cim"],
                         ow["d"], ow["w_gate"], nb, m)
    z = z.reshape(n_new, nb, d).transpose(1, 0, 2).reshape(1, m, d)
    tm = _tile(m)
    specs = [pl.BlockSpec((None, tm, d), lambda b_, i: (b_, i, 0))]
    out = _outproj([z], specs, ow["w_out"], ow["g_post"], hs.reshape(1, m, d), tm)
    return out.reshape(nb, n_new, d), hr, hi


def kernel(x_prompt, x_sample, cache_k, cache_v, page_table, state_conv, state_ssm_re, state_ssm_im, g_mix_pre, g_mix_post, g_ffn_pre, g_ffn_post, w_in_even, lambda_qk, subln_g, conv_w, conv_b, conv_ln_g, conv_ln_b, w_out_even, w_in_odd, ssm_a_re, ssm_a_im, ssm_b_re, ssm_b_im, ssm_c_re, ssm_c_im, ssm_d, ssm_log_dt, w_gate_odd, w_out_odd, w_ffn_up, w_ffn_down):
    depth, d = g_mix_pre.shape
    hp, hs = x_prompt, x_sample
    nb, n_new, _ = x_sample.shape
    n_layers_even, n_phys, page = cache_k.shape[:3]
    n_groups, n_p = ssm_a_re.shape[1:]
    n_tiles = n_groups // GROUPS_PER_TILE
    n_state = n_groups * n_p

    kcache = cache_k.transpose(0, 1, 3, 4, 5, 2).reshape(n_layers_even, n_phys, ATT_W, page)
    vcache = cache_v.reshape(n_layers_even, n_phys, page * N_HEADS, HEAD_W)

    row = lambda x: x.reshape(1, -1)
    outs = {k: [] for k in ("kp", "vp", "ks", "vs", "cp", "cs", "srp", "sip", "srs", "sis")}
    for layer in range(depth):
        i = layer // 2
        if layer % 2 == 0:
            w_in = w_in_even[i].astype(BF16)
            ew = dict(g_pre=row(g_mix_pre[layer]), g_post=row(g_mix_post[layer]), w_in=w_in,
                      wkt=w_in[:, ATT_W:2 * ATT_W].T, lq=lambda_qk[i], subln_g=row(subln_g[i]),
                      conv_w=conv_w[i], conv_b=row(conv_b[i]), ln_g=row(conv_ln_g[i]), ln_b=row(conv_ln_b[i]),
                      w_out=w_out_even[i].astype(BF16))
            hp, k_new, v_new, buf = _even_layer_prompt(hp, layer, ew)
            outs["kp"].append(k_new); outs["vp"].append(v_new); outs["cp"].append(buf)
            hs, k_new, v_new, buf = _even_layer_sample(hs, layer, i, ew, kcache, vcache, page_table, state_conv[i])
            outs["ks"].append(k_new); outs["vs"].append(v_new); outs["cs"].append(buf)
        else:
            lr = ssm_a_re[i].reshape(n_tiles, STATE_TILE)
            li = ssm_a_im[i].reshape(n_tiles, STATE_TILE)
            ldt = jnp.repeat(ssm_log_dt[i], n_p).reshape(n_tiles, STATE_TILE)
            b4 = lambda x: x.reshape(n_tiles, GROUPS_PER_TILE, n_p, GROUP_CH).transpose(0, 1, 3, 2)
            abr, abi, wre, wim = _s5_prep(lr, li, ldt, _block_diag_tiles(b4(ssm_b_re[i]), True),
                                          _block_diag_tiles(b4(ssm_b_im[i]), True))
            c4 = lambda x: x.reshape(n_tiles, GROUPS_PER_TILE, GROUP_CH, n_p).transpose(0, 1, 3, 2)
            ow = dict(g_pre=row(g_mix_pre[layer]), g_post=row(g_mix_post[layer]), w_in=w_in_odd[i].astype(BF16),
                      abr=abr.reshape(1, n_state), abi=abi.reshape(1, n_state), wre=wre, wim=wim,
                      cre=_block_diag_tiles(c4(ssm_c_re[i]), False).astype(BF16),
                      cim=_block_diag_tiles(c4(ssm_c_im[i]), False).astype(BF16),
                      d=row(ssm_d[i]), w_gate=w_gate_odd[i].astype(BF16), w_out=w_out_odd[i].astype(BF16))
            hp, sr, si = _odd_layer_prompt(hp, ow)
            outs["srp"].append(sr.reshape(-1, n_groups, n_p)); outs["sip"].append(si.reshape(-1, n_groups, n_p))
            hs, sr, si = _odd_layer_sample(hs, ow, state_ssm_re[i].reshape(nb, n_state),
                                           state_ssm_im[i].reshape(nb, n_state))
            outs["srs"].append(sr.reshape(nb, n_groups, n_p)); outs["sis"].append(si.reshape(nb, n_groups, n_p))
        wu = w_ffn_up[layer].astype(BF16)
        wd = w_ffn_down[layer].astype(BF16)
        gpre, gpost = row(g_ffn_pre[layer]), row(g_ffn_post[layer])
        mp = hp.shape[0] * hp.shape[1]
        hp = _mlp(hp.reshape(mp, d), gpre, wu, wd, gpost, 1024 if mp % 1024 == 0 else _tile(mp), 512).reshape(hp.shape)
        ms = nb * n_new
        hs = _mlp(hs.reshape(ms, d), gpre, wu, wd, gpost, _tile(ms), 512).reshape(hs.shape)
    st = jnp.stack
    return (hp, hs, st(outs["kp"]), st(outs["vp"]), st(outs["ks"]), st(outs["vs"]), st(outs["cp"]), st(outs["cs"]),
            st(outs["srp"]), st(outs["sip"]), st(outs["srs"]), st(outs["sis"]))
```

```python
import functools
import math

import jax
import jax.numpy as jnp
import numpy as np
from jax import lax
from jax.experimental import pallas as pl
from jax.experimental.pallas import tpu as pltpu

F32 = jnp.float32
BF16 = jnp.bfloat16

RMS_EPS = 1e-6
SUBLN_EPS = 1e-5
LN_EPS = 1e-5
LOG2_E = math.log2(math.e)

LANES = 128
SUBLANES = 8
N_HEADS = 4
QK_DIM = 64
HEAD_W = 2 * QK_DIM
ATT_W = N_HEADS * HEAD_W
CONV_CH = 512
CONV_TAPS = 31
CONV_HIST = CONV_TAPS - 1
GROUP_CH = 16
STATE_P = 64
GROUPS_PER_TILE = LANES // GROUP_CH
STATE_TILE = GROUPS_PER_TILE * STATE_P
VMEM_LIMIT = 52 * 1024 * 1024


def _cparams(semantics):
    return pltpu.CompilerParams(dimension_semantics=semantics, vmem_limit_bytes=VMEM_LIMIT)


def _rms(x, g, eps):
    ms = jnp.mean(x * x, axis=-1, keepdims=True)
    return (x * lax.rsqrt(ms + eps)) * g


def _sigmoid(x):
    return 1.0 / (1.0 + jnp.exp(-x))


def _lambda_full(lq, lam_init):
    s1 = jnp.sum(lq[0:1, :] * lq[1:2, :], axis=-1, keepdims=True)
    s2 = jnp.sum(lq[2:3, :] * lq[3:4, :], axis=-1, keepdims=True)
    return jnp.exp(s1) - jnp.exp(s2) + lam_init


def _lambda_init(layer):
    return 0.8 - 0.6 * math.exp(-0.3 * layer)


def _even_inproj_kernel(x_ref, g_ref, w_ref, wkt_ref, q_ref, kt_ref, ktb_ref, v_ref, vb_ref, ug_ref):
    xn = _rms(x_ref[...], g_ref[...], RMS_EPS).astype(BF16)

    def proj(c0, width):
        return jnp.dot(xn, w_ref[:, c0:c0 + width], preferred_element_type=F32)

    q_ref[...] = (proj(0, ATT_W) * (QK_DIM ** -0.5 * LOG2_E)).astype(BF16)
    kt = lax.dot_general(wkt_ref[...], xn, (((1,), (1,)), ((), ())), preferred_element_type=F32)
    kt_ref[...] = kt
    ktb_ref[...] = kt.astype(BF16)
    v = proj(2 * ATT_W, ATT_W)
    v_ref[...] = v
    vb_ref[...] = v.astype(BF16)
    ug_ref[...] = proj(3 * ATT_W, 2 * CONV_CH)


def _even_inproj(x3, g, w_bf, wkt_bf, tm):
    bx, sx, d = x3.shape
    n_in = w_bf.shape[1]
    row = lambda b, i: (b, i, 0)
    col = lambda b, i: (b, 0, i)
    const = lambda b, i: (0, 0)
    return pl.pallas_call(
        _even_inproj_kernel,
        grid=(bx, sx // tm),
        in_specs=[pl.BlockSpec((None, tm, d), row), pl.BlockSpec((1, d), const),
                  pl.BlockSpec((d, n_in), const), pl.BlockSpec((ATT_W, d), const)],
        out_specs=[pl.BlockSpec((None, tm, ATT_W), row), pl.BlockSpec((None, ATT_W, tm), col),
                   pl.BlockSpec((None, ATT_W, tm), col), pl.BlockSpec((None, tm, ATT_W), row),
                   pl.BlockSpec((None, tm, ATT_W), row), pl.BlockSpec((None, tm, 2 * CONV_CH), row)],
        out_shape=[jax.ShapeDtypeStruct((bx, sx, ATT_W), BF16), jax.ShapeDtypeStruct((bx, ATT_W, sx), F32),
                   jax.ShapeDtypeStruct((bx, ATT_W, sx), BF16), jax.ShapeDtypeStruct((bx, sx, ATT_W), F32),
                   jax.ShapeDtypeStruct((bx, sx, ATT_W), BF16), jax.ShapeDtypeStruct((bx, sx, 2 * CONV_CH), F32)],
        compiler_params=_cparams(("parallel", "parallel")),
        name="even_inproj",
    )(x3, g, w_bf, wkt_bf)


def _flash_kernel(it_ref, jt_ref, lq_ref, g_ref, q_ref, kt_ref, v_ref, o_ref, qm_ref, m_ref, l_ref, acc_ref,
                  *, lam_init):
    t = pl.program_id(2)
    i = it_ref[t]
    j = jt_ref[t]

    @pl.when(j == 0)
    def _init():
        q = q_ref[...]
        lane = lax.broadcasted_iota(jnp.int32, q.shape, 1)
        first = jnp.where(lane < QK_DIM, 1.0, 0.0).astype(BF16)
        qm_ref[0] = q * first
        qm_ref[1] = q * (1.0 - first)
        m_ref[...] = jnp.full(m_ref.shape, -jnp.inf, F32)
        l_ref[...] = jnp.zeros(l_ref.shape, F32)
        acc_ref[...] = jnp.zeros(acc_ref.shape, F32)

    def step(masked):
        kt = kt_ref[...]
        v = v_ref[...]
        n_blk = kt.shape[1] // LANES
        for c in range(2):
            s = jnp.dot(qm_ref[c], kt, preferred_element_type=F32)
            if masked:
                row = lax.broadcasted_iota(jnp.int32, s.shape, 0)
                col = lax.broadcasted_iota(jnp.int32, s.shape, 1)
                s = jnp.where(col <= row, s, -jnp.inf)
            blocks = [s[:, b * LANES:(b + 1) * LANES] for b in range(n_blk)]
            mx = functools.reduce(jnp.maximum, blocks)
            m_prev = m_ref[c]
            m_new = jnp.maximum(m_prev, jnp.max(mx, axis=-1, keepdims=True))
            alpha = jnp.exp2(m_prev - m_new)
            ps = [jnp.exp2(blk - m_new) for blk in blocks]
            l_ref[c] = alpha * l_ref[c] + functools.reduce(jnp.add, ps)
            p = jnp.concatenate([x.astype(BF16) for x in ps], axis=-1)
            acc_ref[c] = alpha * acc_ref[c] + jnp.dot(p, v, preferred_element_type=F32)
            m_ref[c] = m_new

    @pl.when(j < i)
    def _full():
        step(False)

    @pl.when(j == i)
    def _diag():
        step(True)
        lam = _lambda_full(lq_ref[...], lam_init)
        l0 = jnp.sum(l_ref[0], axis=-1, keepdims=True)
        l1 = jnp.sum(l_ref[1], axis=-1, keepdims=True)
        o = acc_ref[0] / l0 - lam * (acc_ref[1] / l1)
        o_ref[...] = (_rms(o, g_ref[...], SUBLN_EPS) * (1.0 - lam_init)).astype(BF16)


def _flash_attention(q, kt, v, lq, subln_g, lam_init, tq):
    b, s, _ = q.shape
    nq = s // tq
    pairs = [(i, j) for i in range(nq) for j in range(i + 1)]
    it = jnp.asarray(np.array([p[0] for p in pairs], np.int32))
    jt = jnp.asarray(np.array([p[1] for p in pairs], np.int32))
    grid_spec = pltpu.PrefetchScalarGridSpec(
        num_scalar_prefetch=2,
        grid=(b, N_HEADS, len(pairs)),
        in_specs=[pl.BlockSpec((4, QK_DIM), lambda b_, h, t, it_, jt_: (0, 0)),
                  pl.BlockSpec((1, HEAD_W), lambda b_, h, t, it_, jt_: (0, 0)),
                  pl.BlockSpec((None, tq, HEAD_W), lambda b_, h, t, it_, jt_: (b_, it_[t], h)),
                  pl.BlockSpec((None, HEAD_W, tq), lambda b_, h, t, it_, jt_: (b_, h, jt_[t])),
                  pl.BlockSpec((None, tq, HEAD_W), lambda b_, h, t, it_, jt_: (b_, jt_[t], h))],
        out_specs=pl.BlockSpec((None, tq, HEAD_W), lambda b_, h, t, it_, jt_: (b_, it_[t], h)),
        scratch_shapes=[pltpu.VMEM((2, tq, HEAD_W), BF16), pltpu.VMEM((2, tq, LANES), F32),
                        pltpu.VMEM((2, tq, LANES), F32), pltpu.VMEM((2, tq, HEAD_W), F32)],
    )
    return pl.pallas_call(
        functools.partial(_flash_kernel, lam_init=lam_init),
        grid_spec=grid_spec,
        out_shape=jax.ShapeDtypeStruct((b, s, ATT_W), BF16),
        compiler_params=_cparams(("parallel", "parallel", "arbitrary")),
        name="flash_diff_attn",
    )(it, jt, lq, subln_g, q, kt, v)


PAGES_PER_STEP = 16
Q_ROWS = N_HEADS * 2 * SUBLANES


def _paged_kernel(pt_ref, lq_ref, g_ref, qbd_ref, ktn_ref, vn_ref, *rest, n_steps, n_new, lam_init):
    k_refs = rest[:PAGES_PER_STEP]
    v_refs = rest[PAGES_PER_STEP:2 * PAGES_PER_STEP]
    o_ref, m_ref, l_ref, acc_ref = rest[2 * PAGES_PER_STEP:]
    del pt_ref
    step = pl.program_id(1)
    rows_per_head = 2 * SUBLANES

    @pl.when(step == 0)
    def _init():
        m_ref[...] = jnp.full(m_ref.shape, -jnp.inf, F32)
        l_ref[...] = jnp.zeros(l_ref.shape, F32)
        acc_ref[...] = jnp.zeros(acc_ref.shape, F32)

    def update(scores, values):
        m_prev = m_ref[...]
        m_new = m_prev
        for s in scores:
            m_new = jnp.maximum(m_new, jnp.max(s, axis=-1, keepdims=True))
        alpha = jnp.exp2(m_prev - m_new)
        l_new = alpha * l_ref[...]
        acc = alpha * acc_ref[...]
        parts = [acc[h * rows_per_head:(h + 1) * rows_per_head] for h in range(N_HEADS)]
        for s, vh in zip(scores, values):
            p = jnp.exp2(s - m_new)
            l_new = l_new + jnp.sum(p, axis=-1, keepdims=True)
            pb = p.astype(BF16)
            for h in range(N_HEADS):
                parts[h] = parts[h] + jnp.dot(pb[h * rows_per_head:(h + 1) * rows_per_head], vh[h],
                                              preferred_element_type=F32)
        m_ref[...] = m_new
        l_ref[...] = l_new
        acc_ref[...] = jnp.concatenate(parts, axis=0)

    qbd = qbd_ref[...]

    @pl.when(step < n_steps)
    def _pages():
        scores, values = [], []
        for kr, vr in zip(k_refs, v_refs):
            scores.append(jnp.dot(qbd, kr[...].astype(BF16), preferred_element_type=F32))
            values.append([vr[pl.ds(h, LANES, stride=N_HEADS), :].astype(BF16) for h in range(N_HEADS)])
        update(scores, values)

    @pl.when(step == n_steps)
    def _new_tokens():
        s = jnp.dot(qbd, ktn_ref[...], preferred_element_type=F32)
        tok = lax.broadcasted_iota(jnp.int32, s.shape, 0) % SUBLANES
        col = lax.broadcasted_iota(jnp.int32, s.shape, 1)
        s = jnp.where((col <= tok) & (col < n_new), s, -jnp.inf)
        update([s], [[vn_ref[h] for h in range(N_HEADS)]])
        lam = _lambda_full(lq_ref[...], lam_init)
        acc = acc_ref[...]
        l = l_ref[...]
        g = g_ref[...]
        for h in range(N_HEADS):
            r0 = h * rows_per_head
            o1 = acc[r0:r0 + SUBLANES] / l[r0:r0 + SUBLANES]
            o2 = acc[r0 + SUBLANES:r0 + 2 * SUBLANES] / l[r0 + SUBLANES:r0 + 2 * SUBLANES]
            o = o1 - lam * o2
            o_ref[:, h * HEAD_W:(h + 1) * HEAD_W] = _rms(o, g, SUBLN_EPS) * (1.0 - lam_init)


def _paged_attention(page_table, lq, subln_g, qbd, ktn, vn, kcache, vcache, layer, n_new, lam_init):
    nb, n_pages = page_table.shape
    n_steps = n_pages // PAGES_PER_STEP
    page_rows = kcache.shape[2]

    def page_map(jj):
        def index_map(b, s, pt):
            return (layer, pt[b, jnp.minimum(s, n_steps - 1) * PAGES_PER_STEP + jj], 0, 0)
        return index_map

    per_b3 = lambda b, s, pt: (b, 0, 0)
    per_b4 = lambda b, s, pt: (b, 0, 0, 0)
    const = lambda b, s, pt: (0, 0)
    page_specs = [pl.BlockSpec((None, None, page_rows, LANES), page_map(jj)) for jj in range(PAGES_PER_STEP)]
    grid_spec = pltpu.PrefetchScalarGridSpec(
        num_scalar_prefetch=1,
        grid=(nb, n_steps + 1),
        in_specs=[pl.BlockSpec((4, QK_DIM), const), pl.BlockSpec((1, HEAD_W), const),
                  pl.BlockSpec((None, Q_ROWS, ATT_W), per_b3), pl.BlockSpec((None, ATT_W, LANES), per_b3),
                  pl.BlockSpec((None, N_HEADS, LANES, HEAD_W), per_b4)] + page_specs + page_specs,
        out_specs=pl.BlockSpec((None, SUBLANES, ATT_W), per_b3),
        scratch_shapes=[pltpu.VMEM((Q_ROWS, 1), F32), pltpu.VMEM((Q_ROWS, 1), F32),
                        pltpu.VMEM((Q_ROWS, HEAD_W), F32)],
    )
    return pl.pallas_call(
        functools.partial(_paged_kernel, n_steps=n_steps, n_new=n_new, lam_init=lam_init),
        grid_spec=grid_spec,
        out_shape=jax.ShapeDtypeStruct((nb, SUBLANES, ATT_W), F32),
        compiler_params=_cparams(("parallel", "arbitrary")),
        name="paged_diff_attn",
    )(page_table, lq, subln_g, qbd, ktn, vn, *([kcache] * PAGES_PER_STEP), *([vcache] * PAGES_PER_STEP))


CONV_CHUNK = 32
CONV_PAD = 32


def _conv_kernel(ug_ref, hist_ref, w_ref, b_ref, lg_ref, lb_ref, c_ref, nb_ref, full_ref, shift_ref):
    t = pl.program_id(1)
    tt = ug_ref.shape[0]
    h0 = CONV_PAD - CONV_HIST

    @pl.when(t == 0)
    def _first():
        full_ref[h0:CONV_PAD, :] = hist_ref[...]

    @pl.when(t > 0)
    def _carry():
        full_ref[h0:CONV_PAD, :] = full_ref[tt + h0:tt + CONV_PAD, :]

    ug = ug_ref[...]
    full_ref[CONV_PAD:CONV_PAD + tt, :] = ug[:, :CONV_CH] * _sigmoid(ug[:, CONV_CH:])
    n_shift = shift_ref.shape[1]
    for phase in range(1, SUBLANES):
        shift_ref[phase - 1] = full_ref[phase:phase + n_shift, :]
    chunk = min(tt, CONV_CHUNK)
    for r0 in range(0, tt, chunk):
        acc = jnp.zeros((chunk, CONV_CH), F32)
        for tap in range(CONV_TAPS):
            tile_row, phase = divmod(h0 + tap, SUBLANES)
            start = tile_row * SUBLANES + r0
            if phase == 0:
                window = full_ref[start:start + chunk, :]
            else:
                window = shift_ref[phase - 1, start:start + chunk, :]
            acc = acc + w_ref[tap:tap + 1, :] * window
        c = acc + b_ref[...]
        mu = jnp.mean(c, axis=-1, keepdims=True)
        xc = c - mu
        y = xc * lax.rsqrt(jnp.mean(xc * xc, axis=-1, keepdims=True) + LN_EPS)
        y = y * lg_ref[...] + lb_ref[...]
        c_ref[r0:r0 + chunk, :] = (y * _sigmoid(y)).astype(BF16)

    @pl.when(t == pl.num_programs(1) - 1)
    def _tail():
        nb_ref[...] = full_ref[tt + h0:tt + CONV_PAD, :]


def _conv_branch(ug, hist, w, b, lg, lb, tt):
    bx, sx, _ = ug.shape
    row = lambda b_, t: (b_, t, 0)
    per_b = lambda b_, t: (b_, 0, 0)
    const = lambda b_, t: (0, 0)
    return pl.pallas_call(
        _conv_kernel,
        grid=(bx, sx // tt),
        in_specs=[pl.BlockSpec((None, tt, 2 * CONV_CH), row), pl.BlockSpec((None, CONV_HIST, CONV_CH), per_b),
                  pl.BlockSpec((CONV_TAPS, CONV_CH), const), pl.BlockSpec((1, CONV_CH), const),
                  pl.BlockSpec((1, CONV_CH), const), pl.BlockSpec((1, CONV_CH), const)],
        out_specs=[pl.BlockSpec((None, tt, CONV_CH), row), pl.BlockSpec((None, CONV_HIST, CONV_CH), per_b)],
        out_shape=[jax.ShapeDtypeStruct((bx, sx, CONV_CH), BF16), jax.ShapeDtypeStruct((bx, CONV_HIST, CONV_CH), F32)],
        scratch_shapes=[pltpu.VMEM((CONV_PAD + tt, CONV_CH), F32),
                        pltpu.VMEM((SUBLANES - 1, CONV_PAD - SUBLANES + tt, CONV_CH), F32)],
        compiler_params=_cparams(("parallel", "arbitrary")),
        name="conv_branch",
    )(ug, hist, w, b, lg, lb)


def _outproj_kernel(*refs, widths):
    a_refs = refs[:len(widths)]
    w_ref, g_ref, h_ref, o_ref = refs[len(widths):]
    acc = None
    r0 = 0
    for a_ref, width in zip(a_refs, widths):
        part = jnp.dot(a_ref[...].astype(BF16), w_ref[r0:r0 + width, :], preferred_element_type=F32)
        acc = part if acc is None else acc + part
        r0 += width
    o_ref[...] = h_ref[...] + _rms(acc, g_ref[...], RMS_EPS)


def _outproj(a_list, a_specs, w_bf, g, h3, tm):
    bx, sx, d = h3.shape
    widths = tuple(spec.block_shape[-1] for spec in a_specs)
    row = lambda b, i: (b, i, 0)
    const = lambda b, i: (0, 0)
    return pl.pallas_call(
        functools.partial(_outproj_kernel, widths=widths),
        grid=(bx, sx // tm),
        in_specs=list(a_specs) + [pl.BlockSpec(w_bf.shape, const), pl.BlockSpec((1, d), const),
                                  pl.BlockSpec((None, tm, d), row)],
        out_specs=pl.BlockSpec((None, tm, d), row),
        out_shape=jax.ShapeDtypeStruct((bx, sx, d), F32),
        compiler_params=_cparams(("parallel", "parallel")),
        name="mixer_outproj",
    )(*a_list, w_bf, g, h3)


def _mlp_kernel(x_ref, gpre_ref, wu_ref, wd_ref, gpost_ref, o_ref, xn_ref, acc_ref):
    k = pl.program_id(1)

    @pl.when(k == 0)
    def _init():
        xn_ref[...] = _rms(x_ref[...], gpre_ref[...], RMS_EPS).astype(BF16)
        acc_ref[...] = jnp.zeros(acc_ref.shape, F32)

    up = jnp.dot(xn_ref[...], wu_ref[...], preferred_element_type=F32)
    act = jnp.square(jnp.maximum(up, 0.0)).astype(BF16)
    acc_ref[...] += jnp.dot(act, wd_ref[...], preferred_element_type=F32)

    @pl.when(k == pl.num_programs(1) - 1)
    def _done():
        o_ref[...] = x_ref[...] + _rms(acc_ref[...], gpost_ref[...], RMS_EPS)


def _mlp(x2, gpre, wu_bf, wd_bf, gpost, tm, tk):
    m, d = x2.shape
    dff = wu_bf.shape[1]
    return pl.pallas_call(
        _mlp_kernel,
        grid=(m // tm, dff // tk),
        in_specs=[pl.BlockSpec((tm, d), lambda i, k: (i, 0)), pl.BlockSpec((1, d), lambda i, k: (0, 0)),
                  pl.BlockSpec((d, tk), lambda i, k: (0, k)), pl.BlockSpec((tk, d), lambda i, k: (k, 0)),
                  pl.BlockSpec((1, d), lambda i, k: (0, 0))],
        out_specs=pl.BlockSpec((tm, d), lambda i, k: (i, 0)),
        out_shape=jax.ShapeDtypeStruct((m, d), F32),
        scratch_shapes=[pltpu.VMEM((tm, d), BF16), pltpu.VMEM((tm, d), F32)],
        compiler_params=_cparams(("parallel", "arbitrary")),
        name="sq_relu_mlp",
    )(x2, gpre, wu_bf, wd_bf, gpost)


def _odd_inproj_kernel(x_ref, g_ref, w_ref, u_ref):
    xn = _rms(x_ref[...], g_ref[...], RMS_EPS).astype(BF16)
    u_ref[...] = jnp.dot(xn, w_ref[...], preferred_element_type=F32)


def _odd_inproj(x3, g, w_bf, tm):
    bx, sx, d = x3.shape
    c = w_bf.shape[1]
    out = pl.pallas_call(
        _odd_inproj_kernel,
        grid=(bx, sx // tm),
        in_specs=[pl.BlockSpec((None, tm, d), lambda b, i: (b, i, 0)), pl.BlockSpec((1, d), lambda b, i: (0, 0)),
                  pl.BlockSpec((d, c), lambda b, i: (0, 0))],
        out_specs=pl.BlockSpec((tm, c), lambda b, i: (i, b)),
        out_shape=jax.ShapeDtypeStruct((sx, bx * c), F32),
        compiler_params=_cparams(("parallel", "parallel")),
        name="odd_inproj",
    )(x3, g, w_bf)
    return out.reshape(sx * bx, c)


def _s5_prep_kernel(lr_ref, li_ref, ldt_ref, br_ref, bi_ref, abr_ref, abi_ref, wre_ref, wim_ref):
    lr = lr_ref[...]
    li = li_ref[...]
    dt = jnp.exp(ldt_ref[...])
    mag = jnp.exp(lr * dt)
    ang = li * dt
    abr = mag * jnp.cos(ang)
    abi = mag * jnp.sin(ang)
    den = lr * lr + li * li
    nr = abr - 1.0
    ni = abi
    fr = (nr * lr + ni * li) / den
    fi = (ni * lr - nr * li) / den
    abr_ref[...] = abr
    abi_ref[...] = abi
    for j in range(lr.shape[0]):
        frj = fr[j:j + 1, :]
        fij = fi[j:j + 1, :]
        br = br_ref[j]
        bi = bi_ref[j]
        wr = frj * br - fij * bi
        wi = frj * bi + fij * br
        wre_ref[j, :LANES, :] = wr.astype(BF16)
        wim_ref[j, :LANES, :] = wi.astype(BF16)
        arj = abr[j:j + 1, :]
        aij = abi[j:j + 1, :]
        wre_ref[j, LANES:, :] = (arj * wr - aij * wi).astype(BF16)
        wim_ref[j, LANES:, :] = (arj * wi + aij * wr).astype(BF16)


def _s5_prep(lr, li, ldt, br_bd, bi_bd):
    nt = lr.shape[0]
    vm = pl.BlockSpec(memory_space=pltpu.VMEM)
    return pl.pallas_call(
        _s5_prep_kernel,
        in_specs=[vm] * 5,
        out_specs=[vm] * 4,
        out_shape=[jax.ShapeDtypeStruct((nt, STATE_TILE), F32), jax.ShapeDtypeStruct((nt, STATE_TILE), F32),
                   jax.ShapeDtypeStruct((nt, 2 * LANES, STATE_TILE), BF16),
                   jax.ShapeDtypeStruct((nt, 2 * LANES, STATE_TILE), BF16)],
        name="s5_discretise",
    )(lr, li, ldt, br_bd, bi_bd)


SCAN_LANES = 512


def _scan_interleaved(bur_ref, bui_ref, cr_ref, ci_ref, abr_ref, abi_ref):
    rc, n_state = bur_ref.shape
    upper = lax.broadcasted_iota(jnp.int32, (SUBLANES, SCAN_LANES), 0) >= 4
    for g in range(n_state // SCAN_LANES):
        sl = slice(g * SCAN_LANES, (g + 1) * SCAN_LANES)
        ar = jnp.broadcast_to(abr_ref[:, sl], (SUBLANES, SCAN_LANES))
        ai = jnp.broadcast_to(abi_ref[:, sl], (SUBLANES, SCAN_LANES))
        a2r = ar * ar - ai * ai
        a2i = ar * ai + ai * ar
        acr = jnp.where(upper, a2r, ar)
        aci = jnp.where(upper, a2i, ai)

        def body(r, carry, sl=sl, acr=acr, aci=aci):
            pr, pi = carry
            rows = pl.ds(pl.multiple_of(r * SUBLANES, SUBLANES), SUBLANES)
            x1r = bur_ref[rows, sl]
            x1i = bui_ref[rows, sl]
            x2r = x1r + (acr * pr - aci * pi)
            x2i = x1i + (acr * pi + aci * pr)
            bur_ref[rows, sl] = x2r
            bui_ref[rows, sl] = x2i
            npr = jnp.where(upper, x2r, pltpu.roll(x2r, 4, 0))
            npi = jnp.where(upper, x2i, pltpu.roll(x2i, 4, 0))
            return npr, npi

        pr, pi = lax.fori_loop(0, rc // SUBLANES, body, (cr_ref[:, sl], ci_ref[:, sl]))
        cr_ref[:, sl] = pr
        ci_ref[:, sl] = pi


def _scan_dense(bur_ref, bui_ref, cr_ref, ci_ref, abr_ref, abi_ref, bsz):
    rc, n_state = bur_ref.shape
    for g in range(n_state // SCAN_LANES):
        sl = slice(g * SCAN_LANES, (g + 1) * SCAN_LANES)
        ar = abr_ref[:, sl]
        ai = abi_ref[:, sl]
        pr = cr_ref[:, sl]
        pi = ci_ref[:, sl]
        for t in range(rc // bsz):
            rows = slice(t * bsz, (t + 1) * bsz)
            nr = bur_ref[rows, sl] + (ar * pr - ai * pi)
            ni = bui_ref[rows, sl] + (ar * pi + ai * pr)
            bur_ref[rows, sl] = nr
            bui_ref[rows, sl] = ni
            pr, pi = nr, ni
        cr_ref[:, sl] = pr
        ci_ref[:, sl] = pi


def _s5_kernel(u_ref, h0r_ref, h0i_ref, abr_ref, abi_ref, wre_ref, wim_ref, cre_ref, cim_ref, d_ref, wg_ref,
               z_ref, hr_ref, hi_ref, bur_ref, bui_ref, cr_ref, ci_ref, *, bsz):
    step = pl.program_id(0)
    n_tiles = wre_ref.shape[0]

    @pl.when(step == 0)
    def _init():
        cr_ref[...] = h0r_ref[...]
        ci_ref[...] = h0i_ref[...]

    u = u_ref[...]
    ub = u.astype(BF16)
    if bsz == 4:
        row = lax.broadcasted_iota(jnp.int32, u.shape, 0)
        ub_prev = jnp.where((row & 4) != 0, pltpu.roll(u, 4, 0), 0.0).astype(BF16)
    for j in range(n_tiles):
        cols = slice(j * LANES, (j + 1) * LANES)
        states = slice(j * STATE_TILE, (j + 1) * STATE_TILE)
        if bsz == 4:
            lhs = jnp.concatenate([ub[:, cols], ub_prev[:, cols]], axis=-1)
            bur_ref[:, states] = jnp.dot(lhs, wre_ref[j], preferred_element_type=F32)
            bui_ref[:, states] = jnp.dot(lhs, wim_ref[j], preferred_element_type=F32)
        else:
            bur_ref[:, states] = jnp.dot(ub[:, cols], wre_ref[j, :LANES, :], preferred_element_type=F32)
            bui_ref[:, states] = jnp.dot(ub[:, cols], wim_ref[j, :LANES, :], preferred_element_type=F32)

    if bsz == 4:
        _scan_interleaved(bur_ref, bui_ref, cr_ref, ci_ref, abr_ref, abi_ref)
    else:
        _scan_dense(bur_ref, bui_ref, cr_ref, ci_ref, abr_ref, abi_ref, bsz)

    ys = []
    for j in range(n_tiles):
        hr = bur_ref[:, j * STATE_TILE:(j + 1) * STATE_TILE].astype(BF16)
        hi = bui_ref[:, j * STATE_TILE:(j + 1) * STATE_TILE].astype(BF16)
        ys.append(jnp.dot(hr, cre_ref[j], preferred_element_type=F32)
                  - jnp.dot(hi, cim_ref[j], preferred_element_type=F32))
    y = jnp.concatenate(ys, axis=-1) + d_ref[...] * u
    z = 0.5 * y * (1.0 + lax.erf(y * math.sqrt(0.5)))
    gate = jnp.dot(z.astype(BF16), wg_ref[...], preferred_element_type=F32)
    z_ref[...] = (z * _sigmoid(gate)).astype(BF16)

    @pl.when(step == pl.num_programs(0) - 1)
    def _state():
        hr_ref[...] = cr_ref[...]
        hi_ref[...] = ci_ref[...]


def _s5_core(u_tm, h0r, h0i, abr, abi, wre, wim, cre, cim, d_skip, wg_bf, bsz, rc):
    rows, c = u_tm.shape
    n_state = abr.shape[1]
    carry_rows = h0r.shape[0]
    const2 = lambda s: (0, 0)
    const3 = lambda s: (0, 0, 0)
    return pl.pallas_call(
        functools.partial(_s5_kernel, bsz=bsz),
        grid=(rows // rc,),
        in_specs=[pl.BlockSpec((rc, c), lambda s: (s, 0)),
                  pl.BlockSpec((carry_rows, n_state), const2), pl.BlockSpec((carry_rows, n_state), const2),
                  pl.BlockSpec((1, n_state), const2), pl.BlockSpec((1, n_state), const2),
                  pl.BlockSpec(wre.shape, const3), pl.BlockSpec(wim.shape, const3),
                  pl.BlockSpec(cre.shape, const3), pl.BlockSpec(cim.shape, const3),
                  pl.BlockSpec((1, c), const2), pl.BlockSpec(wg_bf.shape, const2)],
        out_specs=[pl.BlockSpec((rc, c), lambda s: (s, 0)),
                   pl.BlockSpec((carry_rows, n_state), const2), pl.BlockSpec((carry_rows, n_state), const2)],
        out_shape=[jax.ShapeDtypeStruct((rows, c), BF16), jax.ShapeDtypeStruct((carry_rows, n_state), F32),
                   jax.ShapeDtypeStruct((carry_rows, n_state), F32)],
        scratch_shapes=[pltpu.VMEM((rc, n_state), F32), pltpu.VMEM((rc, n_state), F32),
                        pltpu.VMEM((carry_rows, n_state), F32), pltpu.VMEM((carry_rows, n_state), F32)],
        compiler_params=_cparams(("arbitrary",)),
        name="s5_core",
    )(u_tm, h0r, h0i, abr, abi, wre, wim, cre, cim, d_skip, wg_bf)


def _block_diag_tiles(x, rows_first):
    nt, ng, a, b = x.shape
    eye = jnp.eye(ng, dtype=jnp.bool_)
    out = jnp.where(eye[None, :, None, :, None], x[:, :, :, None, :], jnp.zeros((), x.dtype))
    del rows_first
    return out.reshape(nt, ng * a, ng * b)


def _tile(n):
    for cand in (512, 256, 128, 64, 32, 16, 8):
        if n % cand == 0:
            return cand
    return n


def _even_layer_prompt(hp, layer, ew):
    b, s, d = hp.shape
    q, kt, ktb, v, vb, ug = _even_inproj(hp, ew["g_pre"], ew["w_in"], ew["wkt"], _tile(s))
    lam_init = _lambda_init(layer)
    attn = _flash_attention(q, ktb, vb, ew["lq"], ew["subln_g"], lam_init, _tile(s))
    hist = jnp.zeros((b, CONV_HIST, CONV_CH), F32)
    conv, new_buf = _conv_branch(ug, hist, ew["conv_w"], ew["conv_b"], ew["ln_g"], ew["ln_b"], _tile(s))
    tm = _tile(s)
    row = lambda b_, i: (b_, i, 0)
    specs = [pl.BlockSpec((None, tm, ATT_W), row), pl.BlockSpec((None, tm, CONV_CH), row)]
    hp = _outproj([attn, conv], specs, ew["w_out"], ew["g_post"], hp, tm)
    k_new = kt.reshape(b, N_HEADS, 2, QK_DIM, s).transpose(0, 4, 1, 2, 3)
    v_new = v.reshape(b, s, N_HEADS, HEAD_W)
    return hp, k_new, v_new, new_buf


def _even_layer_sample(hs, layer, idx, ew, kcache, vcache, page_table, state_conv):
    nb, n_new, d = hs.shape
    m = nb * n_new
    q, kt, _, v, _, ug = _even_inproj(hs.reshape(1, m, d), ew["g_pre"], ew["w_in"], ew["wkt"], _tile(m))
    lam_init = _lambda_init(layer)
    q5 = q.reshape(nb, n_new, N_HEADS, 2, QK_DIM).transpose(0, 2, 3, 1, 4)
    q5 = jnp.pad(q5, ((0, 0), (0, 0), (0, 0), (0, SUBLANES - n_new), (0, 0)))
    eye = jnp.eye(2 * N_HEADS, dtype=jnp.bool_).reshape(N_HEADS, 2, 1, N_HEADS, 2, 1)
    qbd = jnp.where(eye[None], q5[:, :, :, :, None, None, :], jnp.zeros((), BF16)).reshape(nb, Q_ROWS, ATT_W)
    ktn = kt.reshape(ATT_W, nb, n_new).transpose(1, 0, 2)
    ktn = jnp.pad(ktn, ((0, 0), (0, 0), (0, LANES - n_new))).astype(BF16)
    vn = v.reshape(nb, n_new, N_HEADS, HEAD_W).transpose(0, 2, 1, 3)
    vn = jnp.pad(vn, ((0, 0), (0, 0), (0, LANES - n_new), (0, 0))).astype(BF16)
    attn = _paged_attention(page_table, ew["lq"], ew["subln_g"], qbd, ktn, vn, kcache, vcache, idx, n_new, lam_init)
    attn = attn[:, :n_new, :]
    conv, new_buf = _conv_branch(ug.reshape(nb, n_new, 2 * CONV_CH), state_conv, ew["conv_w"], ew["conv_b"],
                                 ew["ln_g"], ew["ln_b"], n_new)
    tm = _tile(m)
    row = lambda b_, i: (b_, i, 0)
    specs = [pl.BlockSpec((None, tm, ATT_W), row), pl.BlockSpec((None, tm, CONV_CH), row)]
    out = _outproj([attn.reshape(1, m, ATT_W), conv.reshape(1, m, CONV_CH)], specs, ew["w_out"], ew["g_post"],
                   hs.reshape(1, m, d), tm)
    k_new = kt.reshape(N_HEADS, 2, QK_DIM, nb, n_new).transpose(3, 4, 0, 1, 2)
    v_new = v.reshape(nb, n_new, N_HEADS, HEAD_W)
    return out.reshape(nb, n_new, d), k_new, v_new, new_buf


def _odd_layer_prompt(hp, ow):
    b, s, d = hp.shape
    tm = _tile(s)
    u_tm = _odd_inproj(hp, ow["g_pre"], ow["w_in"], tm)
    n_state = ow["abr"].shape[1]
    zeros = jnp.zeros((SUBLANES, n_state), F32)
    z, hr, hi = _s5_core(u_tm, zeros, zeros, ow["abr"], ow["abi"], ow["wre"], ow["wim"], ow["cre"], ow["cim"],
                         ow["d"], ow["w_gate"], b, _tile(s * b))
    z2 = z.reshape(s, b * d)
    specs = [pl.BlockSpec((tm, d), lambda b_, i: (i, b_))]
    hp = _outproj([z2], specs, ow["w_out"], ow["g_post"], hp, tm)
    return hp, hr[:b], hi[:b]


def _odd_layer_sample(hs, ow, h0r, h0i):
    nb, n_new, d = hs.shape
    m = nb * n_new
    u_tm = _odd_inproj(hs.transpose(1, 0, 2).reshape(1, m, d), ow["g_pre"], ow["w_in"], _tile(m))
    z, hr, hi = _s5_core(u_tm, h0r, h0i, ow["abr"], ow["abi"], ow["wre"], ow["wim"], ow["cre"], ow["cim"],
                         ow["d"], ow["w_gate"], nb, m)
    z = z.reshape(n_new, nb, d).transpose(1, 0, 2).reshape(1, m, d)
    tm = _tile(m)
    specs = [pl.BlockSpec((None, tm, d), lambda b_, i: (b_, i, 0))]
    out = _outproj([z], specs, ow["w_out"], ow["g_post"], hs.reshape(1, m, d), tm)
    return out.reshape(nb, n_new, d), hr, hi


def kernel(x_prompt, x_sample, cache_k, cache_v, page_table, state_conv, state_ssm_re, state_ssm_im, g_mix_pre, g_mix_post, g_ffn_pre, g_ffn_post, w_in_even, lambda_qk, subln_g, conv_w, conv_b, conv_ln_g, conv_ln_b, w_out_even, w_in_odd, ssm_a_re, ssm_a_im, ssm_b_re, ssm_b_im, ssm_c_re, ssm_c_im, ssm_d, ssm_log_dt, w_gate_odd, w_out_odd, w_ffn_up, w_ffn_down):
    depth, d = g_mix_pre.shape
    hp, hs = x_prompt, x_sample
    nb, n_new, _ = x_sample.shape
    n_layers_even, n_phys, page = cache_k.shape[:3]
    n_groups, n_p = ssm_a_re.shape[1:]
    n_tiles = n_groups // GROUPS_PER_TILE
    n_state = n_groups * n_p

    kcache = cache_k.transpose(0, 1, 3, 4, 5, 2).reshape(n_layers_even, n_phys, ATT_W, page)
    vcache = cache_v.reshape(n_layers_even, n_phys, page * N_HEADS, HEAD_W)

    row = lambda x: x.reshape(1, -1)
    outs = {k: [] for k in ("kp", "vp", "ks", "vs", "cp", "cs", "srp", "sip", "srs", "sis")}
    for layer in range(depth):
        i = layer // 2
        if layer % 2 == 0:
            w_in = w_in_even[i].astype(BF16)
            ew = dict(g_pre=row(g_mix_pre[layer]), g_post=row(g_mix_post[layer]), w_in=w_in,
                      wkt=w_in[:, ATT_W:2 * ATT_W].T, lq=lambda_qk[i], subln_g=row(subln_g[i]),
                      conv_w=conv_w[i], conv_b=row(conv_b[i]), ln_g=row(conv_ln_g[i]), ln_b=row(conv_ln_b[i]),
                      w_out=w_out_even[i].astype(BF16))
            hp, k_new, v_new, buf = _even_layer_prompt(hp, layer, ew)
            outs["kp"].append(k_new); outs["vp"].append(v_new); outs["cp"].append(buf)
            hs, k_new, v_new, buf = _even_layer_sample(hs, layer, i, ew, kcache, vcache, page_table, state_conv[i])
            outs["ks"].append(k_new); outs["vs"].append(v_new); outs["cs"].append(buf)
        else:
            lr = ssm_a_re[i].reshape(n_tiles, STATE_TILE)
            li = ssm_a_im[i].reshape(n_tiles, STATE_TILE)
            ldt = jnp.repeat(ssm_log_dt[i], n_p).reshape(n_tiles, STATE_TILE)
            b4 = lambda x: x.reshape(n_tiles, GROUPS_PER_TILE, n_p, GROUP_CH).transpose(0, 1, 3, 2)
            abr, abi, wre, wim = _s5_prep(lr, li, ldt, _block_diag_tiles(b4(ssm_b_re[i]), True),
                                          _block_diag_tiles(b4(ssm_b_im[i]), True))
            c4 = lambda x: x.reshape(n_tiles, GROUPS_PER_TILE, GROUP_CH, n_p).transpose(0, 1, 3, 2)
            ow = dict(g_pre=row(g_mix_pre[layer]), g_post=row(g_mix_post[layer]), w_in=w_in_odd[i].astype(BF16),
                      abr=abr.reshape(1, n_state), abi=abi.reshape(1, n_state), wre=wre, wim=wim,
                      cre=_block_diag_tiles(c4(ssm_c_re[i]), False).astype(BF16),
                      cim=_block_diag_tiles(c4(ssm_c_im[i]), False).astype(BF16),
                      d=row(ssm_d[i]), w_gate=w_gate_odd[i].astype(BF16), w_out=w_out_odd[i].astype(BF16))
            hp, sr, si = _odd_layer_prompt(hp, ow)
            outs["srp"].append(sr.reshape(-1, n_groups, n_p)); outs["sip"].append(si.reshape(-1, n_groups, n_p))
            hs, sr, si = _odd_layer_sample(hs, ow, state_ssm_re[i].reshape(nb, n_state),
                                           state_ssm_im[i].reshape(nb, n_state))
            outs["srs"].append(sr.reshape(nb, n_groups, n_p)); outs["sis"].append(si.reshape(nb, n_groups, n_p))
        wu = w_ffn_up[layer].astype(BF16)
        wd = w_ffn_down[layer].astype(BF16)
        gpre, gpost = row(g_ffn_pre[layer]), row(g_ffn_post[layer])
        mp = hp.shape[0] * hp.shape[1]
        hp = _mlp(hp.reshape(mp, d), gpre, wu, wd, gpost, 1024 if mp % 1024 == 0 else _tile(mp), 512).reshape(hp.shape)
        ms = nb * n_new
        hs = _mlp(hs.reshape(ms, d), gpre, wu, wd, gpost, _tile(ms), 512).reshape(hs.shape)
    st = jnp.stack
    return (hp, hs, st(outs["kp"]), st(outs["vp"]), st(outs["ks"]), st(outs["vs"]), st(outs["cp"]), st(outs["cs"]),
            st(outs["srp"]), st(outs["sip"]), st(outs["srs"]), st(outs["sis"]))
```

```python
import functools
import math

import jax
import jax.numpy as jnp
import numpy as np
from jax import lax
from jax.experimental import pallas as pl
from jax.experimental.pallas import tpu as pltpu

F32 = jnp.float32
BF16 = jnp.bfloat16

RMS_EPS = 1e-6
SUBLN_EPS = 1e-5
LN_EPS = 1e-5
LOG2_E = math.log2(math.e)

LANES = 128
SUBLANES = 8
N_HEADS = 4
QK_DIM = 64
HEAD_W = 2 * QK_DIM
ATT_W = N_HEADS * HEAD_W
CONV_CH = 512
CONV_TAPS = 31
CONV_HIST = CONV_TAPS - 1
GROUP_CH = 16
STATE_P = 64
GROUPS_PER_TILE = LANES // GROUP_CH
STATE_TILE = GROUPS_PER_TILE * STATE_P
VMEM_LIMIT = 52 * 1024 * 1024


def _cparams(semantics):
    return pltpu.CompilerParams(dimension_semantics=semantics, vmem_limit_bytes=VMEM_LIMIT)


def _rms(x, g, eps):
    ms = jnp.mean(x * x, axis=-1, keepdims=True)
    return (x * lax.rsqrt(ms + eps)) * g


def _sigmoid(x):
    return 1.0 / (1.0 + jnp.exp(-x))


def _lambda_full(lq, lam_init):
    s1 = jnp.sum(lq[0:1, :] * lq[1:2, :], axis=-1, keepdims=True)
    s2 = jnp.sum(lq[2:3, :] * lq[3:4, :], axis=-1, keepdims=True)
    return jnp.exp(s1) - jnp.exp(s2) + lam_init


def _lambda_init(layer):
    return 0.8 - 0.6 * math.exp(-0.3 * layer)


def _even_inproj_kernel(x_ref, g_ref, w_ref, wkt_ref, q_ref, kt_ref, ktb_ref, v_ref, vb_ref, ug_ref):
    xn = _rms(x_ref[...], g_ref[...], RMS_EPS).astype(BF16)

    def proj(c0, width):
        return jnp.dot(xn, w_ref[:, c0:c0 + width], preferred_element_type=F32)

    q_ref[...] = (proj(0, ATT_W) * (QK_DIM ** -0.5 * LOG2_E)).astype(BF16)
    kt = lax.dot_general(wkt_ref[...], xn, (((1,), (1,)), ((), ())), preferred_element_type=F32)
    kt_ref[...] = kt
    ktb_ref[...] = kt.astype(BF16)
    v = proj(2 * ATT_W, ATT_W)
    v_ref[...] = v
    vb_ref[...] = v.astype(BF16)
    ug_ref[...] = proj(3 * ATT_W, 2 * CONV_CH)


def _even_inproj(x3, g, w_bf, wkt_bf, tm):
    bx, sx, d = x3.shape
    n_in = w_bf.shape[1]
    row = lambda b, i: (b, i, 0)
    col = lambda b, i: (b, 0, i)
    const = lambda b, i: (0, 0)
    return pl.pallas_call(
        _even_inproj_kernel,
        grid=(bx, sx // tm),
        in_specs=[pl.BlockSpec((None, tm, d), row), pl.BlockSpec((1, d), const),
                  pl.BlockSpec((d, n_in), const), pl.BlockSpec((ATT_W, d), const)],
        out_specs=[pl.BlockSpec((None, tm, ATT_W), row), pl.BlockSpec((None, ATT_W, tm), col),
                   pl.BlockSpec((None, ATT_W, tm), col), pl.BlockSpec((None, tm, ATT_W), row),
                   pl.BlockSpec((None, tm, ATT_W), row), pl.BlockSpec((None, tm, 2 * CONV_CH), row)],
        out_shape=[jax.ShapeDtypeStruct((bx, sx, ATT_W), BF16), jax.ShapeDtypeStruct((bx, ATT_W, sx), F32),
                   jax.ShapeDtypeStruct((bx, ATT_W, sx), BF16), jax.ShapeDtypeStruct((bx, sx, ATT_W), F32),
                   jax.ShapeDtypeStruct((bx, sx, ATT_W), BF16), jax.ShapeDtypeStruct((bx, sx, 2 * CONV_CH), F32)],
        compiler_params=_cparams(("parallel", "parallel")),
        name="even_inproj",
    )(x3, g, w_bf, wkt_bf)


def _flash_kernel(it_ref, jt_ref, lq_ref, g_ref, q_ref, kt_ref, v_ref, o_ref, qm_ref, m_ref, l_ref, acc_ref,
                  *, lam_init):
    t = pl.program_id(1)
    i = it_ref[t]
    j = jt_ref[t]
    heads = [slice(h * HEAD_W, (h + 1) * HEAD_W) for h in range(N_HEADS)]

    @pl.when(j == 0)
    def _init():
        q = q_ref[...]
        lane = lax.broadcasted_iota(jnp.int32, (q.shape[0], HEAD_W), 1)
        first = jnp.where(lane < QK_DIM, 1.0, 0.0).astype(BF16)
        for h, hs in enumerate(heads):
            qm_ref[2 * h] = q[:, hs] * first
            qm_ref[2 * h + 1] = q[:, hs] * (1.0 - first)
        m_ref[...] = jnp.full(m_ref.shape, -jnp.inf, F32)
        l_ref[...] = jnp.zeros(l_ref.shape, F32)
        acc_ref[...] = jnp.zeros(acc_ref.shape, F32)

    def step(masked):
        n_blk = kt_ref.shape[1] // LANES
        for h, hs in enumerate(heads):
            kt = kt_ref[hs, :]
            v = v_ref[:, hs]
            for c in range(2):
                hc = 2 * h + c
                s = jnp.dot(qm_ref[hc], kt, preferred_element_type=F32)
                if masked:
                    row = lax.broadcasted_iota(jnp.int32, s.shape, 0)
                    col = lax.broadcasted_iota(jnp.int32, s.shape, 1)
                    s = jnp.where(col <= row, s, -jnp.inf)
                blocks = [s[:, b * LANES:(b + 1) * LANES] for b in range(n_blk)]
                mx = functools.reduce(jnp.maximum, blocks)
                m_prev = m_ref[hc]
                m_new = jnp.maximum(m_prev, jnp.max(mx, axis=-1, keepdims=True))
                alpha = jnp.exp2(m_prev - m_new)
                ps = [jnp.exp2(blk - m_new) for blk in blocks]
                l_ref[hc] = alpha * l_ref[hc] + functools.reduce(jnp.add, ps)
                p = jnp.concatenate([x.astype(BF16) for x in ps], axis=-1)
                acc_ref[hc] = alpha * acc_ref[hc] + jnp.dot(p, v, preferred_element_type=F32)
                m_ref[hc] = m_new

    @pl.when(j < i)
    def _full():
        step(False)

    @pl.when(j == i)
    def _diag():
        step(True)
        lam = _lambda_full(lq_ref[...], lam_init)
        for h, hs in enumerate(heads):
            l0 = jnp.sum(l_ref[2 * h], axis=-1, keepdims=True)
            l1 = jnp.sum(l_ref[2 * h + 1], axis=-1, keepdims=True)
            o = acc_ref[2 * h] / l0 - lam * (acc_ref[2 * h + 1] / l1)
            o_ref[:, hs] = (_rms(o, g_ref[...], SUBLN_EPS) * (1.0 - lam_init)).astype(BF16)


def _flash_attention(q, kt, v, lq, subln_g, lam_init, tq):
    b, s, _ = q.shape
    nq = s // tq
    pairs = [(i, j) for i in range(nq) for j in range(i + 1)]
    it = jnp.asarray(np.array([p[0] for p in pairs], np.int32))
    jt = jnp.asarray(np.array([p[1] for p in pairs], np.int32))
    grid_spec = pltpu.PrefetchScalarGridSpec(
        num_scalar_prefetch=2,
        grid=(b, len(pairs)),
        in_specs=[pl.BlockSpec((4, QK_DIM), lambda b_, t, it_, jt_: (0, 0)),
                  pl.BlockSpec((1, HEAD_W), lambda b_, t, it_, jt_: (0, 0)),
                  pl.BlockSpec((None, tq, ATT_W), lambda b_, t, it_, jt_: (b_, it_[t], 0)),
                  pl.BlockSpec((None, ATT_W, tq), lambda b_, t, it_, jt_: (b_, 0, jt_[t])),
                  pl.BlockSpec((None, tq, ATT_W), lambda b_, t, it_, jt_: (b_, jt_[t], 0))],
        out_specs=pl.BlockSpec((None, tq, ATT_W), lambda b_, t, it_, jt_: (b_, it_[t], 0)),
        scratch_shapes=[pltpu.VMEM((2 * N_HEADS, tq, HEAD_W), BF16), pltpu.VMEM((2 * N_HEADS, tq, LANES), F32),
                        pltpu.VMEM((2 * N_HEADS, tq, LANES), F32), pltpu.VMEM((2 * N_HEADS, tq, HEAD_W), F32)],
    )
    return pl.pallas_call(
        functools.partial(_flash_kernel, lam_init=lam_init),
        grid_spec=grid_spec,
        out_shape=jax.ShapeDtypeStruct((b, s, ATT_W), BF16),
        compiler_params=_cparams(("parallel", "arbitrary")),
        name="flash_diff_attn",
    )(it, jt, lq, subln_g, q, kt, v)


PAGES_PER_STEP = 16
Q_ROWS = N_HEADS * 2 * SUBLANES


def _paged_kernel(pt_ref, lq_ref, g_ref, qbd_ref, ktn_ref, vn_ref, *rest, n_steps, n_new, lam_init):
    k_refs = rest[:PAGES_PER_STEP]
    v_refs = rest[PAGES_PER_STEP:2 * PAGES_PER_STEP]
    o_ref, m_ref, l_ref, acc_ref = rest[2 * PAGES_PER_STEP:]
    del pt_ref
    step = pl.program_id(1)
    rows_per_head = 2 * SUBLANES

    @pl.when(step == 0)
    def _init():
        m_ref[...] = jnp.full(m_ref.shape, -jnp.inf, F32)
        l_ref[...] = jnp.zeros(l_ref.shape, F32)
        acc_ref[...] = jnp.zeros(acc_ref.shape, F32)

    def update(scores, values):
        m_prev = m_ref[...]
        m_new = m_prev
        for s in scores:
            m_new = jnp.maximum(m_new, jnp.max(s, axis=-1, keepdims=True))
        alpha = jnp.exp2(m_prev - m_new)
        l_new = alpha * l_ref[...]
        acc = alpha * acc_ref[...]
        parts = [acc[h * rows_per_head:(h + 1) * rows_per_head] for h in range(N_HEADS)]
        for s, vh in zip(scores, values):
            p = jnp.exp2(s - m_new)
            l_new = l_new + jnp.sum(p, axis=-1, keepdims=True)
            pb = p.astype(BF16)
            for h in range(N_HEADS):
                parts[h] = parts[h] + jnp.dot(pb[h * rows_per_head:(h + 1) * rows_per_head], vh[h],
                                              preferred_element_type=F32)
        m_ref[...] = m_new
        l_ref[...] = l_new
        acc_ref[...] = jnp.concatenate(parts, axis=0)

    qbd = qbd_ref[...]

    @pl.when(step < n_steps)
    def _pages():
        scores, values = [], []
        for kr, vr in zip(k_refs, v_refs):
            scores.append(jnp.dot(qbd, kr[...].astype(BF16), preferred_element_type=F32))
            values.append([vr[pl.ds(h, LANES, stride=N_HEADS), :].astype(BF16) for h in range(N_HEADS)])
        update(scores, values)

    @pl.when(step == n_steps)
    def _new_tokens():
        s = jnp.dot(qbd, ktn_ref[...], preferred_element_type=F32)
        tok = lax.broadcasted_iota(jnp.int32, s.shape, 0) % SUBLANES
        col = lax.broadcasted_iota(jnp.int32, s.shape, 1)
        s = jnp.where((col <= tok) & (col < n_new), s, -jnp.inf)
        update([s], [[vn_ref[h] for h in range(N_HEADS)]])
        lam = _lambda_full(lq_ref[...], lam_init)
        acc = acc_ref[...]
        l = l_ref[...]
        g = g_ref[...]
        for h in range(N_HEADS):
            r0 = h * rows_per_head
            o1 = acc[r0:r0 + SUBLANES] / l[r0:r0 + SUBLANES]
            o2 = acc[r0 + SUBLANES:r0 + 2 * SUBLANES] / l[r0 + SUBLANES:r0 + 2 * SUBLANES]
            o = o1 - lam * o2
            o_ref[:, h * HEAD_W:(h + 1) * HEAD_W] = _rms(o, g, SUBLN_EPS) * (1.0 - lam_init)


def _paged_attention(page_table, lq, subln_g, qbd, ktn, vn, kcache, vcache, layer, n_new, lam_init):
    nb, n_pages = page_table.shape
    n_steps = n_pages // PAGES_PER_STEP
    page_rows = kcache.shape[2]

    def page_map(jj):
        def index_map(b, s, pt):
            return (layer, pt[b, jnp.minimum(s, n_steps - 1) * PAGES_PER_STEP + jj], 0, 0)
        return index_map

    per_b3 = lambda b, s, pt: (b, 0, 0)
    per_b4 = lambda b, s, pt: (b, 0, 0, 0)
    const = lambda b, s, pt: (0, 0)
    page_specs = [pl.BlockSpec((None, None, page_rows, LANES), page_map(jj)) for jj in range(PAGES_PER_STEP)]
    grid_spec = pltpu.PrefetchScalarGridSpec(
        num_scalar_prefetch=1,
        grid=(nb, n_steps + 1),
        in_specs=[pl.BlockSpec((4, QK_DIM), const), pl.BlockSpec((1, HEAD_W), const),
                  pl.BlockSpec((None, Q_ROWS, ATT_W), per_b3), pl.BlockSpec((None, ATT_W, LANES), per_b3),
                  pl.BlockSpec((None, N_HEADS, LANES, HEAD_W), per_b4)] + page_specs + page_specs,
        out_specs=pl.BlockSpec((None, SUBLANES, ATT_W), per_b3),
        scratch_shapes=[pltpu.VMEM((Q_ROWS, 1), F32), pltpu.VMEM((Q_ROWS, 1), F32),
                        pltpu.VMEM((Q_ROWS, HEAD_W), F32)],
    )
    return pl.pallas_call(
        functools.partial(_paged_kernel, n_steps=n_steps, n_new=n_new, lam_init=lam_init),
        grid_spec=grid_spec,
        out_shape=jax.ShapeDtypeStruct((nb, SUBLANES, ATT_W), F32),
        compiler_params=_cparams(("parallel", "arbitrary")),
        name="paged_diff_attn",
    )(page_table, lq, subln_g, qbd, ktn, vn, *([kcache] * PAGES_PER_STEP), *([vcache] * PAGES_PER_STEP))


CONV_CHUNK = 32
CONV_PAD = 32


def _conv_kernel(ug_ref, hist_ref, w_ref, b_ref, lg_ref, lb_ref, c_ref, nb_ref, full_ref, shift_ref):
    t = pl.program_id(1)
    tt = ug_ref.shape[0]
    h0 = CONV_PAD - CONV_HIST

    @pl.when(t == 0)
    def _first():
        full_ref[h0:CONV_PAD, :] = hist_ref[...]

    @pl.when(t > 0)
    def _carry():
        full_ref[h0:CONV_PAD, :] = full_ref[tt + h0:tt + CONV_PAD, :]

    ug = ug_ref[...]
    full_ref[CONV_PAD:CONV_PAD + tt, :] = ug[:, :CONV_CH] * _sigmoid(ug[:, CONV_CH:])
    n_shift = shift_ref.shape[1]
    for phase in range(1, SUBLANES):
        shift_ref[phase - 1] = full_ref[phase:phase + n_shift, :]
    chunk = min(tt, CONV_CHUNK)
    for r0 in range(0, tt, chunk):
        acc = jnp.zeros((chunk, CONV_CH), F32)
        for tap in range(CONV_TAPS):
            tile_row, phase = divmod(h0 + tap, SUBLANES)
            start = tile_row * SUBLANES + r0
            if phase == 0:
                window = full_ref[start:start + chunk, :]
            else:
                window = shift_ref[phase - 1, start:start + chunk, :]
            acc = acc + w_ref[tap:tap + 1, :] * window
        c = acc + b_ref[...]
        mu = jnp.mean(c, axis=-1, keepdims=True)
        xc = c - mu
        y = xc * lax.rsqrt(jnp.mean(xc * xc, axis=-1, keepdims=True) + LN_EPS)
        y = y * lg_ref[...] + lb_ref[...]
        c_ref[r0:r0 + chunk, :] = (y * _sigmoid(y)).astype(BF16)

    @pl.when(t == pl.num_programs(1) - 1)
    def _tail():
        nb_ref[...] = full_ref[tt + h0:tt + CONV_PAD, :]


def _conv_branch(ug, hist, w, b, lg, lb, tt):
    bx, sx, _ = ug.shape
    row = lambda b_, t: (b_, t, 0)
    per_b = lambda b_, t: (b_, 0, 0)
    const = lambda b_, t: (0, 0)
    return pl.pallas_call(
        _conv_kernel,
        grid=(bx, sx // tt),
        in_specs=[pl.BlockSpec((None, tt, 2 * CONV_CH), row), pl.BlockSpec((None, CONV_HIST, CONV_CH), per_b),
                  pl.BlockSpec((CONV_TAPS, CONV_CH), const), pl.BlockSpec((1, CONV_CH), const),
                  pl.BlockSpec((1, CONV_CH), const), pl.BlockSpec((1, CONV_CH), const)],
        out_specs=[pl.BlockSpec((None, tt, CONV_CH), row), pl.BlockSpec((None, CONV_HIST, CONV_CH), per_b)],
        out_shape=[jax.ShapeDtypeStruct((bx, sx, CONV_CH), BF16), jax.ShapeDtypeStruct((bx, CONV_HIST, CONV_CH), F32)],
        scratch_shapes=[pltpu.VMEM((CONV_PAD + tt, CONV_CH), F32),
                        pltpu.VMEM((SUBLANES - 1, CONV_PAD - SUBLANES + tt, CONV_CH), F32)],
        compiler_params=_cparams(("parallel", "arbitrary")),
        name="conv_branch",
    )(ug, hist, w, b, lg, lb)


def _outproj_kernel(*refs, widths):
    a_refs = refs[:len(widths)]
    w_ref, g_ref, h_ref, o_ref = refs[len(widths):]
    acc = None
    r0 = 0
    for a_ref, width in zip(a_refs, widths):
        part = jnp.dot(a_ref[...].astype(BF16), w_ref[r0:r0 + width, :], preferred_element_type=F32)
        acc = part if acc is None else acc + part
        r0 += width
    o_ref[...] = h_ref[...] + _rms(acc, g_ref[...], RMS_EPS)


def _outproj(a_list, a_specs, w_bf, g, h3, tm):
    bx, sx, d = h3.shape
    widths = tuple(spec.block_shape[-1] for spec in a_specs)
    row = lambda b, i: (b, i, 0)
    const = lambda b, i: (0, 0)
    return pl.pallas_call(
        functools.partial(_outproj_kernel, widths=widths),
        grid=(bx, sx // tm),
        in_specs=list(a_specs) + [pl.BlockSpec(w_bf.shape, const), pl.BlockSpec((1, d), const),
                                  pl.BlockSpec((None, tm, d), row)],
        out_specs=pl.BlockSpec((None, tm, d), row),
        out_shape=jax.ShapeDtypeStruct((bx, sx, d), F32),
        compiler_params=_cparams(("parallel", "parallel")),
        name="mixer_outproj",
    )(*a_list, w_bf, g, h3)


MLP_TM = 1024
MLP_TK = 1024


def _mlp_kernel(x_ref, gpre_ref, wu_ref, wd_ref, gpost_ref, o_ref, xn_ref, acc_ref):
    k = pl.program_id(1)

    @pl.when(k == 0)
    def _init():
        xn_ref[...] = _rms(x_ref[...], gpre_ref[...], RMS_EPS).astype(BF16)
        acc_ref[...] = jnp.zeros(acc_ref.shape, F32)

    up = jnp.dot(xn_ref[...], wu_ref[...], preferred_element_type=F32)
    act = jnp.square(jnp.maximum(up, 0.0)).astype(BF16)
    acc_ref[...] += jnp.dot(act, wd_ref[...], preferred_element_type=F32)

    @pl.when(k == pl.num_programs(1) - 1)
    def _done():
        o_ref[...] = x_ref[...] + _rms(acc_ref[...], gpost_ref[...], RMS_EPS)


def _mlp(x2, gpre, wu_bf, wd_bf, gpost, layer, tm, tk):
    m, d = x2.shape
    dff = wu_bf.shape[2]
    return pl.pallas_call(
        _mlp_kernel,
        grid=(m // tm, dff // tk),
        in_specs=[pl.BlockSpec((tm, d), lambda i, k: (i, 0)), pl.BlockSpec((1, d), lambda i, k: (0, 0)),
                  pl.BlockSpec((None, d, tk), lambda i, k: (layer, 0, k)),
                  pl.BlockSpec((None, tk, d), lambda i, k: (layer, k, 0)),
                  pl.BlockSpec((1, d), lambda i, k: (0, 0))],
        out_specs=pl.BlockSpec((tm, d), lambda i, k: (i, 0)),
        out_shape=jax.ShapeDtypeStruct((m, d), F32),
        scratch_shapes=[pltpu.VMEM((tm, d), BF16), pltpu.VMEM((tm, d), F32)],
        compiler_params=_cparams(("parallel", "arbitrary")),
        name="sq_relu_mlp",
    )(x2, gpre, wu_bf, wd_bf, gpost)


def _odd_inproj_kernel(x_ref, g_ref, w_ref, u_ref):
    xn = _rms(x_ref[...], g_ref[...], RMS_EPS).astype(BF16)
    u_ref[...] = jnp.dot(xn, w_ref[...], preferred_element_type=F32)


def _odd_inproj(x3, g, w_bf, tm):
    bx, sx, d = x3.shape
    c = w_bf.shape[1]
    out = pl.pallas_call(
        _odd_inproj_kernel,
        grid=(bx, sx // tm),
        in_specs=[pl.BlockSpec((None, tm, d), lambda b, i: (b, i, 0)), pl.BlockSpec((1, d), lambda b, i: (0, 0)),
                  pl.BlockSpec((d, c), lambda b, i: (0, 0))],
        out_specs=pl.BlockSpec((tm, c), lambda b, i: (i, b)),
        out_shape=jax.ShapeDtypeStruct((sx, bx * c), F32),
        compiler_params=_cparams(("parallel", "parallel")),
        name="odd_inproj",
    )(x3, g, w_bf)
    return out.reshape(sx * bx, c)


def _s5_prep_kernel(lr_ref, li_ref, ldt_ref, br_ref, bi_ref, abr_ref, abi_ref, wre_ref, wim_ref):
    lr = lr_ref[...]
    li = li_ref[...]
    dt = jnp.exp(ldt_ref[...])
    mag = jnp.exp(lr * dt)
    ang = li * dt
    abr = mag * jnp.cos(ang)
    abi = mag * jnp.sin(ang)
    den = lr * lr + li * li
    nr = abr - 1.0
    ni = abi
    fr = (nr * lr + ni * li) / den
    fi = (ni * lr - nr * li) / den
    abr_ref[...] = abr
    abi_ref[...] = abi
    for j in range(lr.shape[0]):
        frj = fr[j:j + 1, :]
        fij = fi[j:j + 1, :]
        br = br_ref[j]
        bi = bi_ref[j]
        wr = frj * br - fij * bi
        wi = frj * bi + fij * br
        wre_ref[j, :LANES, :] = wr.astype(BF16)
        wim_ref[j, :LANES, :] = wi.astype(BF16)
        arj = abr[j:j + 1, :]
        aij = abi[j:j + 1, :]
        wre_ref[j, LANES:, :] = (arj * wr - aij * wi).astype(BF16)
        wim_ref[j, LANES:, :] = (arj * wi + aij * wr).astype(BF16)


def _s5_prep(lr, li, ldt, br_bd, bi_bd):
    nt = lr.shape[0]
    vm = pl.BlockSpec(memory_space=pltpu.VMEM)
    return pl.pallas_call(
        _s5_prep_kernel,
        in_specs=[vm] * 5,
        out_specs=[vm] * 4,
        out_shape=[jax.ShapeDtypeStruct((nt, STATE_TILE), F32), jax.ShapeDtypeStruct((nt, STATE_TILE), F32),
                   jax.ShapeDtypeStruct((nt, 2 * LANES, STATE_TILE), BF16),
                   jax.ShapeDtypeStruct((nt, 2 * LANES, STATE_TILE), BF16)],
        name="s5_discretise",
    )(lr, li, ldt, br_bd, bi_bd)


SCAN_LANES = 512


def _scan_interleaved(bur_ref, bui_ref, cr_ref, ci_ref, abr_ref, abi_ref):
    rc, n_state = bur_ref.shape
    upper = lax.broadcasted_iota(jnp.int32, (SUBLANES, SCAN_LANES), 0) >= 4
    for g in range(n_state // SCAN_LANES):
        sl = slice(g * SCAN_LANES, (g + 1) * SCAN_LANES)
        ar = jnp.broadcast_to(abr_ref[:, sl], (SUBLANES, SCAN_LANES))
        ai = jnp.broadcast_to(abi_ref[:, sl], (SUBLANES, SCAN_LANES))
        a2r = ar * ar - ai * ai
        a2i = ar * ai + ai * ar
        acr = jnp.where(upper, a2r, ar)
        aci = jnp.where(upper, a2i, ai)

        def body(r, carry, sl=sl, acr=acr, aci=aci):
            pr, pi = carry
            rows = pl.ds(pl.multiple_of(r * SUBLANES, SUBLANES), SUBLANES)
            x1r = bur_ref[rows, sl]
            x1i = bui_ref[rows, sl]
            x2r = x1r + (acr * pr - aci * pi)
            x2i = x1i + (acr * pi + aci * pr)
            bur_ref[rows, sl] = x2r
            bui_ref[rows, sl] = x2i
            npr = jnp.where(upper, x2r, pltpu.roll(x2r, 4, 0))
            npi = jnp.where(upper, x2i, pltpu.roll(x2i, 4, 0))
            return npr, npi

        pr, pi = lax.fori_loop(0, rc // SUBLANES, body, (cr_ref[:, sl], ci_ref[:, sl]))
        cr_ref[:, sl] = pr
        ci_ref[:, sl] = pi


def _scan_dense(bur_ref, bui_ref, cr_ref, ci_ref, abr_ref, abi_ref, bsz):
    rc, n_state = bur_ref.shape
    for g in range(n_state // SCAN_LANES):
        sl = slice(g * SCAN_LANES, (g + 1) * SCAN_LANES)
        ar = abr_ref[:, sl]
        ai = abi_ref[:, sl]
        pr = cr_ref[:, sl]
        pi = ci_ref[:, sl]
        for t in range(rc // bsz):
            rows = slice(t * bsz, (t + 1) * bsz)
            nr = bur_ref[rows, sl] + (ar * pr - ai * pi)
            ni = bui_ref[rows, sl] + (ar * pi + ai * pr)
            bur_ref[rows, sl] = nr
            bui_ref[rows, sl] = ni
            pr, pi = nr, ni
        cr_ref[:, sl] = pr
        ci_ref[:, sl] = pi


def _s5_kernel(u_ref, h0r_ref, h0i_ref, abr_ref, abi_ref, wre_ref, wim_ref, cre_ref, cim_ref, d_ref, wg_ref,
               z_ref, hr_ref, hi_ref, bur_ref, bui_ref, cr_ref, ci_ref, *, bsz):
    step = pl.program_id(0)
    n_tiles = wre_ref.shape[0]

    @pl.when(step == 0)
    def _init():
        cr_ref[...] = h0r_ref[...]
        ci_ref[...] = h0i_ref[...]

    u = u_ref[...]
    ub = u.astype(BF16)
    if bsz == 4:
        row = lax.broadcasted_iota(jnp.int32, u.shape, 0)
        ub_prev = jnp.where((row & 4) != 0, pltpu.roll(u, 4, 0), 0.0).astype(BF16)
    for j in range(n_tiles):
        cols = slice(j * LANES, (j + 1) * LANES)
        states = slice(j * STATE_TILE, (j + 1) * STATE_TILE)
        if bsz == 4:
            lhs = jnp.concatenate([ub[:, cols], ub_prev[:, cols]], axis=-1)
            bur_ref[:, states] = jnp.dot(lhs, wre_ref[j], preferred_element_type=F32)
            bui_ref[:, states] = jnp.dot(lhs, wim_ref[j], preferred_element_type=F32)
        else:
            bur_ref[:, states] = jnp.dot(ub[:, cols], wre_ref[j, :LANES, :], preferred_element_type=F32)
            bui_ref[:, states] = jnp.dot(ub[:, cols], wim_ref[j, :LANES, :], preferred_element_type=F32)

    if bsz == 4:
        _scan_interleaved(bur_ref, bui_ref, cr_ref, ci_ref, abr_ref, abi_ref)
    else:
        _scan_dense(bur_ref, bui_ref, cr_ref, ci_ref, abr_ref, abi_ref, bsz)

    ys = []
    for j in range(n_tiles):
        hr = bur_ref[:, j * STATE_TILE:(j + 1) * STATE_TILE].astype(BF16)
        hi = bui_ref[:, j * STATE_TILE:(j + 1) * STATE_TILE].astype(BF16)
        ys.append(jnp.dot(hr, cre_ref[j], preferred_element_type=F32)
                  - jnp.dot(hi, cim_ref[j], preferred_element_type=F32))
    y = jnp.concatenate(ys, axis=-1) + d_ref[...] * u
    z = 0.5 * y * (1.0 + lax.erf(y * math.sqrt(0.5)))
    gate = jnp.dot(z.astype(BF16), wg_ref[...], preferred_element_type=F32)
    z_ref[...] = (z * _sigmoid(gate)).astype(BF16)

    @pl.when(step == pl.num_programs(0) - 1)
    def _state():
        hr_ref[...] = cr_ref[...]
        hi_ref[...] = ci_ref[...]


def _s5_core(u_tm, h0r, h0i, abr, abi, wre, wim, cre, cim, d_skip, wg_bf, bsz, rc):
    rows, c = u_tm.shape
    n_state = abr.shape[1]
    carry_rows = h0r.shape[0]
    const2 = lambda s: (0, 0)
    const3 = lambda s: (0, 0, 0)
    return pl.pallas_call(
        functools.partial(_s5_kernel, bsz=bsz),
        grid=(rows // rc,),
        in_specs=[pl.BlockSpec((rc, c), lambda s: (s, 0)),
                  pl.BlockSpec((carry_rows, n_state), const2), pl.BlockSpec((carry_rows, n_state), const2),
                  pl.BlockSpec((1, n_state), const2), pl.BlockSpec((1, n_state), const2),
                  pl.BlockSpec(wre.shape, const3), pl.BlockSpec(wim.shape, const3),
                  pl.BlockSpec(cre.shape, const3), pl.BlockSpec(cim.shape, const3),
                  pl.BlockSpec((1, c), const2), pl.BlockSpec(wg_bf.shape, const2)],
        out_specs=[pl.BlockSpec((rc, c), lambda s: (s, 0)),
                   pl.BlockSpec((carry_rows, n_state), const2), pl.BlockSpec((carry_rows, n_state), const2)],
        out_shape=[jax.ShapeDtypeStruct((rows, c), BF16), jax.ShapeDtypeStruct((carry_rows, n_state), F32),
                   jax.ShapeDtypeStruct((carry_rows, n_state), F32)],
        scratch_shapes=[pltpu.VMEM((rc, n_state), F32), pltpu.VMEM((rc, n_state), F32),
                        pltpu.VMEM((carry_rows, n_state), F32), pltpu.VMEM((carry_rows, n_state), F32)],
        compiler_params=_cparams(("arbitrary",)),
        name="s5_core",
    )(u_tm, h0r, h0i, abr, abi, wre, wim, cre, cim, d_skip, wg_bf)


def _block_diag_tiles(x, rows_first):
    nt, ng, a, b = x.shape
    eye = jnp.eye(ng, dtype=jnp.bool_)
    out = jnp.where(eye[None, :, None, :, None], x[:, :, :, None, :], jnp.zeros((), x.dtype))
    del rows_first
    return out.reshape(nt, ng * a, ng * b)


def _tile(n):
    for cand in (512, 256, 128, 64, 32, 16, 8):
        if n % cand == 0:
            return cand
    return n


def _even_layer_prompt(hp, layer, ew):
    b, s, d = hp.shape
    q, kt, ktb, v, vb, ug = _even_inproj(hp, ew["g_pre"], ew["w_in"], ew["wkt"], _tile(s))
    lam_init = _lambda_init(layer)
    attn = _flash_attention(q, ktb, vb, ew["lq"], ew["subln_g"], lam_init, _tile(s))
    hist = jnp.zeros((b, CONV_HIST, CONV_CH), F32)
    conv, new_buf = _conv_branch(ug, hist, ew["conv_w"], ew["conv_b"], ew["ln_g"], ew["ln_b"], _tile(s))
    tm = _tile(s)
    row = lambda b_, i: (b_, i, 0)
    specs = [pl.BlockSpec((None, tm, ATT_W), row), pl.BlockSpec((None, tm, CONV_CH), row)]
    hp = _outproj([attn, conv], specs, ew["w_out"], ew["g_post"], hp, tm)
    k_new = kt.reshape(b, N_HEADS, 2, QK_DIM, s).transpose(0, 4, 1, 2, 3)
    v_new = v.reshape(b, s, N_HEADS, HEAD_W)
    return hp, k_new, v_new, new_buf


def _even_layer_sample(hs, layer, idx, ew, kcache, vcache, page_table, state_conv):
    nb, n_new, d = hs.shape
    m = nb * n_new
    q, kt, _, v, _, ug = _even_inproj(hs.reshape(1, m, d), ew["g_pre"], ew["w_in"], ew["wkt"], _tile(m))
    lam_init = _lambda_init(layer)
    q5 = q.reshape(nb, n_new, N_HEADS, 2, QK_DIM).transpose(0, 2, 3, 1, 4)
    q5 = jnp.pad(q5, ((0, 0), (0, 0), (0, 0), (0, SUBLANES - n_new), (0, 0)))
    eye = jnp.eye(2 * N_HEADS, dtype=jnp.bool_).reshape(N_HEADS, 2, 1, N_HEADS, 2, 1)
    qbd = jnp.where(eye[None], q5[:, :, :, :, None, None, :], jnp.zeros((), BF16)).reshape(nb, Q_ROWS, ATT_W)
    ktn = kt.reshape(ATT_W, nb, n_new).transpose(1, 0, 2)
    ktn = jnp.pad(ktn, ((0, 0), (0, 0), (0, LANES - n_new))).astype(BF16)
    vn = v.reshape(nb, n_new, N_HEADS, HEAD_W).transpose(0, 2, 1, 3)
    vn = jnp.pad(vn, ((0, 0), (0, 0), (0, LANES - n_new), (0, 0))).astype(BF16)
    attn = _paged_attention(page_table, ew["lq"], ew["subln_g"], qbd, ktn, vn, kcache, vcache, idx, n_new, lam_init)
    attn = attn[:, :n_new, :]
    conv, new_buf = _conv_branch(ug.reshape(nb, n_new, 2 * CONV_CH), state_conv, ew["conv_w"], ew["conv_b"],
                                 ew["ln_g"], ew["ln_b"], n_new)
    tm = _tile(m)
    row = lambda b_, i: (b_, i, 0)
    specs = [pl.BlockSpec((None, tm, ATT_W), row), pl.BlockSpec((None, tm, CONV_CH), row)]
    out = _outproj([attn.reshape(1, m, ATT_W), conv.reshape(1, m, CONV_CH)], specs, ew["w_out"], ew["g_post"],
                   hs.reshape(1, m, d), tm)
    k_new = kt.reshape(N_HEADS, 2, QK_DIM, nb, n_new).transpose(3, 4, 0, 1, 2)
    v_new = v.reshape(nb, n_new, N_HEADS, HEAD_W)
    return out.reshape(nb, n_new, d), k_new, v_new, new_buf


def _odd_layer_prompt(hp, ow):
    b, s, d = hp.shape
    tm = _tile(s)
    u_tm = _odd_inproj(hp, ow["g_pre"], ow["w_in"], tm)
    n_state = ow["abr"].shape[1]
    zeros = jnp.zeros((SUBLANES, n_state), F32)
    z, hr, hi = _s5_core(u_tm, zeros, zeros, ow["abr"], ow["abi"], ow["wre"], ow["wim"], ow["cre"], ow["cim"],
                         ow["d"], ow["w_gate"], b, _tile(s * b))
    z2 = z.reshape(s, b * d)
    specs = [pl.BlockSpec((tm, d), lambda b_, i: (i, b_))]
    hp = _outproj([z2], specs, ow["w_out"], ow["g_post"], hp, tm)
    return hp, hr[:b], hi[:b]


def _odd_layer_sample(hs, ow, h0r, h0i):
    nb, n_new, d = hs.shape
    m = nb * n_new
    u_tm = _odd_inproj(hs.transpose(1, 0, 2).reshape(1, m, d), ow["g_pre"], ow["w_in"], _tile(m))
    z, hr, hi = _s5_core(u_tm, h0r, h0i, ow["abr"], ow["abi"], ow["wre"], ow["wim"], ow["cre"], ow["cim"],
                         ow["d"], ow["w_gate"], nb, m)
    z = z.reshape(n_new, nb, d).transpose(1, 0, 2).reshape(1, m, d)
    tm = _tile(m)
    specs = [pl.BlockSpec((None, tm, d), lambda b_, i: (b_, i, 0))]
    out = _outproj([z], specs, ow["w_out"], ow["g_post"], hs.reshape(1, m, d), tm)
    return out.reshape(nb, n_new, d), hr, hi


def kernel(x_prompt, x_sample, cache_k, cache_v, page_table, state_conv, state_ssm_re, state_ssm_im, g_mix_pre, g_mix_post, g_ffn_pre, g_ffn_post, w_in_even, lambda_qk, subln_g, conv_w, conv_b, conv_ln_g, conv_ln_b, w_out_even, w_in_odd, ssm_a_re, ssm_a_im, ssm_b_re, ssm_b_im, ssm_c_re, ssm_c_im, ssm_d, ssm_log_dt, w_gate_odd, w_out_odd, w_ffn_up, w_ffn_down):
    depth, d = g_mix_pre.shape
    hp, hs = x_prompt, x_sample
    nb, n_new, _ = x_sample.shape
    n_layers_even, n_phys, page = cache_k.shape[:3]
    n_groups, n_p = ssm_a_re.shape[1:]
    n_tiles = n_groups // GROUPS_PER_TILE
    n_state = n_groups * n_p

    kcache = cache_k.transpose(0, 1, 3, 4, 5, 2).reshape(n_layers_even, n_phys, ATT_W, page)
    vcache = cache_v.reshape(n_layers_even, n_phys, page * N_HEADS, HEAD_W)

    row = lambda x: x.reshape(1, -1)
    wu_all = w_ffn_up.astype(BF16)
    wd_all = w_ffn_down.astype(BF16)
    outs = {k: [] for k in ("kp", "vp", "ks", "vs", "cp", "cs", "srp", "sip", "srs", "sis")}
    for layer in range(depth):
        i = layer // 2
        if layer % 2 == 0:
            w_in = w_in_even[i].astype(BF16)
            ew = dict(g_pre=row(g_mix_pre[layer]), g_post=row(g_mix_post[layer]), w_in=w_in,
                      wkt=w_in[:, ATT_W:2 * ATT_W].T, lq=lambda_qk[i], subln_g=row(subln_g[i]),
                      conv_w=conv_w[i], conv_b=row(conv_b[i]), ln_g=row(conv_ln_g[i]), ln_b=row(conv_ln_b[i]),
                      w_out=w_out_even[i].astype(BF16))
            hp, k_new, v_new, buf = _even_layer_prompt(hp, layer, ew)
            outs["kp"].append(k_new); outs["vp"].append(v_new); outs["cp"].append(buf)
            hs, k_new, v_new, buf = _even_layer_sample(hs, layer, i, ew, kcache, vcache, page_table, state_conv[i])
            outs["ks"].append(k_new); outs["vs"].append(v_new); outs["cs"].append(buf)
        else:
            lr = ssm_a_re[i].reshape(n_tiles, STATE_TILE)
            li = ssm_a_im[i].reshape(n_tiles, STATE_TILE)
            ldt = jnp.repeat(ssm_log_dt[i], n_p).reshape(n_tiles, STATE_TILE)
            b4 = lambda x: x.reshape(n_tiles, GROUPS_PER_TILE, n_p, GROUP_CH).transpose(0, 1, 3, 2)
            abr, abi, wre, wim = _s5_prep(lr, li, ldt, _block_diag_tiles(b4(ssm_b_re[i]), True),
                                          _block_diag_tiles(b4(ssm_b_im[i]), True))
            c4 = lambda x: x.reshape(n_tiles, GROUPS_PER_TILE, GROUP_CH, n_p).transpose(0, 1, 3, 2)
            ow = dict(g_pre=row(g_mix_pre[layer]), g_post=row(g_mix_post[layer]), w_in=w_in_odd[i].astype(BF16),
                      abr=abr.reshape(1, n_state), abi=abi.reshape(1, n_state), wre=wre, wim=wim,
                      cre=_block_diag_tiles(c4(ssm_c_re[i]), False).astype(BF16),
                      cim=_block_diag_tiles(c4(ssm_c_im[i]), False).astype(BF16),
                      d=row(ssm_d[i]), w_gate=w_gate_odd[i].astype(BF16), w_out=w_out_odd[i].astype(BF16))
            hp, sr, si = _odd_layer_prompt(hp, ow)
            outs["srp"].append(sr.reshape(-1, n_groups, n_p)); outs["sip"].append(si.reshape(-1, n_groups, n_p))
            hs, sr, si = _odd_layer_sample(hs, ow, state_ssm_re[i].reshape(nb, n_state),
                                           state_ssm_im[i].reshape(nb, n_state))
            outs["srs"].append(sr.reshape(nb, n_groups, n_p)); outs["sis"].append(si.reshape(nb, n_groups, n_p))
        gpre, gpost = row(g_ffn_pre[layer]), row(g_ffn_post[layer])
        mp = hp.shape[0] * hp.shape[1]
        hp = _mlp(hp.reshape(mp, d), gpre, wu_all, wd_all, gpost, layer, MLP_TM if mp % MLP_TM == 0 else _tile(mp),
                  MLP_TK).reshape(hp.shape)
        ms = nb * n_new
        hs = _mlp(hs.reshape(ms, d), gpre, wu_all, wd_all, gpost, layer, _tile(ms), MLP_TK).reshape(hs.shape)
    st = jnp.stack
    return (hp, hs, st(outs["kp"]), st(outs["vp"]), st(outs["ks"]), st(outs["vs"]), st(outs["cp"]), st(outs["cs"]),
            st(outs["srp"]), st(outs["sip"]), st(outs["srs"]), st(outs["sis"]))
```

```python
import functools
import math

import jax
import jax.numpy as jnp
import numpy as np
from jax import lax
from jax.experimental import pallas as pl
from jax.experimental.pallas import tpu as pltpu

F32 = jnp.float32
BF16 = jnp.bfloat16

RMS_EPS = 1e-6
SUBLN_EPS = 1e-5
LN_EPS = 1e-5
LOG2_E = math.log2(math.e)

LANES = 128
SUBLANES = 8
N_HEADS = 4
QK_DIM = 64
HEAD_W = 2 * QK_DIM
ATT_W = N_HEADS * HEAD_W
CONV_CH = 512
CONV_TAPS = 31
CONV_HIST = CONV_TAPS - 1
GROUP_CH = 16
STATE_P = 64
GROUPS_PER_TILE = LANES // GROUP_CH
STATE_TILE = GROUPS_PER_TILE * STATE_P
VMEM_LIMIT = 52 * 1024 * 1024


def _cparams(semantics):
    return pltpu.CompilerParams(dimension_semantics=semantics, vmem_limit_bytes=VMEM_LIMIT)


def _rms(x, g, eps):
    ms = jnp.mean(x * x, axis=-1, keepdims=True)
    return (x * lax.rsqrt(ms + eps)) * g


def _sigmoid(x):
    return 1.0 / (1.0 + jnp.exp(-x))


def _lambda_full(lq, lam_init):
    s1 = jnp.sum(lq[0:1, :] * lq[1:2, :], axis=-1, keepdims=True)
    s2 = jnp.sum(lq[2:3, :] * lq[3:4, :], axis=-1, keepdims=True)
    return jnp.exp(s1) - jnp.exp(s2) + lam_init


def _lambda_init(layer):
    return 0.8 - 0.6 * math.exp(-0.3 * layer)


def _even_inproj_kernel(x_ref, g_ref, w_ref, wkt_ref, q_ref, kt_ref, ktb_ref, v_ref, vb_ref, ug_ref):
    xn = _rms(x_ref[...], g_ref[...], RMS_EPS).astype(BF16)

    def proj(c0, width):
        return jnp.dot(xn, w_ref[:, c0:c0 + width], preferred_element_type=F32)

    q_ref[...] = (proj(0, ATT_W) * (QK_DIM ** -0.5 * LOG2_E)).astype(BF16)
    kt = lax.dot_general(wkt_ref[...], xn, (((1,), (1,)), ((), ())), preferred_element_type=F32)
    kt_ref[...] = kt
    ktb_ref[...] = kt.astype(BF16)
    v = proj(2 * ATT_W, ATT_W)
    v_ref[...] = v
    vb_ref[...] = v.astype(BF16)
    ug_ref[...] = proj(3 * ATT_W, 2 * CONV_CH)


def _even_inproj(x3, g, w_bf, wkt_bf, tm):
    bx, sx, d = x3.shape
    n_in = w_bf.shape[1]
    row = lambda b, i: (b, i, 0)
    col = lambda b, i: (b, 0, i)
    const = lambda b, i: (0, 0)
    return pl.pallas_call(
        _even_inproj_kernel,
        grid=(bx, sx // tm),
        in_specs=[pl.BlockSpec((None, tm, d), row), pl.BlockSpec((1, d), const),
                  pl.BlockSpec((d, n_in), const), pl.BlockSpec((ATT_W, d), const)],
        out_specs=[pl.BlockSpec((None, tm, ATT_W), row), pl.BlockSpec((None, ATT_W, tm), col),
                   pl.BlockSpec((None, ATT_W, tm), col), pl.BlockSpec((None, tm, ATT_W), row),
                   pl.BlockSpec((None, tm, ATT_W), row), pl.BlockSpec((None, tm, 2 * CONV_CH), row)],
        out_shape=[jax.ShapeDtypeStruct((bx, sx, ATT_W), BF16), jax.ShapeDtypeStruct((bx, ATT_W, sx), F32),
                   jax.ShapeDtypeStruct((bx, ATT_W, sx), BF16), jax.ShapeDtypeStruct((bx, sx, ATT_W), F32),
                   jax.ShapeDtypeStruct((bx, sx, ATT_W), BF16), jax.ShapeDtypeStruct((bx, sx, 2 * CONV_CH), F32)],
        compiler_params=_cparams(("parallel", "parallel")),
        name="even_inproj",
    )(x3, g, w_bf, wkt_bf)


def _flash_kernel(it_ref, jt_ref, lq_ref, g_ref, q_ref, kt_ref, v_ref, o_ref, qm_ref, m_ref, l_ref, acc_ref,
                  *, lam_init):
    t = pl.program_id(1)
    i = it_ref[t]
    j = jt_ref[t]
    heads = [slice(h * HEAD_W, (h + 1) * HEAD_W) for h in range(N_HEADS)]

    @pl.when(j == 0)
    def _init():
        q = q_ref[...]
        lane = lax.broadcasted_iota(jnp.int32, (q.shape[0], HEAD_W), 1)
        first = jnp.where(lane < QK_DIM, 1.0, 0.0).astype(BF16)
        for h, hs in enumerate(heads):
            qm_ref[2 * h] = q[:, hs] * first
            qm_ref[2 * h + 1] = q[:, hs] * (1.0 - first)
        m_ref[...] = jnp.full(m_ref.shape, -jnp.inf, F32)
        l_ref[...] = jnp.zeros(l_ref.shape, F32)
        acc_ref[...] = jnp.zeros(acc_ref.shape, F32)

    def step(masked):
        n_blk = kt_ref.shape[1] // LANES
        for h, hs in enumerate(heads):
            kt = kt_ref[hs, :]
            v = v_ref[:, hs]
            for c in range(2):
                hc = 2 * h + c
                s = jnp.dot(qm_ref[hc], kt, preferred_element_type=F32)
                if masked:
                    row = lax.broadcasted_iota(jnp.int32, s.shape, 0)
                    col = lax.broadcasted_iota(jnp.int32, s.shape, 1)
                    s = jnp.where(col <= row, s, -jnp.inf)
                blocks = [s[:, b * LANES:(b + 1) * LANES] for b in range(n_blk)]
                mx = functools.reduce(jnp.maximum, blocks)
                m_prev = m_ref[hc]
                m_new = jnp.maximum(m_prev, jnp.max(mx, axis=-1, keepdims=True))
                alpha = jnp.exp2(m_prev - m_new)
                ps = [jnp.exp2(blk - m_new) for blk in blocks]
                l_ref[hc] = alpha * l_ref[hc] + functools.reduce(jnp.add, ps)
                p = jnp.concatenate([x.astype(BF16) for x in ps], axis=-1)
                acc_ref[hc] = alpha * acc_ref[hc] + jnp.dot(p, v, preferred_element_type=F32)
                m_ref[hc] = m_new

    @pl.when(j < i)
    def _full():
        step(False)

    @pl.when(j == i)
    def _diag():
        step(True)
        lam = _lambda_full(lq_ref[...], lam_init)
        for h, hs in enumerate(heads):
            l0 = jnp.sum(l_ref[2 * h], axis=-1, keepdims=True)
            l1 = jnp.sum(l_ref[2 * h + 1], axis=-1, keepdims=True)
            o = acc_ref[2 * h] / l0 - lam * (acc_ref[2 * h + 1] / l1)
            o_ref[:, hs] = (_rms(o, g_ref[...], SUBLN_EPS) * (1.0 - lam_init)).astype(BF16)


def _flash_attention(q, kt, v, lq, subln_g, lam_init, tq):
    b, s, _ = q.shape
    nq = s // tq
    pairs = [(i, j) for i in range(nq) for j in range(i + 1)]
    it = jnp.asarray(np.array([p[0] for p in pairs], np.int32))
    jt = jnp.asarray(np.array([p[1] for p in pairs], np.int32))
    grid_spec = pltpu.PrefetchScalarGridSpec(
        num_scalar_prefetch=2,
        grid=(b, len(pairs)),
        in_specs=[pl.BlockSpec((4, QK_DIM), lambda b_, t, it_, jt_: (0, 0)),
                  pl.BlockSpec((1, HEAD_W), lambda b_, t, it_, jt_: (0, 0)),
                  pl.BlockSpec((None, tq, ATT_W), lambda b_, t, it_, jt_: (b_, it_[t], 0)),
                  pl.BlockSpec((None, ATT_W, tq), lambda b_, t, it_, jt_: (b_, 0, jt_[t])),
                  pl.BlockSpec((None, tq, ATT_W), lambda b_, t, it_, jt_: (b_, jt_[t], 0))],
        out_specs=pl.BlockSpec((None, tq, ATT_W), lambda b_, t, it_, jt_: (b_, it_[t], 0)),
        scratch_shapes=[pltpu.VMEM((2 * N_HEADS, tq, HEAD_W), BF16), pltpu.VMEM((2 * N_HEADS, tq, LANES), F32),
                        pltpu.VMEM((2 * N_HEADS, tq, LANES), F32), pltpu.VMEM((2 * N_HEADS, tq, HEAD_W), F32)],
    )
    return pl.pallas_call(
        functools.partial(_flash_kernel, lam_init=lam_init),
        grid_spec=grid_spec,
        out_shape=jax.ShapeDtypeStruct((b, s, ATT_W), BF16),
        compiler_params=_cparams(("parallel", "arbitrary")),
        name="flash_diff_attn",
    )(it, jt, lq, subln_g, q, kt, v)


PAGES_PER_STEP = 16
Q_ROWS = N_HEADS * 2 * SUBLANES


def _paged_kernel(pt_ref, lq_ref, g_ref, qbd_ref, ktn_ref, vn_ref, *rest, n_steps, n_new, lam_init):
    k_refs = rest[:PAGES_PER_STEP]
    v_refs = rest[PAGES_PER_STEP:2 * PAGES_PER_STEP]
    o_ref, m_ref, l_ref, acc_ref = rest[2 * PAGES_PER_STEP:]
    del pt_ref
    step = pl.program_id(1)
    rows_per_head = 2 * SUBLANES

    @pl.when(step == 0)
    def _init():
        m_ref[...] = jnp.full(m_ref.shape, -jnp.inf, F32)
        l_ref[...] = jnp.zeros(l_ref.shape, F32)
        acc_ref[...] = jnp.zeros(acc_ref.shape, F32)

    def update(scores, values):
        m_prev = m_ref[...]
        m_new = m_prev
        for s in scores:
            m_new = jnp.maximum(m_new, jnp.max(s, axis=-1, keepdims=True))
        alpha = jnp.exp2(m_prev - m_new)
        l_new = alpha * l_ref[...]
        acc = alpha * acc_ref[...]
        parts = [acc[h * rows_per_head:(h + 1) * rows_per_head] for h in range(N_HEADS)]
        for s, vh in zip(scores, values):
            p = jnp.exp2(s - m_new)
            l_new = l_new + jnp.sum(p, axis=-1, keepdims=True)
            pb = p.astype(BF16)
            for h in range(N_HEADS):
                parts[h] = parts[h] + jnp.dot(pb[h * rows_per_head:(h + 1) * rows_per_head], vh[h],
                                              preferred_element_type=F32)
        m_ref[...] = m_new
        l_ref[...] = l_new
        acc_ref[...] = jnp.concatenate(parts, axis=0)

    qbd = qbd_ref[...]

    @pl.when(step < n_steps)
    def _pages():
        scores, values = [], []
        for kr, vr in zip(k_refs, v_refs):
            scores.append(jnp.dot(qbd, kr[...].astype(BF16), preferred_element_type=F32))
            values.append([vr[pl.ds(h, LANES, stride=N_HEADS), :].astype(BF16) for h in range(N_HEADS)])
        update(scores, values)

    @pl.when(step == n_steps)
    def _new_tokens():
        s = jnp.dot(qbd, ktn_ref[...], preferred_element_type=F32)
        tok = lax.broadcasted_iota(jnp.int32, s.shape, 0) % SUBLANES
        col = lax.broadcasted_iota(jnp.int32, s.shape, 1)
        s = jnp.where((col <= tok) & (col < n_new), s, -jnp.inf)
        update([s], [[vn_ref[h] for h in range(N_HEADS)]])
        lam = _lambda_full(lq_ref[...], lam_init)
        acc = acc_ref[...]
        l = l_ref[...]
        g = g_ref[...]
        for h in range(N_HEADS):
            r0 = h * rows_per_head
            o1 = acc[r0:r0 + SUBLANES] / l[r0:r0 + SUBLANES]
            o2 = acc[r0 + SUBLANES:r0 + 2 * SUBLANES] / l[r0 + SUBLANES:r0 + 2 * SUBLANES]
            o = o1 - lam * o2
            o_ref[:, h * HEAD_W:(h + 1) * HEAD_W] = _rms(o, g, SUBLN_EPS) * (1.0 - lam_init)


def _paged_attention(page_table, lq, subln_g, qbd, ktn, vn, kcache, vcache, layer, n_new, lam_init):
    nb, n_pages = page_table.shape
    n_steps = n_pages // PAGES_PER_STEP
    page_rows = kcache.shape[2]

    def page_map(jj):
        def index_map(b, s, pt):
            return (layer, pt[b, jnp.minimum(s, n_steps - 1) * PAGES_PER_STEP + jj], 0, 0)
        return index_map

    per_b3 = lambda b, s, pt: (b, 0, 0)
    per_b4 = lambda b, s, pt: (b, 0, 0, 0)
    const = lambda b, s, pt: (0, 0)
    page_specs = [pl.BlockSpec((None, None, page_rows, LANES), page_map(jj)) for jj in range(PAGES_PER_STEP)]
    grid_spec = pltpu.PrefetchScalarGridSpec(
        num_scalar_prefetch=1,
        grid=(nb, n_steps + 1),
        in_specs=[pl.BlockSpec((4, QK_DIM), const), pl.BlockSpec((1, HEAD_W), const),
                  pl.BlockSpec((None, Q_ROWS, ATT_W), per_b3), pl.BlockSpec((None, ATT_W, LANES), per_b3),
                  pl.BlockSpec((None, N_HEADS, LANES, HEAD_W), per_b4)] + page_specs + page_specs,
        out_specs=pl.BlockSpec((None, SUBLANES, ATT_W), per_b3),
        scratch_shapes=[pltpu.VMEM((Q_ROWS, 1), F32), pltpu.VMEM((Q_ROWS, 1), F32),
                        pltpu.VMEM((Q_ROWS, HEAD_W), F32)],
    )
    return pl.pallas_call(
        functools.partial(_paged_kernel, n_steps=n_steps, n_new=n_new, lam_init=lam_init),
        grid_spec=grid_spec,
        out_shape=jax.ShapeDtypeStruct((nb, SUBLANES, ATT_W), F32),
        compiler_params=_cparams(("parallel", "arbitrary")),
        name="paged_diff_attn",
    )(page_table, lq, subln_g, qbd, ktn, vn, *([kcache] * PAGES_PER_STEP), *([vcache] * PAGES_PER_STEP))


CONV_CHUNK = 32
CONV_PAD = 32


def _conv_kernel(ug_ref, hist_ref, w_ref, b_ref, lg_ref, lb_ref, c_ref, nb_ref, full_ref, shift_ref):
    t = pl.program_id(1)
    tt = ug_ref.shape[0]
    h0 = CONV_PAD - CONV_HIST

    @pl.when(t == 0)
    def _first():
        full_ref[h0:CONV_PAD, :] = hist_ref[...]

    @pl.when(t > 0)
    def _carry():
        full_ref[h0:CONV_PAD, :] = full_ref[tt + h0:tt + CONV_PAD, :]

    ug = ug_ref[...]
    full_ref[CONV_PAD:CONV_PAD + tt, :] = ug[:, :CONV_CH] * _sigmoid(ug[:, CONV_CH:])
    n_shift = shift_ref.shape[1]
    for phase in range(1, SUBLANES):
        shift_ref[phase - 1] = full_ref[phase:phase + n_shift, :]
    chunk = min(tt, CONV_CHUNK)
    for r0 in range(0, tt, chunk):
        acc = jnp.zeros((chunk, CONV_CH), F32)
        for tap in range(CONV_TAPS):
            tile_row, phase = divmod(h0 + tap, SUBLANES)
            start = tile_row * SUBLANES + r0
            if phase == 0:
                window = full_ref[start:start + chunk, :]
            else:
                window = shift_ref[phase - 1, start:start + chunk, :]
            acc = acc + w_ref[tap:tap + 1, :] * window
        c = acc + b_ref[...]
        mu = jnp.mean(c, axis=-1, keepdims=True)
        xc = c - mu
        y = xc * lax.rsqrt(jnp.mean(xc * xc, axis=-1, keepdims=True) + LN_EPS)
        y = y * lg_ref[...] + lb_ref[...]
        c_ref[r0:r0 + chunk, :] = (y * _sigmoid(y)).astype(BF16)

    @pl.when(t == pl.num_programs(1) - 1)
    def _tail():
        nb_ref[...] = full_ref[tt + h0:tt + CONV_PAD, :]


def _conv_branch(ug, hist, w, b, lg, lb, tt):
    bx, sx, _ = ug.shape
    row = lambda b_, t: (b_, t, 0)
    per_b = lambda b_, t: (b_, 0, 0)
    const = lambda b_, t: (0, 0)
    return pl.pallas_call(
        _conv_kernel,
        grid=(bx, sx // tt),
        in_specs=[pl.BlockSpec((None, tt, 2 * CONV_CH), row), pl.BlockSpec((None, CONV_HIST, CONV_CH), per_b),
                  pl.BlockSpec((CONV_TAPS, CONV_CH), const), pl.BlockSpec((1, CONV_CH), const),
                  pl.BlockSpec((1, CONV_CH), const), pl.BlockSpec((1, CONV_CH), const)],
        out_specs=[pl.BlockSpec((None, tt, CONV_CH), row), pl.BlockSpec((None, CONV_HIST, CONV_CH), per_b)],
        out_shape=[jax.ShapeDtypeStruct((bx, sx, CONV_CH), BF16), jax.ShapeDtypeStruct((bx, CONV_HIST, CONV_CH), F32)],
        scratch_shapes=[pltpu.VMEM((CONV_PAD + tt, CONV_CH), F32),
                        pltpu.VMEM((SUBLANES - 1, CONV_PAD - SUBLANES + tt, CONV_CH), F32)],
        compiler_params=_cparams(("parallel", "arbitrary")),
        name="conv_branch",
    )(ug, hist, w, b, lg, lb)


def _outproj_kernel(*refs, widths):
    a_refs = refs[:len(widths)]
    w_ref, g_ref, h_ref, o_ref = refs[len(widths):]
    acc = None
    r0 = 0
    for a_ref, width in zip(a_refs, widths):
        part = jnp.dot(a_ref[...].astype(BF16), w_ref[r0:r0 + width, :], preferred_element_type=F32)
        acc = part if acc is None else acc + part
        r0 += width
    o_ref[...] = h_ref[...] + _rms(acc, g_ref[...], RMS_EPS)


def _outproj(a_list, a_specs, w_bf, g, h3, tm):
    bx, sx, d = h3.shape
    widths = tuple(spec.block_shape[-1] for spec in a_specs)
    row = lambda b, i: (b, i, 0)
    const = lambda b, i: (0, 0)
    return pl.pallas_call(
        functools.partial(_outproj_kernel, widths=widths),
        grid=(bx, sx // tm),
        in_specs=list(a_specs) + [pl.BlockSpec(w_bf.shape, const), pl.BlockSpec((1, d), const),
                                  pl.BlockSpec((None, tm, d), row)],
        out_specs=pl.BlockSpec((None, tm, d), row),
        out_shape=jax.ShapeDtypeStruct((bx, sx, d), F32),
        compiler_params=_cparams(("parallel", "parallel")),
        name="mixer_outproj",
    )(*a_list, w_bf, g, h3)


MLP_TM = 1024
MLP_TK = 1024


def _mlp_kernel(x_ref, gpre_ref, wu_ref, wd_ref, gpost_ref, o_ref, xn_ref, acc_ref):
    k = pl.program_id(1)

    @pl.when(k == 0)
    def _init():
        xn_ref[...] = _rms(x_ref[...], gpre_ref[...], RMS_EPS).astype(BF16)
        acc_ref[...] = jnp.zeros(acc_ref.shape, F32)

    up = jnp.dot(xn_ref[...], wu_ref[...], preferred_element_type=F32)
    act = jnp.square(jnp.maximum(up, 0.0)).astype(BF16)
    acc_ref[...] += jnp.dot(act, wd_ref[...], preferred_element_type=F32)

    @pl.when(k == pl.num_programs(1) - 1)
    def _done():
        o_ref[...] = x_ref[...] + _rms(acc_ref[...], gpost_ref[...], RMS_EPS)


def _mlp(x2, gpre, wu_bf, wd_bf, gpost, layer, tm, tk):
    m, d = x2.shape
    dff = wu_bf.shape[2]
    return pl.pallas_call(
        _mlp_kernel,
        grid=(m // tm, dff // tk),
        in_specs=[pl.BlockSpec((tm, d), lambda i, k: (i, 0)), pl.BlockSpec((1, d), lambda i, k: (0, 0)),
                  pl.BlockSpec((None, d, tk), lambda i, k: (layer, 0, k)),
                  pl.BlockSpec((None, tk, d), lambda i, k: (layer, k, 0)),
                  pl.BlockSpec((1, d), lambda i, k: (0, 0))],
        out_specs=pl.BlockSpec((tm, d), lambda i, k: (i, 0)),
        out_shape=jax.ShapeDtypeStruct((m, d), F32),
        scratch_shapes=[pltpu.VMEM((tm, d), BF16), pltpu.VMEM((tm, d), F32)],
        compiler_params=_cparams(("parallel", "arbitrary")),
        name="sq_relu_mlp",
    )(x2, gpre, wu_bf, wd_bf, gpost)


def _odd_inproj_kernel(x_ref, g_ref, w_ref, u_ref):
    xn = _rms(x_ref[...], g_ref[...], RMS_EPS).astype(BF16)
    u_ref[...] = jnp.dot(xn, w_ref[...], preferred_element_type=F32)


def _odd_inproj(x3, g, w_bf, tm):
    bx, sx, d = x3.shape
    c = w_bf.shape[1]
    out = pl.pallas_call(
        _odd_inproj_kernel,
        grid=(bx, sx // tm),
        in_specs=[pl.BlockSpec((None, tm, d), lambda b, i: (b, i, 0)), pl.BlockSpec((1, d), lambda b, i: (0, 0)),
                  pl.BlockSpec((d, c), lambda b, i: (0, 0))],
        out_specs=pl.BlockSpec((tm, c), lambda b, i: (i, b)),
        out_shape=jax.ShapeDtypeStruct((sx, bx * c), F32),
        compiler_params=_cparams(("parallel", "parallel")),
        name="odd_inproj",
    )(x3, g, w_bf)
    return out.reshape(sx * bx, c)


def _s5_prep_kernel(lr_ref, li_ref, ldt_ref, br_ref, bi_ref, abr_ref, abi_ref, wre_ref, wim_ref):
    lr = lr_ref[...]
    li = li_ref[...]
    dt = jnp.exp(ldt_ref[...])
    mag = jnp.exp(lr * dt)
    ang = li * dt
    abr = mag * jnp.cos(ang)
    abi = mag * jnp.sin(ang)
    den = lr * lr + li * li
    nr = abr - 1.0
    ni = abi
    fr = (nr * lr + ni * li) / den
    fi = (ni * lr - nr * li) / den
    abr_ref[...] = abr
    abi_ref[...] = abi
    for j in range(lr.shape[0]):
        frj = fr[j:j + 1, :]
        fij = fi[j:j + 1, :]
        br = br_ref[j]
        bi = bi_ref[j]
        wr = frj * br - fij * bi
        wi = frj * bi + fij * br
        wre_ref[j, :LANES, :] = wr.astype(BF16)
        wim_ref[j, :LANES, :] = wi.astype(BF16)
        arj = abr[j:j + 1, :]
        aij = abi[j:j + 1, :]
        wre_ref[j, LANES:, :] = (arj * wr - aij * wi).astype(BF16)
        wim_ref[j, LANES:, :] = (arj * wi + aij * wr).astype(BF16)


def _s5_prep(lr, li, ldt, br_bd, bi_bd):
    nt = lr.shape[0]
    vm = pl.BlockSpec(memory_space=pltpu.VMEM)
    return pl.pallas_call(
        _s5_prep_kernel,
        in_specs=[vm] * 5,
        out_specs=[vm] * 4,
        out_shape=[jax.ShapeDtypeStruct((nt, STATE_TILE), F32), jax.ShapeDtypeStruct((nt, STATE_TILE), F32),
                   jax.ShapeDtypeStruct((nt, 2 * LANES, STATE_TILE), BF16),
                   jax.ShapeDtypeStruct((nt, 2 * LANES, STATE_TILE), BF16)],
        name="s5_discretise",
    )(lr, li, ldt, br_bd, bi_bd)


SCAN_LANES = 512


def _scan_interleaved_group(bur_ref, bui_ref, cr_ref, ci_ref, abr_ref, abi_ref, g):
    rc = bur_ref.shape[0]
    upper = lax.broadcasted_iota(jnp.int32, (SUBLANES, SCAN_LANES), 0) >= 4
    sl = slice(g * SCAN_LANES, (g + 1) * SCAN_LANES)
    ar = jnp.broadcast_to(abr_ref[:, sl], (SUBLANES, SCAN_LANES))
    ai = jnp.broadcast_to(abi_ref[:, sl], (SUBLANES, SCAN_LANES))
    a2r = ar * ar - ai * ai
    a2i = ar * ai + ai * ar
    acr = jnp.where(upper, a2r, ar)
    aci = jnp.where(upper, a2i, ai)

    def body(r, carry):
        pr, pi = carry
        rows = pl.ds(pl.multiple_of(r * SUBLANES, SUBLANES), SUBLANES)
        x1r = bur_ref[rows, sl]
        x1i = bui_ref[rows, sl]
        x2r = x1r + (acr * pr - aci * pi)
        x2i = x1i + (acr * pi + aci * pr)
        bur_ref[rows, sl] = x2r
        bui_ref[rows, sl] = x2i
        npr = jnp.where(upper, x2r, pltpu.roll(x2r, 4, 0))
        npi = jnp.where(upper, x2i, pltpu.roll(x2i, 4, 0))
        return npr, npi

    pr, pi = lax.fori_loop(0, rc // SUBLANES, body, (cr_ref[:, sl], ci_ref[:, sl]), unroll=True)
    cr_ref[:, sl] = pr
    ci_ref[:, sl] = pi


def _scan_dense_group(bur_ref, bui_ref, cr_ref, ci_ref, abr_ref, abi_ref, g, bsz):
    rc = bur_ref.shape[0]
    sl = slice(g * SCAN_LANES, (g + 1) * SCAN_LANES)
    ar = abr_ref[:, sl]
    ai = abi_ref[:, sl]
    pr = cr_ref[:, sl]
    pi = ci_ref[:, sl]
    for t in range(rc // bsz):
        rows = slice(t * bsz, (t + 1) * bsz)
        nr = bur_ref[rows, sl] + (ar * pr - ai * pi)
        ni = bui_ref[rows, sl] + (ar * pi + ai * pr)
        bur_ref[rows, sl] = nr
        bui_ref[rows, sl] = ni
        pr, pi = nr, ni
    cr_ref[:, sl] = pr
    ci_ref[:, sl] = pi


def _s5_block(u, abr_ref, abi_ref, wre_ref, wim_ref, cre_ref, cim_ref, d_ref, wg_ref,
              bur_ref, bui_ref, cr_ref, ci_ref, bsz):
    n_tiles = wre_ref.shape[0]
    ub = u.astype(BF16)
    if bsz == 4:
        row = lax.broadcasted_iota(jnp.int32, u.shape, 0)
        ub_prev = jnp.where((row & 4) != 0, pltpu.roll(u, 4, 0), 0.0).astype(BF16)
    ys = []
    for j in range(n_tiles):
        cols = slice(j * LANES, (j + 1) * LANES)
        states = slice(j * STATE_TILE, (j + 1) * STATE_TILE)
        if bsz == 4:
            lhs = jnp.concatenate([ub[:, cols], ub_prev[:, cols]], axis=-1)
            bur_ref[:, states] = jnp.dot(lhs, wre_ref[j], preferred_element_type=F32)
            bui_ref[:, states] = jnp.dot(lhs, wim_ref[j], preferred_element_type=F32)
            _scan_interleaved_group(bur_ref, bui_ref, cr_ref, ci_ref, abr_ref, abi_ref, j)
        else:
            bur_ref[:, states] = jnp.dot(ub[:, cols], wre_ref[j, :LANES, :], preferred_element_type=F32)
            bui_ref[:, states] = jnp.dot(ub[:, cols], wim_ref[j, :LANES, :], preferred_element_type=F32)
            _scan_dense_group(bur_ref, bui_ref, cr_ref, ci_ref, abr_ref, abi_ref, j, bsz)
        hr = bur_ref[:, states].astype(BF16)
        hi = bui_ref[:, states].astype(BF16)
        ys.append(jnp.dot(hr, cre_ref[j], preferred_element_type=F32)
                  - jnp.dot(hi, cim_ref[j], preferred_element_type=F32))
    y = jnp.concatenate(ys, axis=-1) + d_ref[...] * u
    z = 0.5 * y * (1.0 + lax.erf(y * math.sqrt(0.5)))
    gate = jnp.dot(z.astype(BF16), wg_ref[...], preferred_element_type=F32)
    return z * _sigmoid(gate)


def _s5_kernel(u_ref, h0r_ref, h0i_ref, abr_ref, abi_ref, wre_ref, wim_ref, cre_ref, cim_ref, d_ref, wg_ref,
               z_ref, hr_ref, hi_ref, bur_ref, bui_ref, cr_ref, ci_ref, *, bsz):
    step = pl.program_id(0)

    @pl.when(step == 0)
    def _init():
        cr_ref[...] = h0r_ref[...]
        ci_ref[...] = h0i_ref[...]

    zg = _s5_block(u_ref[...], abr_ref, abi_ref, wre_ref, wim_ref, cre_ref, cim_ref, d_ref, wg_ref,
                   bur_ref, bui_ref, cr_ref, ci_ref, bsz)
    z_ref[...] = zg.astype(BF16)

    @pl.when(step == pl.num_programs(0) - 1)
    def _state():
        hr_ref[...] = cr_ref[...]
        hi_ref[...] = ci_ref[...]


def _s5_core(u_tm, h0r, h0i, abr, abi, wre, wim, cre, cim, d_skip, wg_bf, bsz, rc):
    rows, c = u_tm.shape
    n_state = abr.shape[1]
    carry_rows = h0r.shape[0]
    const2 = lambda s: (0, 0)
    const3 = lambda s: (0, 0, 0)
    return pl.pallas_call(
        functools.partial(_s5_kernel, bsz=bsz),
        grid=(rows // rc,),
        in_specs=[pl.BlockSpec((rc, c), lambda s: (s, 0)),
                  pl.BlockSpec((carry_rows, n_state), const2), pl.BlockSpec((carry_rows, n_state), const2),
                  pl.BlockSpec((1, n_state), const2), pl.BlockSpec((1, n_state), const2),
                  pl.BlockSpec(wre.shape, const3), pl.BlockSpec(wim.shape, const3),
                  pl.BlockSpec(cre.shape, const3), pl.BlockSpec(cim.shape, const3),
                  pl.BlockSpec((1, c), const2), pl.BlockSpec(wg_bf.shape, const2)],
        out_specs=[pl.BlockSpec((rc, c), lambda s: (s, 0)),
                   pl.BlockSpec((carry_rows, n_state), const2), pl.BlockSpec((carry_rows, n_state), const2)],
        out_shape=[jax.ShapeDtypeStruct((rows, c), BF16), jax.ShapeDtypeStruct((carry_rows, n_state), F32),
                   jax.ShapeDtypeStruct((carry_rows, n_state), F32)],
        scratch_shapes=[pltpu.VMEM((rc, n_state), F32), pltpu.VMEM((rc, n_state), F32),
                        pltpu.VMEM((carry_rows, n_state), F32), pltpu.VMEM((carry_rows, n_state), F32)],
        compiler_params=_cparams(("arbitrary",)),
        name="s5_core",
    )(u_tm, h0r, h0i, abr, abi, wre, wim, cre, cim, d_skip, wg_bf)


ODD_BATCH = 4
ODD_TIME_CHUNK = 128


def _odd_fused_kernel(x_ref, gpre_ref, win_ref, h0r_ref, h0i_ref, abr_ref, abi_ref, wre_ref, wim_ref, cre_ref,
                      cim_ref, d_ref, wg_ref, wout_ref, gpost_ref, o_ref, hr_ref, hi_ref,
                      stage_ref, bur_ref, bui_ref, cr_ref, ci_ref):
    step = pl.program_id(0)
    tc = x_ref.shape[1]
    n_lane_tiles = stage_ref.shape[0]

    @pl.when(step == 0)
    def _init():
        cr_ref[...] = h0r_ref[...]
        ci_ref[...] = h0i_ref[...]

    for b in range(ODD_BATCH):
        xn = _rms(x_ref[b], gpre_ref[...], RMS_EPS)
        for j in range(n_lane_tiles):
            stage_ref[j, pl.ds(b, tc, stride=ODD_BATCH), :] = xn[:, j * LANES:(j + 1) * LANES]
    xn_tm = jnp.concatenate([stage_ref[j] for j in range(n_lane_tiles)], axis=-1).astype(BF16)
    u = jnp.dot(xn_tm, win_ref[...], preferred_element_type=F32)
    zg = _s5_block(u, abr_ref, abi_ref, wre_ref, wim_ref, cre_ref, cim_ref, d_ref, wg_ref,
                   bur_ref, bui_ref, cr_ref, ci_ref, ODD_BATCH)
    mix = jnp.dot(zg.astype(BF16), wout_ref[...], preferred_element_type=F32)
    normed = _rms(mix, gpost_ref[...], RMS_EPS)
    for j in range(n_lane_tiles):
        stage_ref[j] = normed[:, j * LANES:(j + 1) * LANES]
    for b in range(ODD_BATCH):
        back = jnp.concatenate([stage_ref[j, pl.ds(b, tc, stride=ODD_BATCH), :] for j in range(n_lane_tiles)], axis=-1)
        o_ref[b] = x_ref[b] + back

    @pl.when(step == pl.num_programs(0) - 1)
    def _state():
        hr_ref[...] = cr_ref[...]
        hi_ref[...] = ci_ref[...]


def _odd_fused(x3, h0r, h0i, ow, tc):
    b, s, d = x3.shape
    assert b == ODD_BATCH
    n_state = ow["abr"].shape[1]
    const2 = lambda i: (0, 0)
    const3 = lambda i: (0, 0, 0)
    full = lambda a: pl.BlockSpec(a.shape, const2 if a.ndim == 2 else const3)
    rows = ODD_BATCH * tc
    return pl.pallas_call(
        _odd_fused_kernel,
        grid=(s // tc,),
        in_specs=[pl.BlockSpec((b, tc, d), lambda i: (0, i, 0)), full(ow["g_pre"]), full(ow["w_in"]),
                  full(h0r), full(h0i), full(ow["abr"]), full(ow["abi"]), full(ow["wre"]), full(ow["wim"]),
                  full(ow["cre"]), full(ow["cim"]), full(ow["d"]), full(ow["w_gate"]), full(ow["w_out"]),
                  full(ow["g_post"])],
        out_specs=[pl.BlockSpec((b, tc, d), lambda i: (0, i, 0)), full(h0r), full(h0i)],
        out_shape=[jax.ShapeDtypeStruct((b, s, d), F32), jax.ShapeDtypeStruct(h0r.shape, F32),
                   jax.ShapeDtypeStruct(h0i.shape, F32)],
        scratch_shapes=[pltpu.VMEM((d // LANES, rows, LANES), F32),
                        pltpu.VMEM((rows, n_state), F32), pltpu.VMEM((rows, n_state), F32),
                        pltpu.VMEM(h0r.shape, F32), pltpu.VMEM(h0i.shape, F32)],
        compiler_params=_cparams(("arbitrary",)),
        name="odd_mixer_fused",
    )(x3, ow["g_pre"], ow["w_in"], h0r, h0i, ow["abr"], ow["abi"], ow["wre"], ow["wim"], ow["cre"], ow["cim"],
      ow["d"], ow["w_gate"], ow["w_out"], ow["g_post"])


def _block_diag_tiles(x, rows_first):
    nt, ng, a, b = x.shape
    eye = jnp.eye(ng, dtype=jnp.bool_)
    out = jnp.where(eye[None, :, None, :, None], x[:, :, :, None, :], jnp.zeros((), x.dtype))
    del rows_first
    return out.reshape(nt, ng * a, ng * b)


def _tile(n):
    for cand in (512, 256, 128, 64, 32, 16, 8):
        if n % cand == 0:
            return cand
    return n


def _even_layer_prompt(hp, layer, ew):
    b, s, d = hp.shape
    q, kt, ktb, v, vb, ug = _even_inproj(hp, ew["g_pre"], ew["w_in"], ew["wkt"], _tile(s))
    lam_init = _lambda_init(layer)
    attn = _flash_attention(q, ktb, vb, ew["lq"], ew["subln_g"], lam_init, _tile(s))
    hist = jnp.zeros((b, CONV_HIST, CONV_CH), F32)
    conv, new_buf = _conv_branch(ug, hist, ew["conv_w"], ew["conv_b"], ew["ln_g"], ew["ln_b"], _tile(s))
    tm = _tile(s)
    row = lambda b_, i: (b_, i, 0)
    specs = [pl.BlockSpec((None, tm, ATT_W), row), pl.BlockSpec((None, tm, CONV_CH), row)]
    hp = _outproj([attn, conv], specs, ew["w_out"], ew["g_post"], hp, tm)
    k_new = kt.reshape(b, N_HEADS, 2, QK_DIM, s).transpose(0, 4, 1, 2, 3)
    v_new = v.reshape(b, s, N_HEADS, HEAD_W)
    return hp, k_new, v_new, new_buf


def _even_layer_sample(hs, layer, idx, ew, kcache, vcache, page_table, state_conv):
    nb, n_new, d = hs.shape
    m = nb * n_new
    q, kt, _, v, _, ug = _even_inproj(hs.reshape(1, m, d), ew["g_pre"], ew["w_in"], ew["wkt"], _tile(m))
    lam_init = _lambda_init(layer)
    q5 = q.reshape(nb, n_new, N_HEADS, 2, QK_DIM).transpose(0, 2, 3, 1, 4)
    q5 = jnp.pad(q5, ((0, 0), (0, 0), (0, 0), (0, SUBLANES - n_new), (0, 0)))
    eye = jnp.eye(2 * N_HEADS, dtype=jnp.bool_).reshape(N_HEADS, 2, 1, N_HEADS, 2, 1)
    qbd = jnp.where(eye[None], q5[:, :, :, :, None, None, :], jnp.zeros((), BF16)).reshape(nb, Q_ROWS, ATT_W)
    ktn = kt.reshape(ATT_W, nb, n_new).transpose(1, 0, 2)
    ktn = jnp.pad(ktn, ((0, 0), (0, 0), (0, LANES - n_new))).astype(BF16)
    vn = v.reshape(nb, n_new, N_HEADS, HEAD_W).transpose(0, 2, 1, 3)
    vn = jnp.pad(vn, ((0, 0), (0, 0), (0, LANES - n_new), (0, 0))).astype(BF16)
    attn = _paged_attention(page_table, ew["lq"], ew["subln_g"], qbd, ktn, vn, kcache, vcache, idx, n_new, lam_init)
    attn = attn[:, :n_new, :]
    conv, new_buf = _conv_branch(ug.reshape(nb, n_new, 2 * CONV_CH), state_conv, ew["conv_w"], ew["conv_b"],
                                 ew["ln_g"], ew["ln_b"], n_new)
    tm = _tile(m)
    row = lambda b_, i: (b_, i, 0)
    specs = [pl.BlockSpec((None, tm, ATT_W), row), pl.BlockSpec((None, tm, CONV_CH), row)]
    out = _outproj([attn.reshape(1, m, ATT_W), conv.reshape(1, m, CONV_CH)], specs, ew["w_out"], ew["g_post"],
                   hs.reshape(1, m, d), tm)
    k_new = kt.reshape(N_HEADS, 2, QK_DIM, nb, n_new).transpose(3, 4, 0, 1, 2)
    v_new = v.reshape(nb, n_new, N_HEADS, HEAD_W)
    return out.reshape(nb, n_new, d), k_new, v_new, new_buf


def _odd_layer_prompt(hp, ow):
    b, s, d = hp.shape
    n_state = ow["abr"].shape[1]
    zeros = jnp.zeros((SUBLANES, n_state), F32)
    hp, hr, hi = _odd_fused(hp, zeros, zeros, ow, min(s, ODD_TIME_CHUNK))
    return hp, hr[:b], hi[:b]


def _odd_layer_sample(hs, ow, h0r, h0i):
    nb, n_new, d = hs.shape
    m = nb * n_new
    u_tm = _odd_inproj(hs.transpose(1, 0, 2).reshape(1, m, d), ow["g_pre"], ow["w_in"], _tile(m))
    z, hr, hi = _s5_core(u_tm, h0r, h0i, ow["abr"], ow["abi"], ow["wre"], ow["wim"], ow["cre"], ow["cim"],
                         ow["d"], ow["w_gate"], nb, m)
    z = z.reshape(n_new, nb, d).transpose(1, 0, 2).reshape(1, m, d)
    tm = _tile(m)
    specs = [pl.BlockSpec((None, tm, d), lambda b_, i: (b_, i, 0))]
    out = _outproj([z], specs, ow["w_out"], ow["g_post"], hs.reshape(1, m, d), tm)
    return out.reshape(nb, n_new, d), hr, hi


def kernel(x_prompt, x_sample, cache_k, cache_v, page_table, state_conv, state_ssm_re, state_ssm_im, g_mix_pre, g_mix_post, g_ffn_pre, g_ffn_post, w_in_even, lambda_qk, subln_g, conv_w, conv_b, conv_ln_g, conv_ln_b, w_out_even, w_in_odd, ssm_a_re, ssm_a_im, ssm_b_re, ssm_b_im, ssm_c_re, ssm_c_im, ssm_d, ssm_log_dt, w_gate_odd, w_out_odd, w_ffn_up, w_ffn_down):
    depth, d = g_mix_pre.shape
    hp, hs = x_prompt, x_sample
    nb, n_new, _ = x_sample.shape
    n_layers_even, n_phys, page = cache_k.shape[:3]
    n_groups, n_p = ssm_a_re.shape[1:]
    n_tiles = n_groups // GROUPS_PER_TILE
    n_state = n_groups * n_p

    kcache = cache_k.transpose(0, 1, 3, 4, 5, 2).reshape(n_layers_even, n_phys, ATT_W, page)
    vcache = cache_v.reshape(n_layers_even, n_phys, page * N_HEADS, HEAD_W)

    row = lambda x: x.reshape(1, -1)
    wu_all = w_ffn_up.astype(BF16)
    wd_all = w_ffn_down.astype(BF16)
    outs = {k: [] for k in ("kp", "vp", "ks", "vs", "cp", "cs", "srp", "sip", "srs", "sis")}
    for layer in range(depth):
        i = layer // 2
        if layer % 2 == 0:
            w_in = w_in_even[i].astype(BF16)
            ew = dict(g_pre=row(g_mix_pre[layer]), g_post=row(g_mix_post[layer]), w_in=w_in,
                      wkt=w_in[:, ATT_W:2 * ATT_W].T, lq=lambda_qk[i], subln_g=row(subln_g[i]),
                      conv_w=conv_w[i], conv_b=row(conv_b[i]), ln_g=row(conv_ln_g[i]), ln_b=row(conv_ln_b[i]),
                      w_out=w_out_even[i].astype(BF16))
            hp, k_new, v_new, buf = _even_layer_prompt(hp, layer, ew)
            outs["kp"].append(k_new); outs["vp"].append(v_new); outs["cp"].append(buf)
            hs, k_new, v_new, buf = _even_layer_sample(hs, layer, i, ew, kcache, vcache, page_table, state_conv[i])
            outs["ks"].append(k_new); outs["vs"].append(v_new); outs["cs"].append(buf)
        else:
            lr = ssm_a_re[i].reshape(n_tiles, STATE_TILE)
            li = ssm_a_im[i].reshape(n_tiles, STATE_TILE)
            ldt = jnp.repeat(ssm_log_dt[i], n_p).reshape(n_tiles, STATE_TILE)
            b4 = lambda x: x.reshape(n_tiles, GROUPS_PER_TILE, n_p, GROUP_CH).transpose(0, 1, 3, 2)
            abr, abi, wre, wim = _s5_prep(lr, li, ldt, _block_diag_tiles(b4(ssm_b_re[i]), True),
                                          _block_diag_tiles(b4(ssm_b_im[i]), True))
            c4 = lambda x: x.reshape(n_tiles, GROUPS_PER_TILE, GROUP_CH, n_p).transpose(0, 1, 3, 2)
            ow = dict(g_pre=row(g_mix_pre[layer]), g_post=row(g_mix_post[layer]), w_in=w_in_odd[i].astype(BF16),
                      abr=abr.reshape(1, n_state), abi=abi.reshape(1, n_state), wre=wre, wim=wim,
                      cre=_block_diag_tiles(c4(ssm_c_re[i]), False).astype(BF16),
                      cim=_block_diag_tiles(c4(ssm_c_im[i]), False).astype(BF16),
                      d=row(ssm_d[i]), w_gate=w_gate_odd[i].astype(BF16), w_out=w_out_odd[i].astype(BF16))
            hp, sr, si = _odd_layer_prompt(hp, ow)
            outs["srp"].append(sr.reshape(-1, n_groups, n_p)); outs["sip"].append(si.reshape(-1, n_groups, n_p))
            hs, sr, si = _odd_layer_sample(hs, ow, state_ssm_re[i].reshape(nb, n_state),
                                           state_ssm_im[i].reshape(nb, n_state))
            outs["srs"].append(sr.reshape(nb, n_groups, n_p)); outs["sis"].append(si.reshape(nb, n_groups, n_p))
        gpre, gpost = row(g_ffn_pre[layer]), row(g_ffn_post[layer])
        mp = hp.shape[0] * hp.shape[1]
        hp = _mlp(hp.reshape(mp, d), gpre, wu_all, wd_all, gpost, layer, MLP_TM if mp % MLP_TM == 0 else _tile(mp),
                  MLP_TK).reshape(hp.shape)
        ms = nb * n_new
        hs = _mlp(hs.reshape(ms, d), gpre, wu_all, wd_all, gpost, layer, _tile(ms), MLP_TK).reshape(hs.shape)
    st = jnp.stack
    return (hp, hs, st(outs["kp"]), st(outs["vp"]), st(outs["ks"]), st(outs["vs"]), st(outs["cp"]), st(outs["cs"]),
            st(outs["srp"]), st(outs["sip"]), st(outs["srs"]), st(outs["sis"]))
```

```python
import functools
import math

import jax
import jax.numpy as jnp
import numpy as np
from jax import lax
from jax.experimental import pallas as pl
from jax.experimental.pallas import tpu as pltpu

F32 = jnp.float32
BF16 = jnp.bfloat16

RMS_EPS = 1e-6
SUBLN_EPS = 1e-5
LN_EPS = 1e-5
LOG2_E = math.log2(math.e)

LANES = 128
SUBLANES = 8
N_HEADS = 4
QK_DIM = 64
HEAD_W = 2 * QK_DIM
ATT_W = N_HEADS * HEAD_W
CONV_CH = 512
CONV_TAPS = 31
CONV_HIST = CONV_TAPS - 1
GROUP_CH = 16
STATE_P = 64
GROUPS_PER_TILE = LANES // GROUP_CH
STATE_TILE = GROUPS_PER_TILE * STATE_P
CONV_CHUNK = 32
CONV_PAD = 32
VMEM_LIMIT = 52 * 1024 * 1024


def _cparams(semantics):
    return pltpu.CompilerParams(dimension_semantics=semantics, vmem_limit_bytes=VMEM_LIMIT)


def _rms(x, g, eps):
    ms = jnp.mean(x * x, axis=-1, keepdims=True)
    return (x * lax.rsqrt(ms + eps)) * g


def _sigmoid(x):
    return 1.0 / (1.0 + jnp.exp(-x))


def _lambda_full(lq, lam_init):
    s1 = jnp.sum(lq[0:1, :] * lq[1:2, :], axis=-1, keepdims=True)
    s2 = jnp.sum(lq[2:3, :] * lq[3:4, :], axis=-1, keepdims=True)
    return jnp.exp(s1) - jnp.exp(s2) + lam_init


def _lambda_init(layer):
    return 0.8 - 0.6 * math.exp(-0.3 * layer)


def _even_inproj_kernel(x_ref, g_ref, w_ref, wkt_ref, q_ref, kt_ref, ktb_ref, v_ref, vb_ref, ug_ref):
    xn = _rms(x_ref[...], g_ref[...], RMS_EPS).astype(BF16)

    def proj(c0, width):
        return jnp.dot(xn, w_ref[:, c0:c0 + width], preferred_element_type=F32)

    q_ref[...] = (proj(0, ATT_W) * (QK_DIM ** -0.5 * LOG2_E)).astype(BF16)
    kt = lax.dot_general(wkt_ref[...], xn, (((1,), (1,)), ((), ())), preferred_element_type=F32)
    kt_ref[...] = kt
    ktb_ref[...] = kt.astype(BF16)
    v = proj(2 * ATT_W, ATT_W)
    v_ref[...] = v
    vb_ref[...] = v.astype(BF16)
    ug_ref[...] = proj(3 * ATT_W, 2 * CONV_CH)


def _even_inproj(x3, g, w_bf, wkt_bf, tm):
    bx, sx, d = x3.shape
    n_in = w_bf.shape[1]
    row = lambda b, i: (b, i, 0)
    col = lambda b, i: (b, 0, i)
    const = lambda b, i: (0, 0)
    return pl.pallas_call(
        _even_inproj_kernel,
        grid=(bx, sx // tm),
        in_specs=[pl.BlockSpec((None, tm, d), row), pl.BlockSpec((1, d), const),
                  pl.BlockSpec((d, n_in), const), pl.BlockSpec((ATT_W, d), const)],
        out_specs=[pl.BlockSpec((None, tm, ATT_W), row), pl.BlockSpec((None, ATT_W, tm), col),
                   pl.BlockSpec((None, ATT_W, tm), col), pl.BlockSpec((None, tm, ATT_W), row),
                   pl.BlockSpec((None, tm, ATT_W), row), pl.BlockSpec((None, tm, 2 * CONV_CH), row)],
        out_shape=[jax.ShapeDtypeStruct((bx, sx, ATT_W), BF16), jax.ShapeDtypeStruct((bx, ATT_W, sx), F32),
                   jax.ShapeDtypeStruct((bx, ATT_W, sx), BF16), jax.ShapeDtypeStruct((bx, sx, ATT_W), F32),
                   jax.ShapeDtypeStruct((bx, sx, ATT_W), BF16), jax.ShapeDtypeStruct((bx, sx, 2 * CONV_CH), F32)],
        compiler_params=_cparams(("parallel", "parallel")),
        name="even_inproj",
    )(x3, g, w_bf, wkt_bf)


def _conv_taps(full_ref, shift_ref, w_ref, b_ref, lg_ref, lb_ref, c_ref, tt):
    h0 = CONV_PAD - CONV_HIST
    n_shift = shift_ref.shape[1]
    for phase in range(1, SUBLANES):
        shift_ref[phase - 1] = full_ref[phase:phase + n_shift, :]
    chunk = min(tt, CONV_CHUNK)
    for r0 in range(0, tt, chunk):
        acc = jnp.zeros((chunk, CONV_CH), F32)
        for tap in range(CONV_TAPS):
            tile_row, phase = divmod(h0 + tap, SUBLANES)
            start = tile_row * SUBLANES + r0
            if phase == 0:
                window = full_ref[start:start + chunk, :]
            else:
                window = shift_ref[phase - 1, start:start + chunk, :]
            acc = acc + w_ref[tap:tap + 1, :] * window
        c = acc + b_ref[...]
        mu = jnp.mean(c, axis=-1, keepdims=True)
        xc = c - mu
        y = xc * lax.rsqrt(jnp.mean(xc * xc, axis=-1, keepdims=True) + LN_EPS)
        y = y * lg_ref[...] + lb_ref[...]
        c_ref[r0:r0 + chunk, :] = (y * _sigmoid(y)).astype(BF16)


def _even_front_kernel(*refs, n_prev):
    x_ref, g_ref, w_ref, wkt_ref, cw_ref, cb_ref, lg_ref, lb_ref = refs[:8]
    prev_refs = refs[8:8 + 2 * (n_prev > 0)]
    q_ref, ktb_ref, vb_ref, c_ref, nb_ref, kt_ref, v_ref, full_ref, shift_ref = refs[8 + len(prev_refs):]
    t = pl.program_id(1)
    tt = x_ref.shape[0]
    h0 = CONV_PAD - CONV_HIST
    xn = _rms(x_ref[...], g_ref[...], RMS_EPS).astype(BF16)

    def proj(c0, width):
        return jnp.dot(xn, w_ref[:, c0:c0 + width], preferred_element_type=F32)

    q_ref[...] = (proj(0, ATT_W) * (QK_DIM ** -0.5 * LOG2_E)).astype(BF16)
    kt = lax.dot_general(wkt_ref[...], xn, (((1,), (1,)), ((), ())), preferred_element_type=F32)
    for layer in range(n_prev):
        kt_ref[layer] = prev_refs[0][layer]
        v_ref[layer] = prev_refs[1][layer]
    kt_ref[n_prev] = kt
    ktb_ref[...] = kt.astype(BF16)
    v = proj(2 * ATT_W, ATT_W)
    vb_ref[...] = v.astype(BF16)
    for h in range(N_HEADS):
        v_ref[n_prev, pl.ds(h, tt, stride=N_HEADS), :] = v[:, h * HEAD_W:(h + 1) * HEAD_W]

    @pl.when(t == 0)
    def _first():
        full_ref[h0:CONV_PAD, :] = jnp.zeros((CONV_HIST, CONV_CH), F32)

    @pl.when(t > 0)
    def _carry():
        full_ref[h0:CONV_PAD, :] = full_ref[tt + h0:tt + CONV_PAD, :]

    full_ref[CONV_PAD:CONV_PAD + tt, :] = proj(3 * ATT_W, CONV_CH) * _sigmoid(proj(3 * ATT_W + CONV_CH, CONV_CH))
    _conv_taps(full_ref, shift_ref, cw_ref, cb_ref, lg_ref, lb_ref, c_ref, tt)

    @pl.when(t == pl.num_programs(1) - 1)
    def _tail():
        nb_ref[...] = full_ref[tt + h0:tt + CONV_PAD, :]


def _even_front(x3, ew, k_prev, v_prev, tt):
    b, s, d = x3.shape
    w_bf = ew["w_in"]
    n_prev = 0 if k_prev is None else k_prev.shape[0]
    row = lambda b_, t: (b_, t, 0)
    col = lambda b_, t: (b_, 0, t)
    per_b = lambda b_, t: (b_, 0, 0)
    const = lambda b_, t: (0, 0)
    k_block = lambda n: pl.BlockSpec((n, None, ATT_W, tt), lambda b_, t: (0, b_, 0, t))
    v_block = lambda n: pl.BlockSpec((n, None, tt * N_HEADS, HEAD_W), lambda b_, t: (0, b_, t, 0))
    prev = [] if n_prev == 0 else [k_prev, v_prev]
    prev_specs = [] if n_prev == 0 else [k_block(n_prev), v_block(n_prev)]
    return pl.pallas_call(
        functools.partial(_even_front_kernel, n_prev=n_prev),
        grid=(b, s // tt),
        in_specs=[pl.BlockSpec((None, tt, d), row), pl.BlockSpec((1, d), const),
                  pl.BlockSpec(w_bf.shape, const), pl.BlockSpec((ATT_W, d), const),
                  pl.BlockSpec((CONV_TAPS, CONV_CH), const), pl.BlockSpec((1, CONV_CH), const),
                  pl.BlockSpec((1, CONV_CH), const), pl.BlockSpec((1, CONV_CH), const)] + prev_specs,
        out_specs=[pl.BlockSpec((None, tt, ATT_W), row), pl.BlockSpec((None, ATT_W, tt), col),
                   pl.BlockSpec((None, tt, ATT_W), row), pl.BlockSpec((None, tt, CONV_CH), row),
                   pl.BlockSpec((None, CONV_HIST, CONV_CH), per_b), k_block(n_prev + 1), v_block(n_prev + 1)],
        out_shape=[jax.ShapeDtypeStruct((b, s, ATT_W), BF16), jax.ShapeDtypeStruct((b, ATT_W, s), BF16),
                   jax.ShapeDtypeStruct((b, s, ATT_W), BF16), jax.ShapeDtypeStruct((b, s, CONV_CH), BF16),
                   jax.ShapeDtypeStruct((b, CONV_HIST, CONV_CH), F32),
                   jax.ShapeDtypeStruct((n_prev + 1, b, ATT_W, s), F32),
                   jax.ShapeDtypeStruct((n_prev + 1, b, s * N_HEADS, HEAD_W), F32)],
        scratch_shapes=[pltpu.VMEM((CONV_PAD + tt, CONV_CH), F32),
                        pltpu.VMEM((SUBLANES - 1, CONV_PAD - SUBLANES + tt, CONV_CH), F32)],
        compiler_params=_cparams(("parallel", "arbitrary")),
        name="even_front",
    )(x3, ew["g_pre"], w_bf, ew["wkt"], ew["conv_w"], ew["conv_b"], ew["ln_g"], ew["ln_b"], *prev)


def _flash_kernel(it_ref, jt_ref, lq_ref, g_ref, q_ref, kt_ref, v_ref, o_ref, qm_ref, m_ref, l_ref, acc_ref,
                  *, lam_init):
    t = pl.program_id(1)
    i = it_ref[t]
    j = jt_ref[t]
    heads = [slice(h * HEAD_W, (h + 1) * HEAD_W) for h in range(N_HEADS)]

    @pl.when(j == 0)
    def _init():
        q = q_ref[...]
        lane = lax.broadcasted_iota(jnp.int32, (q.shape[0], HEAD_W), 1)
        first = jnp.where(lane < QK_DIM, 1.0, 0.0).astype(BF16)
        for h, hs in enumerate(heads):
            qm_ref[2 * h] = q[:, hs] * first
            qm_ref[2 * h + 1] = q[:, hs] * (1.0 - first)
        m_ref[...] = jnp.full(m_ref.shape, -jnp.inf, F32)
        l_ref[...] = jnp.zeros(l_ref.shape, F32)
        acc_ref[...] = jnp.zeros(acc_ref.shape, F32)

    def step(masked):
        n_blk = kt_ref.shape[1] // LANES
        for h, hs in enumerate(heads):
            kt = kt_ref[hs, :]
            v = v_ref[:, hs]
            for c in range(2):
                hc = 2 * h + c
                s = jnp.dot(qm_ref[hc], kt, preferred_element_type=F32)
                if masked:
                    row = lax.broadcasted_iota(jnp.int32, s.shape, 0)
                    col = lax.broadcasted_iota(jnp.int32, s.shape, 1)
                    s = jnp.where(col <= row, s, -jnp.inf)
                blocks = [s[:, b * LANES:(b + 1) * LANES] for b in range(n_blk)]
                mx = functools.reduce(jnp.maximum, blocks)
                m_prev = m_ref[hc]
                m_new = jnp.maximum(m_prev, jnp.max(mx, axis=-1, keepdims=True))
                alpha = jnp.exp2(m_prev - m_new)
                ps = [jnp.exp2(blk - m_new) for blk in blocks]
                l_ref[hc] = alpha * l_ref[hc] + functools.reduce(jnp.add, ps)
                p = jnp.concatenate([x.astype(BF16) for x in ps], axis=-1)
                acc_ref[hc] = alpha * acc_ref[hc] + jnp.dot(p, v, preferred_element_type=F32)
                m_ref[hc] = m_new

    @pl.when(j < i)
    def _full():
        step(False)

    @pl.when(j == i)
    def _diag():
        step(True)
        lam = _lambda_full(lq_ref[...], lam_init)
        for h, hs in enumerate(heads):
            l0 = jnp.sum(l_ref[2 * h], axis=-1, keepdims=True)
            l1 = jnp.sum(l_ref[2 * h + 1], axis=-1, keepdims=True)
            o = acc_ref[2 * h] / l0 - lam * (acc_ref[2 * h + 1] / l1)
            o_ref[:, hs] = (_rms(o, g_ref[...], SUBLN_EPS) * (1.0 - lam_init)).astype(BF16)


def _flash_attention(q, kt, v, lq, subln_g, lam_init, tq):
    b, s, _ = q.shape
    nq = s // tq
    pairs = [(i, j) for i in range(nq) for j in range(i + 1)]
    it = jnp.asarray(np.array([p[0] for p in pairs], np.int32))
    jt = jnp.asarray(np.array([p[1] for p in pairs], np.int32))
    grid_spec = pltpu.PrefetchScalarGridSpec(
        num_scalar_prefetch=2,
        grid=(b, len(pairs)),
        in_specs=[pl.BlockSpec((4, QK_DIM), lambda b_, t, it_, jt_: (0, 0)),
                  pl.BlockSpec((1, HEAD_W), lambda b_, t, it_, jt_: (0, 0)),
                  pl.BlockSpec((None, tq, ATT_W), lambda b_, t, it_, jt_: (b_, it_[t], 0)),
                  pl.BlockSpec((None, ATT_W, tq), lambda b_, t, it_, jt_: (b_, 0, jt_[t])),
                  pl.BlockSpec((None, tq, ATT_W), lambda b_, t, it_, jt_: (b_, jt_[t], 0))],
        out_specs=pl.BlockSpec((None, tq, ATT_W), lambda b_, t, it_, jt_: (b_, it_[t], 0)),
        scratch_shapes=[pltpu.VMEM((2 * N_HEADS, tq, HEAD_W), BF16), pltpu.VMEM((2 * N_HEADS, tq, LANES), F32),
                        pltpu.VMEM((2 * N_HEADS, tq, LANES), F32), pltpu.VMEM((2 * N_HEADS, tq, HEAD_W), F32)],
    )
    return pl.pallas_call(
        functools.partial(_flash_kernel, lam_init=lam_init),
        grid_spec=grid_spec,
        out_shape=jax.ShapeDtypeStruct((b, s, ATT_W), BF16),
        compiler_params=_cparams(("parallel", "arbitrary")),
        name="flash_diff_attn",
    )(it, jt, lq, subln_g, q, kt, v)


PAGES_PER_STEP = 16
Q_ROWS = N_HEADS * 2 * SUBLANES


def _paged_kernel(pt_ref, lq_ref, g_ref, qbd_ref, ktn_ref, vn_ref, *rest, n_steps, n_new, lam_init):
    k_refs = rest[:PAGES_PER_STEP]
    v_refs = rest[PAGES_PER_STEP:2 * PAGES_PER_STEP]
    o_ref, m_ref, l_ref, acc_ref = rest[2 * PAGES_PER_STEP:]
    del pt_ref
    step = pl.program_id(1)
    rows_per_head = 2 * SUBLANES

    @pl.when(step == 0)
    def _init():
        m_ref[...] = jnp.full(m_ref.shape, -jnp.inf, F32)
        l_ref[...] = jnp.zeros(l_ref.shape, F32)
        acc_ref[...] = jnp.zeros(acc_ref.shape, F32)

    def update(scores, values):
        m_prev = m_ref[...]
        m_new = m_prev
        for s in scores:
            m_new = jnp.maximum(m_new, jnp.max(s, axis=-1, keepdims=True))
        alpha = jnp.exp2(m_prev - m_new)
        l_new = alpha * l_ref[...]
        acc = alpha * acc_ref[...]
        parts = [acc[h * rows_per_head:(h + 1) * rows_per_head] for h in range(N_HEADS)]
        for s, vh in zip(scores, values):
            p = jnp.exp2(s - m_new)
            l_new = l_new + jnp.sum(p, axis=-1, keepdims=True)
            pb = p.astype(BF16)
            for h in range(N_HEADS):
                parts[h] = parts[h] + jnp.dot(pb[h * rows_per_head:(h + 1) * rows_per_head], vh[h],
                                              preferred_element_type=F32)
        m_ref[...] = m_new
        l_ref[...] = l_new
        acc_ref[...] = jnp.concatenate(parts, axis=0)

    qbd = qbd_ref[...]

    @pl.when(step < n_steps)
    def _pages():
        scores, values = [], []
        for kr, vr in zip(k_refs, v_refs):
            scores.append(jnp.dot(qbd, kr[...].astype(BF16), preferred_element_type=F32))
            values.append([vr[pl.ds(h, LANES, stride=N_HEADS), :].astype(BF16) for h in range(N_HEADS)])
        update(scores, values)

    @pl.when(step == n_steps)
    def _new_tokens():
        s = jnp.dot(qbd, ktn_ref[...], preferred_element_type=F32)
        tok = lax.broadcasted_iota(jnp.int32, s.shape, 0) % SUBLANES
        col = lax.broadcasted_iota(jnp.int32, s.shape, 1)
        s = jnp.where((col <= tok) & (col < n_new), s, -jnp.inf)
        update([s], [[vn_ref[h] for h in range(N_HEADS)]])
        lam = _lambda_full(lq_ref[...], lam_init)
        acc = acc_ref[...]
        l = l_ref[...]
        g = g_ref[...]
        for h in range(N_HEADS):
            r0 = h * rows_per_head
            o1 = acc[r0:r0 + SUBLANES] / l[r0:r0 + SUBLANES]
            o2 = acc[r0 + SUBLANES:r0 + 2 * SUBLANES] / l[r0 + SUBLANES:r0 + 2 * SUBLANES]
            o = o1 - lam * o2
            o_ref[:, h * HEAD_W:(h + 1) * HEAD_W] = _rms(o, g, SUBLN_EPS) * (1.0 - lam_init)


def _paged_attention(page_table, lq, subln_g, qbd, ktn, vn, kcache, vcache, layer, n_new, lam_init):
    nb, n_pages = page_table.shape
    n_steps = n_pages // PAGES_PER_STEP
    page_rows = kcache.shape[2]

    def page_map(jj):
        def index_map(b, s, pt):
            return (layer, pt[b, jnp.minimum(s, n_steps - 1) * PAGES_PER_STEP + jj], 0, 0)
        return index_map

    per_b3 = lambda b, s, pt: (b, 0, 0)
    per_b4 = lambda b, s, pt: (b, 0, 0, 0)
    const = lambda b, s, pt: (0, 0)
    page_specs = [pl.BlockSpec((None, None, page_rows, LANES), page_map(jj)) for jj in range(PAGES_PER_STEP)]
    grid_spec = pltpu.PrefetchScalarGridSpec(
        num_scalar_prefetch=1,
        grid=(nb, n_steps + 1),
        in_specs=[pl.BlockSpec((4, QK_DIM), const), pl.BlockSpec((1, HEAD_W), const),
                  pl.BlockSpec((None, Q_ROWS, ATT_W), per_b3), pl.BlockSpec((None, ATT_W, LANES), per_b3),
                  pl.BlockSpec((None, N_HEADS, LANES, HEAD_W), per_b4)] + page_specs + page_specs,
        out_specs=pl.BlockSpec((None, SUBLANES, ATT_W), per_b3),
        scratch_shapes=[pltpu.VMEM((Q_ROWS, 1), F32), pltpu.VMEM((Q_ROWS, 1), F32),
                        pltpu.VMEM((Q_ROWS, HEAD_W), F32)],
    )
    return pl.pallas_call(
        functools.partial(_paged_kernel, n_steps=n_steps, n_new=n_new, lam_init=lam_init),
        grid_spec=grid_spec,
        out_shape=jax.ShapeDtypeStruct((nb, SUBLANES, ATT_W), F32),
        compiler_params=_cparams(("parallel", "arbitrary")),
        name="paged_diff_attn",
    )(page_table, lq, subln_g, qbd, ktn, vn, *([kcache] * PAGES_PER_STEP), *([vcache] * PAGES_PER_STEP))


def _conv_kernel(ug_ref, hist_ref, w_ref, b_ref, lg_ref, lb_ref, c_ref, nb_ref, full_ref, shift_ref):
    t = pl.program_id(1)
    tt = ug_ref.shape[0]
    h0 = CONV_PAD - CONV_HIST

    @pl.when(t == 0)
    def _first():
        full_ref[h0:CONV_PAD, :] = hist_ref[...]

    @pl.when(t > 0)
    def _carry():
        full_ref[h0:CONV_PAD, :] = full_ref[tt + h0:tt + CONV_PAD, :]

    ug = ug_ref[...]
    full_ref[CONV_PAD:CONV_PAD + tt, :] = ug[:, :CONV_CH] * _sigmoid(ug[:, CONV_CH:])
    _conv_taps(full_ref, shift_ref, w_ref, b_ref, lg_ref, lb_ref, c_ref, tt)

    @pl.when(t == pl.num_programs(1) - 1)
    def _tail():
        nb_ref[...] = full_ref[tt + h0:tt + CONV_PAD, :]


def _conv_branch(ug, hist, w, b, lg, lb, tt):
    bx, sx, _ = ug.shape
    row = lambda b_, t: (b_, t, 0)
    per_b = lambda b_, t: (b_, 0, 0)
    const = lambda b_, t: (0, 0)
    return pl.pallas_call(
        _conv_kernel,
        grid=(bx, sx // tt),
        in_specs=[pl.BlockSpec((None, tt, 2 * CONV_CH), row), pl.BlockSpec((None, CONV_HIST, CONV_CH), per_b),
                  pl.BlockSpec((CONV_TAPS, CONV_CH), const), pl.BlockSpec((1, CONV_CH), const),
                  pl.BlockSpec((1, CONV_CH), const), pl.BlockSpec((1, CONV_CH), const)],
        out_specs=[pl.BlockSpec((None, tt, CONV_CH), row), pl.BlockSpec((None, CONV_HIST, CONV_CH), per_b)],
        out_shape=[jax.ShapeDtypeStruct((bx, sx, CONV_CH), BF16), jax.ShapeDtypeStruct((bx, CONV_HIST, CONV_CH), F32)],
        scratch_shapes=[pltpu.VMEM((CONV_PAD + tt, CONV_CH), F32),
                        pltpu.VMEM((SUBLANES - 1, CONV_PAD - SUBLANES + tt, CONV_CH), F32)],
        compiler_params=_cparams(("parallel", "arbitrary")),
        name="conv_branch",
    )(ug, hist, w, b, lg, lb)


def _outproj_kernel(*refs, widths):
    a_refs = refs[:len(widths)]
    w_ref, g_ref, h_ref, o_ref = refs[len(widths):]
    acc = None
    r0 = 0
    for a_ref, width in zip(a_refs, widths):
        part = jnp.dot(a_ref[...].astype(BF16), w_ref[r0:r0 + width, :], preferred_element_type=F32)
        acc = part if acc is None else acc + part
        r0 += width
    o_ref[...] = h_ref[...] + _rms(acc, g_ref[...], RMS_EPS)


def _outproj(a_list, a_specs, w_bf, g, h3, tm):
    bx, sx, d = h3.shape
    widths = tuple(spec.block_shape[-1] for spec in a_specs)
    row = lambda b, i: (b, i, 0)
    const = lambda b, i: (0, 0)
    return pl.pallas_call(
        functools.partial(_outproj_kernel, widths=widths),
        grid=(bx, sx // tm),
        in_specs=list(a_specs) + [pl.BlockSpec(w_bf.shape, const), pl.BlockSpec((1, d), const),
                                  pl.BlockSpec((None, tm, d), row)],
        out_specs=pl.BlockSpec((None, tm, d), row),
        out_shape=jax.ShapeDtypeStruct((bx, sx, d), F32),
        compiler_params=_cparams(("parallel", "parallel")),
        name="mixer_outproj",
    )(*a_list, w_bf, g, h3)


MLP_TM = 1024
MLP_TK = 1024


def _mlp_kernel(x_ref, gpre_ref, wu_ref, wd_ref, gpost_ref, o_ref, xn_ref, acc_ref):
    k = pl.program_id(1)

    @pl.when(k == 0)
    def _init():
        xn_ref[...] = _rms(x_ref[...], gpre_ref[...], RMS_EPS).astype(BF16)
        acc_ref[...] = jnp.zeros(acc_ref.shape, F32)

    up = jnp.dot(xn_ref[...], wu_ref[...], preferred_element_type=F32)
    act = jnp.square(jnp.maximum(up, 0.0)).astype(BF16)
    acc_ref[...] += jnp.dot(act, wd_ref[...], preferred_element_type=F32)

    @pl.when(k == pl.num_programs(1) - 1)
    def _done():
        o_ref[...] = x_ref[...] + _rms(acc_ref[...], gpost_ref[...], RMS_EPS)


def _mlp(x2, gpre, wu_bf, wd_bf, gpost, layer, tm, tk):
    m, d = x2.shape
    dff = wu_bf.shape[2]
    return pl.pallas_call(
        _mlp_kernel,
        grid=(m // tm, dff // tk),
        in_specs=[pl.BlockSpec((tm, d), lambda i, k: (i, 0)), pl.BlockSpec((1, d), lambda i, k: (0, 0)),
                  pl.BlockSpec((None, d, tk), lambda i, k: (layer, 0, k)),
                  pl.BlockSpec((None, tk, d), lambda i, k: (layer, k, 0)),
                  pl.BlockSpec((1, d), lambda i, k: (0, 0))],
        out_specs=pl.BlockSpec((tm, d), lambda i, k: (i, 0)),
        out_shape=jax.ShapeDtypeStruct((m, d), F32),
        scratch_shapes=[pltpu.VMEM((tm, d), BF16), pltpu.VMEM((tm, d), F32)],
        compiler_params=_cparams(("parallel", "arbitrary")),
        name="sq_relu_mlp",
    )(x2, gpre, wu_bf, wd_bf, gpost)


def _odd_inproj_kernel(x_ref, g_ref, w_ref, u_ref):
    xn = _rms(x_ref[...], g_ref[...], RMS_EPS).astype(BF16)
    u_ref[...] = jnp.dot(xn, w_ref[...], preferred_element_type=F32)


def _odd_inproj(x3, g, w_bf, tm):
    bx, sx, d = x3.shape
    c = w_bf.shape[1]
    out = pl.pallas_call(
        _odd_inproj_kernel,
        grid=(bx, sx // tm),
        in_specs=[pl.BlockSpec((None, tm, d), lambda b, i: (b, i, 0)), pl.BlockSpec((1, d), lambda b, i: (0, 0)),
                  pl.BlockSpec((d, c), lambda b, i: (0, 0))],
        out_specs=pl.BlockSpec((tm, c), lambda b, i: (i, b)),
        out_shape=jax.ShapeDtypeStruct((sx, bx * c), F32),
        compiler_params=_cparams(("parallel", "parallel")),
        name="odd_inproj",
    )(x3, g, w_bf)
    return out.reshape(sx * bx, c)


def _s5_prep_kernel(lr_ref, li_ref, ldt_ref, br_ref, bi_ref, abr_ref, abi_ref, wre_ref, wim_ref):
    lr = lr_ref[...]
    li = li_ref[...]
    dt = jnp.exp(ldt_ref[...])
    mag = jnp.exp(lr * dt)
    ang = li * dt
    abr = mag * jnp.cos(ang)
    abi = mag * jnp.sin(ang)
    den = lr * lr + li * li
    nr = abr - 1.0
    ni = abi
    fr = (nr * lr + ni * li) / den
    fi = (ni * lr - nr * li) / den
    abr_ref[...] = abr
    abi_ref[...] = abi
    for j in range(lr.shape[0]):
        frj = fr[j:j + 1, :]
        fij = fi[j:j + 1, :]
        br = br_ref[j]
        bi = bi_ref[j]
        wr = frj * br - fij * bi
        wi = frj * bi + fij * br
        wre_ref[j, :LANES, :] = wr.astype(BF16)
        wim_ref[j, :LANES, :] = wi.astype(BF16)
        arj = abr[j:j + 1, :]
        aij = abi[j:j + 1, :]
        wre_ref[j, LANES:, :] = (arj * wr - aij * wi).astype(BF16)
        wim_ref[j, LANES:, :] = (arj * wi + aij * wr).astype(BF16)


def _s5_prep(lr, li, ldt, br_bd, bi_bd):
    nt = lr.shape[0]
    vm = pl.BlockSpec(memory_space=pltpu.VMEM)
    return pl.pallas_call(
        _s5_prep_kernel,
        in_specs=[vm] * 5,
        out_specs=[vm] * 4,
        out_shape=[jax.ShapeDtypeStruct((nt, STATE_TILE), F32), jax.ShapeDtypeStruct((nt, STATE_TILE), F32),
                   jax.ShapeDtypeStruct((nt, 2 * LANES, STATE_TILE), BF16),
                   jax.ShapeDtypeStruct((nt, 2 * LANES, STATE_TILE), BF16)],
        name="s5_discretise",
    )(lr, li, ldt, br_bd, bi_bd)


SCAN_LANES = 512


def _scan_interleaved_group(bur_ref, bui_ref, cr_ref, ci_ref, abr_ref, abi_ref, g):
    rc = bur_ref.shape[0]
    upper = lax.broadcasted_iota(jnp.int32, (SUBLANES, SCAN_LANES), 0) >= 4
    sl = slice(g * SCAN_LANES, (g + 1) * SCAN_LANES)
    ar = jnp.broadcast_to(abr_ref[:, sl], (SUBLANES, SCAN_LANES))
    ai = jnp.broadcast_to(abi_ref[:, sl], (SUBLANES, SCAN_LANES))
    a2r = ar * ar - ai * ai
    a2i = ar * ai + ai * ar
    acr = jnp.where(upper, a2r, ar)
    aci = jnp.where(upper, a2i, ai)

    def body(r, carry):
        pr, pi = carry
        rows = pl.ds(pl.multiple_of(r * SUBLANES, SUBLANES), SUBLANES)
        x1r = bur_ref[rows, sl]
        x1i = bui_ref[rows, sl]
        x2r = x1r + (acr * pr - aci * pi)
        x2i = x1i + (acr * pi + aci * pr)
        bur_ref[rows, sl] = x2r
        bui_ref[rows, sl] = x2i
        npr = jnp.where(upper, x2r, pltpu.roll(x2r, 4, 0))
        npi = jnp.where(upper, x2i, pltpu.roll(x2i, 4, 0))
        return npr, npi

    pr, pi = lax.fori_loop(0, rc // SUBLANES, body, (cr_ref[:, sl], ci_ref[:, sl]), unroll=True)
    cr_ref[:, sl] = pr
    ci_ref[:, sl] = pi


def _scan_dense_group(bur_ref, bui_ref, cr_ref, ci_ref, abr_ref, abi_ref, g, bsz):
    rc = bur_ref.shape[0]
    sl = slice(g * SCAN_LANES, (g + 1) * SCAN_LANES)
    ar = abr_ref[:, sl]
    ai = abi_ref[:, sl]
    pr = cr_ref[:, sl]
    pi = ci_ref[:, sl]
    for t in range(rc // bsz):
        rows = slice(t * bsz, (t + 1) * bsz)
        nr = bur_ref[rows, sl] + (ar * pr - ai * pi)
        ni = bui_ref[rows, sl] + (ar * pi + ai * pr)
        bur_ref[rows, sl] = nr
        bui_ref[rows, sl] = ni
        pr, pi = nr, ni
    cr_ref[:, sl] = pr
    ci_ref[:, sl] = pi


def _s5_block(u, abr_ref, abi_ref, wre_ref, wim_ref, cre_ref, cim_ref, d_ref, wg_ref,
              bur_ref, bui_ref, cr_ref, ci_ref, bsz):
    n_tiles = wre_ref.shape[0]
    ub = u.astype(BF16)
    if bsz == 4:
        row = lax.broadcasted_iota(jnp.int32, u.shape, 0)
        ub_prev = jnp.where((row & 4) != 0, pltpu.roll(u, 4, 0), 0.0).astype(BF16)
    ys = []
    for j in range(n_tiles):
        cols = slice(j * LANES, (j + 1) * LANES)
        states = slice(j * STATE_TILE, (j + 1) * STATE_TILE)
        if bsz == 4:
            lhs = jnp.concatenate([ub[:, cols], ub_prev[:, cols]], axis=-1)
            bur_ref[:, states] = jnp.dot(lhs, wre_ref[j], preferred_element_type=F32)
            bui_ref[:, states] = jnp.dot(lhs, wim_ref[j], preferred_element_type=F32)
            _scan_interleaved_group(bur_ref, bui_ref, cr_ref, ci_ref, abr_ref, abi_ref, j)
        else:
            bur_ref[:, states] = jnp.dot(ub[:, cols], wre_ref[j, :LANES, :], preferred_element_type=F32)
            bui_ref[:, states] = jnp.dot(ub[:, cols], wim_ref[j, :LANES, :], preferred_element_type=F32)
            _scan_dense_group(bur_ref, bui_ref, cr_ref, ci_ref, abr_ref, abi_ref, j, bsz)
        hr = bur_ref[:, states].astype(BF16)
        hi = bui_ref[:, states].astype(BF16)
        ys.append(jnp.dot(hr, cre_ref[j], preferred_element_type=F32)
                  - jnp.dot(hi, cim_ref[j], preferred_element_type=F32))
    y = jnp.concatenate(ys, axis=-1) + d_ref[...] * u
    z = 0.5 * y * (1.0 + lax.erf(y * math.sqrt(0.5)))
    gate = jnp.dot(z.astype(BF16), wg_ref[...], preferred_element_type=F32)
    return z * _sigmoid(gate)


def _s5_kernel(u_ref, h0r_ref, h0i_ref, abr_ref, abi_ref, wre_ref, wim_ref, cre_ref, cim_ref, d_ref, wg_ref,
               z_ref, hr_ref, hi_ref, bur_ref, bui_ref, cr_ref, ci_ref, *, bsz):
    step = pl.program_id(0)

    @pl.when(step == 0)
    def _init():
        cr_ref[...] = h0r_ref[...]
        ci_ref[...] = h0i_ref[...]

    zg = _s5_block(u_ref[...], abr_ref, abi_ref, wre_ref, wim_ref, cre_ref, cim_ref, d_ref, wg_ref,
                   bur_ref, bui_ref, cr_ref, ci_ref, bsz)
    z_ref[...] = zg.astype(BF16)

    @pl.when(step == pl.num_programs(0) - 1)
    def _state():
        hr_ref[...] = cr_ref[...]
        hi_ref[...] = ci_ref[...]


def _s5_core(u_tm, h0r, h0i, abr, abi, wre, wim, cre, cim, d_skip, wg_bf, bsz, rc):
    rows, c = u_tm.shape
    n_state = abr.shape[1]
    carry_rows = h0r.shape[0]
    const2 = lambda s: (0, 0)
    const3 = lambda s: (0, 0, 0)
    return pl.pallas_call(
        functools.partial(_s5_kernel, bsz=bsz),
        grid=(rows // rc,),
        in_specs=[pl.BlockSpec((rc, c), lambda s: (s, 0)),
                  pl.BlockSpec((carry_rows, n_state), const2), pl.BlockSpec((carry_rows, n_state), const2),
                  pl.BlockSpec((1, n_state), const2), pl.BlockSpec((1, n_state), const2),
                  pl.BlockSpec(wre.shape, const3), pl.BlockSpec(wim.shape, const3),
                  pl.BlockSpec(cre.shape, const3), pl.BlockSpec(cim.shape, const3),
                  pl.BlockSpec((1, c), const2), pl.BlockSpec(wg_bf.shape, const2)],
        out_specs=[pl.BlockSpec((rc, c), lambda s: (s, 0)),
                   pl.BlockSpec((carry_rows, n_state), const2), pl.BlockSpec((carry_rows, n_state), const2)],
        out_shape=[jax.ShapeDtypeStruct((rows, c), BF16), jax.ShapeDtypeStruct((carry_rows, n_state), F32),
                   jax.ShapeDtypeStruct((carry_rows, n_state), F32)],
        scratch_shapes=[pltpu.VMEM((rc, n_state), F32), pltpu.VMEM((rc, n_state), F32),
                        pltpu.VMEM((carry_rows, n_state), F32), pltpu.VMEM((carry_rows, n_state), F32)],
        compiler_params=_cparams(("arbitrary",)),
        name="s5_core",
    )(u_tm, h0r, h0i, abr, abi, wre, wim, cre, cim, d_skip, wg_bf)


ODD_BATCH = 4
ODD_TIME_CHUNK = 128


def _odd_fused_kernel(x_ref, gpre_ref, win_ref, h0r_ref, h0i_ref, abr_ref, abi_ref, wre_ref, wim_ref, cre_ref,
                      cim_ref, d_ref, wg_ref, wout_ref, gpost_ref, o_ref, hr_ref, hi_ref,
                      stage_ref, bur_ref, bui_ref, cr_ref, ci_ref):
    step = pl.program_id(0)
    tc = x_ref.shape[1]
    n_lane_tiles = stage_ref.shape[0]

    @pl.when(step == 0)
    def _init():
        cr_ref[...] = h0r_ref[...]
        ci_ref[...] = h0i_ref[...]

    for b in range(ODD_BATCH):
        xn = _rms(x_ref[b], gpre_ref[...], RMS_EPS)
        for j in range(n_lane_tiles):
            stage_ref[j, pl.ds(b, tc, stride=ODD_BATCH), :] = xn[:, j * LANES:(j + 1) * LANES]
    xn_tm = jnp.concatenate([stage_ref[j] for j in range(n_lane_tiles)], axis=-1).astype(BF16)
    u = jnp.dot(xn_tm, win_ref[...], preferred_element_type=F32)
    zg = _s5_block(u, abr_ref, abi_ref, wre_ref, wim_ref, cre_ref, cim_ref, d_ref, wg_ref,
                   bur_ref, bui_ref, cr_ref, ci_ref, ODD_BATCH)
    mix = jnp.dot(zg.astype(BF16), wout_ref[...], preferred_element_type=F32)
    normed = _rms(mix, gpost_ref[...], RMS_EPS)
    for j in range(n_lane_tiles):
        stage_ref[j] = normed[:, j * LANES:(j + 1) * LANES]
    for b in range(ODD_BATCH):
        back = jnp.concatenate([stage_ref[j, pl.ds(b, tc, stride=ODD_BATCH), :] for j in range(n_lane_tiles)], axis=-1)
        o_ref[b] = x_ref[b] + back

    @pl.when(step == pl.num_programs(0) - 1)
    def _state():
        hr_ref[...] = cr_ref[...]
        hi_ref[...] = ci_ref[...]


def _odd_fused(x3, h0r, h0i, ow, tc):
    b, s, d = x3.shape
    assert b == ODD_BATCH
    n_state = ow["abr"].shape[1]
    const2 = lambda i: (0, 0)
    const3 = lambda i: (0, 0, 0)
    full = lambda a: pl.BlockSpec(a.shape, const2 if a.ndim == 2 else const3)
    rows = ODD_BATCH * tc
    return pl.pallas_call(
        _odd_fused_kernel,
        grid=(s // tc,),
        in_specs=[pl.BlockSpec((b, tc, d), lambda i: (0, i, 0)), full(ow["g_pre"]), full(ow["w_in"]),
                  full(h0r), full(h0i), full(ow["abr"]), full(ow["abi"]), full(ow["wre"]), full(ow["wim"]),
                  full(ow["cre"]), full(ow["cim"]), full(ow["d"]), full(ow["w_gate"]), full(ow["w_out"]),
                  full(ow["g_post"])],
        out_specs=[pl.BlockSpec((b, tc, d), lambda i: (0, i, 0)), full(h0r), full(h0i)],
        out_shape=[jax.ShapeDtypeStruct((b, s, d), F32), jax.ShapeDtypeStruct(h0r.shape, F32),
                   jax.ShapeDtypeStruct(h0i.shape, F32)],
        scratch_shapes=[pltpu.VMEM((d // LANES, rows, LANES), F32),
                        pltpu.VMEM((rows, n_state), F32), pltpu.VMEM((rows, n_state), F32),
                        pltpu.VMEM(h0r.shape, F32), pltpu.VMEM(h0i.shape, F32)],
        compiler_params=_cparams(("arbitrary",)),
        name="odd_mixer_fused",
    )(x3, ow["g_pre"], ow["w_in"], h0r, h0i, ow["abr"], ow["abi"], ow["wre"], ow["wim"], ow["cre"], ow["cim"],
      ow["d"], ow["w_gate"], ow["w_out"], ow["g_post"])


def _block_diag_tiles(x, rows_first):
    nt, ng, a, b = x.shape
    eye = jnp.eye(ng, dtype=jnp.bool_)
    out = jnp.where(eye[None, :, None, :, None], x[:, :, :, None, :], jnp.zeros((), x.dtype))
    del rows_first
    return out.reshape(nt, ng * a, ng * b)


def _tile(n):
    for cand in (512, 256, 128, 64, 32, 16, 8):
        if n % cand == 0:
            return cand
    return n


def _even_layer_prompt(hp, layer, ew, k_stack, v_stack):
    b, s, d = hp.shape
    tm = _tile(s)
    q, ktb, vb, conv, new_buf, k_stack, v_stack = _even_front(hp, ew, k_stack, v_stack, tm)
    lam_init = _lambda_init(layer)
    attn = _flash_attention(q, ktb, vb, ew["lq"], ew["subln_g"], lam_init, tm)
    row = lambda b_, i: (b_, i, 0)
    specs = [pl.BlockSpec((None, tm, ATT_W), row), pl.BlockSpec((None, tm, CONV_CH), row)]
    hp = _outproj([attn, conv], specs, ew["w_out"], ew["g_post"], hp, tm)
    return hp, k_stack, v_stack, new_buf


def _even_layer_sample(hs, layer, idx, ew, kcache, vcache, page_table, state_conv):
    nb, n_new, d = hs.shape
    m = nb * n_new
    q, kt, _, v, _, ug = _even_inproj(hs.reshape(1, m, d), ew["g_pre"], ew["w_in"], ew["wkt"], _tile(m))
    lam_init = _lambda_init(layer)
    q5 = q.reshape(nb, n_new, N_HEADS, 2, QK_DIM).transpose(0, 2, 3, 1, 4)
    q5 = jnp.pad(q5, ((0, 0), (0, 0), (0, 0), (0, SUBLANES - n_new), (0, 0)))
    eye = jnp.eye(2 * N_HEADS, dtype=jnp.bool_).reshape(N_HEADS, 2, 1, N_HEADS, 2, 1)
    qbd = jnp.where(eye[None], q5[:, :, :, :, None, None, :], jnp.zeros((), BF16)).reshape(nb, Q_ROWS, ATT_W)
    ktn = kt.reshape(ATT_W, nb, n_new).transpose(1, 0, 2)
    ktn = jnp.pad(ktn, ((0, 0), (0, 0), (0, LANES - n_new))).astype(BF16)
    vn = v.reshape(nb, n_new, N_HEADS, HEAD_W).transpose(0, 2, 1, 3)
    vn = jnp.pad(vn, ((0, 0), (0, 0), (0, LANES - n_new), (0, 0))).astype(BF16)
    attn = _paged_attention(page_table, ew["lq"], ew["subln_g"], qbd, ktn, vn, kcache, vcache, idx, n_new, lam_init)
    attn = attn[:, :n_new, :]
    conv, new_buf = _conv_branch(ug.reshape(nb, n_new, 2 * CONV_CH), state_conv, ew["conv_w"], ew["conv_b"],
                                 ew["ln_g"], ew["ln_b"], n_new)
    tm = _tile(m)
    row = lambda b_, i: (b_, i, 0)
    specs = [pl.BlockSpec((None, tm, ATT_W), row), pl.BlockSpec((None, tm, CONV_CH), row)]
    out = _outproj([attn.reshape(1, m, ATT_W), conv.reshape(1, m, CONV_CH)], specs, ew["w_out"], ew["g_post"],
                   hs.reshape(1, m, d), tm)
    k_new = kt.reshape(N_HEADS, 2, QK_DIM, nb, n_new).transpose(3, 4, 0, 1, 2)
    v_new = v.reshape(nb, n_new, N_HEADS, HEAD_W)
    return out.reshape(nb, n_new, d), k_new, v_new, new_buf


def _odd_layer_prompt(hp, ow):
    b, s, d = hp.shape
    n_state = ow["abr"].shape[1]
    zeros = jnp.zeros((SUBLANES, n_state), F32)
    hp, hr, hi = _odd_fused(hp, zeros, zeros, ow, min(s, ODD_TIME_CHUNK))
    return hp, hr[:b], hi[:b]


def _odd_layer_sample(hs, ow, h0r, h0i):
    nb, n_new, d = hs.shape
    m = nb * n_new
    u_tm = _odd_inproj(hs.transpose(1, 0, 2).reshape(1, m, d), ow["g_pre"], ow["w_in"], _tile(m))
    z, hr, hi = _s5_core(u_tm, h0r, h0i, ow["abr"], ow["abi"], ow["wre"], ow["wim"], ow["cre"], ow["cim"],
                         ow["d"], ow["w_gate"], nb, m)
    z = z.reshape(n_new, nb, d).transpose(1, 0, 2).reshape(1, m, d)
    tm = _tile(m)
    specs = [pl.BlockSpec((None, tm, d), lambda b_, i: (b_, i, 0))]
    out = _outproj([z], specs, ow["w_out"], ow["g_post"], hs.reshape(1, m, d), tm)
    return out.reshape(nb, n_new, d), hr, hi


def kernel(x_prompt, x_sample, cache_k, cache_v, page_table, state_conv, state_ssm_re, state_ssm_im, g_mix_pre, g_mix_post, g_ffn_pre, g_ffn_post, w_in_even, lambda_qk, subln_g, conv_w, conv_b, conv_ln_g, conv_ln_b, w_out_even, w_in_odd, ssm_a_re, ssm_a_im, ssm_b_re, ssm_b_im, ssm_c_re, ssm_c_im, ssm_d, ssm_log_dt, w_gate_odd, w_out_odd, w_ffn_up, w_ffn_down):
    depth, d = g_mix_pre.shape
    hp, hs = x_prompt, x_sample
    nb, n_new, _ = x_sample.shape
    n_layers_even, n_phys, page = cache_k.shape[:3]
    n_groups, n_p = ssm_a_re.shape[1:]
    n_tiles = n_groups // GROUPS_PER_TILE
    n_state = n_groups * n_p

    kcache = cache_k.transpose(0, 1, 3, 4, 5, 2).reshape(n_layers_even, n_phys, ATT_W, page)
    vcache = cache_v.reshape(n_layers_even, n_phys, page * N_HEADS, HEAD_W)

    row = lambda x: x.reshape(1, -1)
    wu_all = w_ffn_up.astype(BF16)
    wd_all = w_ffn_down.astype(BF16)
    outs = {k: [] for k in ("ks", "vs", "cp", "cs", "srp", "sip", "srs", "sis")}
    k_stack = v_stack = None
    for layer in range(depth):
        i = layer // 2
        if layer % 2 == 0:
            w_in = w_in_even[i].astype(BF16)
            ew = dict(g_pre=row(g_mix_pre[layer]), g_post=row(g_mix_post[layer]), w_in=w_in,
                      wkt=w_in[:, ATT_W:2 * ATT_W].T, lq=lambda_qk[i], subln_g=row(subln_g[i]),
                      conv_w=conv_w[i], conv_b=row(conv_b[i]), ln_g=row(conv_ln_g[i]), ln_b=row(conv_ln_b[i]),
                      w_out=w_out_even[i].astype(BF16))
            hp, k_stack, v_stack, buf = _even_layer_prompt(hp, layer, ew, k_stack, v_stack)
            outs["cp"].append(buf)
            hs, k_new, v_new, buf = _even_layer_sample(hs, layer, i, ew, kcache, vcache, page_table, state_conv[i])
            outs["ks"].append(k_new); outs["vs"].append(v_new); outs["cs"].append(buf)
        else:
            lr = ssm_a_re[i].reshape(n_tiles, STATE_TILE)
            li = ssm_a_im[i].reshape(n_tiles, STATE_TILE)
            ldt = jnp.repeat(ssm_log_dt[i], n_p).reshape(n_tiles, STATE_TILE)
            b4 = lambda x: x.reshape(n_tiles, GROUPS_PER_TILE, n_p, GROUP_CH).transpose(0, 1, 3, 2)
            abr, abi, wre, wim = _s5_prep(lr, li, ldt, _block_diag_tiles(b4(ssm_b_re[i]), True),
                                          _block_diag_tiles(b4(ssm_b_im[i]), True))
            c4 = lambda x: x.reshape(n_tiles, GROUPS_PER_TILE, GROUP_CH, n_p).transpose(0, 1, 3, 2)
            ow = dict(g_pre=row(g_mix_pre[layer]), g_post=row(g_mix_post[layer]), w_in=w_in_odd[i].astype(BF16),
                      abr=abr.reshape(1, n_state), abi=abi.reshape(1, n_state), wre=wre, wim=wim,
                      cre=_block_diag_tiles(c4(ssm_c_re[i]), False).astype(BF16),
                      cim=_block_diag_tiles(c4(ssm_c_im[i]), False).astype(BF16),
                      d=row(ssm_d[i]), w_gate=w_gate_odd[i].astype(BF16), w_out=w_out_odd[i].astype(BF16))
            hp, sr, si = _odd_layer_prompt(hp, ow)
            outs["srp"].append(sr.reshape(-1, n_groups, n_p)); outs["sip"].append(si.reshape(-1, n_groups, n_p))
            hs, sr, si = _odd_layer_sample(hs, ow, state_ssm_re[i].reshape(nb, n_state),
                                           state_ssm_im[i].reshape(nb, n_state))
            outs["srs"].append(sr.reshape(nb, n_groups, n_p)); outs["sis"].append(si.reshape(nb, n_groups, n_p))
        gpre, gpost = row(g_ffn_pre[layer]), row(g_ffn_post[layer])
        mp = hp.shape[0] * hp.shape[1]
        hp = _mlp(hp.reshape(mp, d), gpre, wu_all, wd_all, gpost, layer, MLP_TM if mp % MLP_TM == 0 else _tile(mp),
                  MLP_TK).reshape(hp.shape)
        ms = nb * n_new
        hs = _mlp(hs.reshape(ms, d), gpre, wu_all, wd_all, gpost, layer, _tile(ms), MLP_TK).reshape(hs.shape)
    st = jnp.stack
    bp, sp = x_prompt.shape[:2]
    new_k_prompt = k_stack.reshape(n_layers_even, bp, N_HEADS, 2, QK_DIM, sp).transpose(0, 1, 5, 2, 3, 4)
    new_v_prompt = v_stack.reshape(n_layers_even, bp, sp, N_HEADS, HEAD_W)
    return (hp, hs, new_k_prompt, new_v_prompt, st(outs["ks"]), st(outs["vs"]), st(outs["cp"]), st(outs["cs"]),
            st(outs["srp"]), st(outs["sip"]), st(outs["srs"]), st(outs["sis"]))
```

```python
import functools
import math

import jax
import jax.numpy as jnp
import numpy as np
from jax import lax
from jax.experimental import pallas as pl
from jax.experimental.pallas import tpu as pltpu

F32 = jnp.float32
BF16 = jnp.bfloat16

RMS_EPS = 1e-6
SUBLN_EPS = 1e-5
LN_EPS = 1e-5
LOG2_E = math.log2(math.e)

LANES = 128
SUBLANES = 8
N_HEADS = 4
QK_DIM = 64
HEAD_W = 2 * QK_DIM
ATT_W = N_HEADS * HEAD_W
CONV_CH = 512
CONV_TAPS = 31
CONV_HIST = CONV_TAPS - 1
GROUP_CH = 16
STATE_P = 64
GROUPS_PER_TILE = LANES // GROUP_CH
STATE_TILE = GROUPS_PER_TILE * STATE_P
CONV_CHUNK = 32
CONV_PAD = 32
VMEM_LIMIT = 52 * 1024 * 1024


def _cparams(semantics):
    return pltpu.CompilerParams(dimension_semantics=semantics, vmem_limit_bytes=VMEM_LIMIT)


def _rms(x, g, eps):
    ms = jnp.mean(x * x, axis=-1, keepdims=True)
    return (x * lax.rsqrt(ms + eps)) * g


def _sigmoid(x):
    return 1.0 / (1.0 + jnp.exp(-x))


def _lambda_full(lq, lam_init):
    s1 = jnp.sum(lq[0:1, :] * lq[1:2, :], axis=-1, keepdims=True)
    s2 = jnp.sum(lq[2:3, :] * lq[3:4, :], axis=-1, keepdims=True)
    return jnp.exp(s1) - jnp.exp(s2) + lam_init


def _lambda_init(layer):
    return 0.8 - 0.6 * math.exp(-0.3 * layer)


def _even_inproj_kernel(x_ref, g_ref, w_ref, wkt_ref, q_ref, kt_ref, ktb_ref, v_ref, vb_ref, ug_ref):
    xn = _rms(x_ref[...], g_ref[...], RMS_EPS).astype(BF16)

    def proj(c0, width):
        return jnp.dot(xn, w_ref[:, c0:c0 + width], preferred_element_type=F32)

    q_ref[...] = (proj(0, ATT_W) * (QK_DIM ** -0.5 * LOG2_E)).astype(BF16)
    kt = lax.dot_general(wkt_ref[...], xn, (((1,), (1,)), ((), ())), preferred_element_type=F32)
    kt_ref[...] = kt
    ktb_ref[...] = kt.astype(BF16)
    v = proj(2 * ATT_W, ATT_W)
    v_ref[...] = v
    vb_ref[...] = v.astype(BF16)
    ug_ref[...] = proj(3 * ATT_W, 2 * CONV_CH)


def _even_inproj(x3, g, w_bf, wkt_bf, tm):
    bx, sx, d = x3.shape
    n_in = w_bf.shape[1]
    row = lambda b, i: (b, i, 0)
    col = lambda b, i: (b, 0, i)
    const = lambda b, i: (0, 0)
    return pl.pallas_call(
        _even_inproj_kernel,
        grid=(bx, sx // tm),
        in_specs=[pl.BlockSpec((None, tm, d), row), pl.BlockSpec((1, d), const),
                  pl.BlockSpec((d, n_in), const), pl.BlockSpec((ATT_W, d), const)],
        out_specs=[pl.BlockSpec((None, tm, ATT_W), row), pl.BlockSpec((None, ATT_W, tm), col),
                   pl.BlockSpec((None, ATT_W, tm), col), pl.BlockSpec((None, tm, ATT_W), row),
                   pl.BlockSpec((None, tm, ATT_W), row), pl.BlockSpec((None, tm, 2 * CONV_CH), row)],
        out_shape=[jax.ShapeDtypeStruct((bx, sx, ATT_W), BF16), jax.ShapeDtypeStruct((bx, ATT_W, sx), F32),
                   jax.ShapeDtypeStruct((bx, ATT_W, sx), BF16), jax.ShapeDtypeStruct((bx, sx, ATT_W), F32),
                   jax.ShapeDtypeStruct((bx, sx, ATT_W), BF16), jax.ShapeDtypeStruct((bx, sx, 2 * CONV_CH), F32)],
        compiler_params=_cparams(("parallel", "parallel")),
        name="even_inproj",
    )(x3, g, w_bf, wkt_bf)


def _conv_taps(full_ref, shift_ref, w_ref, b_ref, lg_ref, lb_ref, c_ref, tt):
    h0 = CONV_PAD - CONV_HIST
    n_shift = shift_ref.shape[1]
    for phase in range(1, SUBLANES):
        shift_ref[phase - 1] = full_ref[phase:phase + n_shift, :]
    chunk = min(tt, CONV_CHUNK)
    for r0 in range(0, tt, chunk):
        acc = jnp.zeros((chunk, CONV_CH), F32)
        for tap in range(CONV_TAPS):
            tile_row, phase = divmod(h0 + tap, SUBLANES)
            start = tile_row * SUBLANES + r0
            if phase == 0:
                window = full_ref[start:start + chunk, :]
            else:
                window = shift_ref[phase - 1, start:start + chunk, :]
            acc = acc + w_ref[tap:tap + 1, :] * window
        c = acc + b_ref[...]
        mu = jnp.mean(c, axis=-1, keepdims=True)
        xc = c - mu
        y = xc * lax.rsqrt(jnp.mean(xc * xc, axis=-1, keepdims=True) + LN_EPS)
        y = y * lg_ref[...] + lb_ref[...]
        c_ref[r0:r0 + chunk, :] = (y * _sigmoid(y)).astype(BF16)


def _even_front_kernel(*refs, n_prev):
    x_ref, g_ref, w_ref, wkt_ref, cw_ref, cb_ref, lg_ref, lb_ref = refs[:8]
    prev_refs = refs[8:8 + 2 * (n_prev > 0)]
    q_ref, ktb_ref, vb_ref, c_ref, nb_ref, kt_ref, v_ref, full_ref, shift_ref = refs[8 + len(prev_refs):]
    t = pl.program_id(1)
    tt = x_ref.shape[0]
    h0 = CONV_PAD - CONV_HIST
    xn = _rms(x_ref[...], g_ref[...], RMS_EPS).astype(BF16)

    def proj(c0, width):
        return jnp.dot(xn, w_ref[:, c0:c0 + width], preferred_element_type=F32)

    q_ref[...] = (proj(0, ATT_W) * (QK_DIM ** -0.5 * LOG2_E)).astype(BF16)
    kt = lax.dot_general(wkt_ref[...], xn, (((1,), (1,)), ((), ())), preferred_element_type=F32)
    for layer in range(n_prev):
        kt_ref[layer] = prev_refs[0][layer]
        v_ref[layer] = prev_refs[1][layer]
    kt_ref[n_prev] = kt
    ktb_ref[...] = kt.astype(BF16)
    v = proj(2 * ATT_W, ATT_W)
    vb_ref[...] = v.astype(BF16)
    for h in range(N_HEADS):
        v_ref[n_prev, pl.ds(h, tt, stride=N_HEADS), :] = v[:, h * HEAD_W:(h + 1) * HEAD_W]

    @pl.when(t == 0)
    def _first():
        full_ref[h0:CONV_PAD, :] = jnp.zeros((CONV_HIST, CONV_CH), F32)

    @pl.when(t > 0)
    def _carry():
        full_ref[h0:CONV_PAD, :] = full_ref[tt + h0:tt + CONV_PAD, :]

    full_ref[CONV_PAD:CONV_PAD + tt, :] = proj(3 * ATT_W, CONV_CH) * _sigmoid(proj(3 * ATT_W + CONV_CH, CONV_CH))
    _conv_taps(full_ref, shift_ref, cw_ref, cb_ref, lg_ref, lb_ref, c_ref, tt)

    @pl.when(t == pl.num_programs(1) - 1)
    def _tail():
        nb_ref[...] = full_ref[tt + h0:tt + CONV_PAD, :]


def _even_front(x3, ew, k_prev, v_prev, tt):
    b, s, d = x3.shape
    w_bf = ew["w_in"]
    n_prev = 0 if k_prev is None else k_prev.shape[0]
    row = lambda b_, t: (b_, t, 0)
    col = lambda b_, t: (b_, 0, t)
    per_b = lambda b_, t: (b_, 0, 0)
    const = lambda b_, t: (0, 0)
    k_block = lambda n: pl.BlockSpec((n, None, ATT_W, tt), lambda b_, t: (0, b_, 0, t))
    v_block = lambda n: pl.BlockSpec((n, None, tt * N_HEADS, HEAD_W), lambda b_, t: (0, b_, t, 0))
    prev = [] if n_prev == 0 else [k_prev, v_prev]
    prev_specs = [] if n_prev == 0 else [k_block(n_prev), v_block(n_prev)]
    return pl.pallas_call(
        functools.partial(_even_front_kernel, n_prev=n_prev),
        grid=(b, s // tt),
        in_specs=[pl.BlockSpec((None, tt, d), row), pl.BlockSpec((1, d), const),
                  pl.BlockSpec(w_bf.shape, const), pl.BlockSpec((ATT_W, d), const),
                  pl.BlockSpec((CONV_TAPS, CONV_CH), const), pl.BlockSpec((1, CONV_CH), const),
                  pl.BlockSpec((1, CONV_CH), const), pl.BlockSpec((1, CONV_CH), const)] + prev_specs,
        out_specs=[pl.BlockSpec((None, tt, ATT_W), row), pl.BlockSpec((None, ATT_W, tt), col),
                   pl.BlockSpec((None, tt, ATT_W), row), pl.BlockSpec((None, tt, CONV_CH), row),
                   pl.BlockSpec((None, CONV_HIST, CONV_CH), per_b), k_block(n_prev + 1), v_block(n_prev + 1)],
        out_shape=[jax.ShapeDtypeStruct((b, s, ATT_W), BF16), jax.ShapeDtypeStruct((b, ATT_W, s), BF16),
                   jax.ShapeDtypeStruct((b, s, ATT_W), BF16), jax.ShapeDtypeStruct((b, s, CONV_CH), BF16),
                   jax.ShapeDtypeStruct((b, CONV_HIST, CONV_CH), F32),
                   jax.ShapeDtypeStruct((n_prev + 1, b, ATT_W, s), F32),
                   jax.ShapeDtypeStruct((n_prev + 1, b, s * N_HEADS, HEAD_W), F32)],
        scratch_shapes=[pltpu.VMEM((CONV_PAD + tt, CONV_CH), F32),
                        pltpu.VMEM((SUBLANES - 1, CONV_PAD - SUBLANES + tt, CONV_CH), F32)],
        compiler_params=_cparams(("parallel", "arbitrary")),
        name="even_front",
    )(x3, ew["g_pre"], w_bf, ew["wkt"], ew["conv_w"], ew["conv_b"], ew["ln_g"], ew["ln_b"], *prev)


PAGES_PER_STEP = 16
Q_ROWS = N_HEADS * 2 * SUBLANES


def _flash_step(i, j, lq_ref, g_ref, q_ref, kt_ref, v_ref, o_ref, qm_ref, m_ref, l_ref, acc_ref, lam_init):
    heads = [slice(h * HEAD_W, (h + 1) * HEAD_W) for h in range(N_HEADS)]

    @pl.when(j == 0)
    def _init():
        q = q_ref[...]
        lane = lax.broadcasted_iota(jnp.int32, (q.shape[0], HEAD_W), 1)
        first = jnp.where(lane < QK_DIM, 1.0, 0.0).astype(BF16)
        for h, hs in enumerate(heads):
            qm_ref[2 * h] = q[:, hs] * first
            qm_ref[2 * h + 1] = q[:, hs] * (1.0 - first)
        m_ref[...] = jnp.full(m_ref.shape, -jnp.inf, F32)
        l_ref[...] = jnp.zeros(l_ref.shape, F32)
        acc_ref[...] = jnp.zeros(acc_ref.shape, F32)

    def step(masked):
        n_blk = kt_ref.shape[1] // LANES
        for h, hs in enumerate(heads):
            kt = kt_ref[hs, :]
            v = v_ref[:, hs]
            for c in range(2):
                hc = 2 * h + c
                s = jnp.dot(qm_ref[hc], kt, preferred_element_type=F32)
                if masked:
                    row = lax.broadcasted_iota(jnp.int32, s.shape, 0)
                    col = lax.broadcasted_iota(jnp.int32, s.shape, 1)
                    s = jnp.where(col <= row, s, -jnp.inf)
                blocks = [s[:, b * LANES:(b + 1) * LANES] for b in range(n_blk)]
                mx = functools.reduce(jnp.maximum, blocks)
                m_prev = m_ref[hc]
                m_new = jnp.maximum(m_prev, jnp.max(mx, axis=-1, keepdims=True))
                alpha = jnp.exp2(m_prev - m_new)
                ps = [jnp.exp2(blk - m_new) for blk in blocks]
                l_ref[hc] = alpha * l_ref[hc] + functools.reduce(jnp.add, ps)
                p = jnp.concatenate([x.astype(BF16) for x in ps], axis=-1)
                acc_ref[hc] = alpha * acc_ref[hc] + jnp.dot(p, v, preferred_element_type=F32)
                m_ref[hc] = m_new

    @pl.when(j < i)
    def _full():
        step(False)

    @pl.when(j == i)
    def _diag():
        step(True)
        lam = _lambda_full(lq_ref[...], lam_init)
        for h, hs in enumerate(heads):
            l0 = jnp.sum(l_ref[2 * h], axis=-1, keepdims=True)
            l1 = jnp.sum(l_ref[2 * h + 1], axis=-1, keepdims=True)
            o = acc_ref[2 * h] / l0 - lam * (acc_ref[2 * h + 1] / l1)
            o_ref[:, hs] = (_rms(o, g_ref[...], SUBLN_EPS) * (1.0 - lam_init)).astype(BF16)


def _paged_step(step, lq_ref, g_ref, qbd_ref, ktn_ref, vn_ref, k_refs, v_refs, o_ref, m_ref, l_ref, acc_ref,
                n_steps, n_new, lam_init):
    rows_per_head = 2 * SUBLANES

    @pl.when(step == 0)
    def _init():
        m_ref[...] = jnp.full(m_ref.shape, -jnp.inf, F32)
        l_ref[...] = jnp.zeros(l_ref.shape, F32)
        acc_ref[...] = jnp.zeros(acc_ref.shape, F32)

    def update(scores, values):
        m_prev = m_ref[...]
        m_new = m_prev
        for s in scores:
            m_new = jnp.maximum(m_new, jnp.max(s, axis=-1, keepdims=True))
        alpha = jnp.exp2(m_prev - m_new)
        l_new = alpha * l_ref[...]
        acc = alpha * acc_ref[...]
        parts = [acc[h * rows_per_head:(h + 1) * rows_per_head] for h in range(N_HEADS)]
        for s, vh in zip(scores, values):
            p = jnp.exp2(s - m_new)
            l_new = l_new + jnp.sum(p, axis=-1, keepdims=True)
            pb = p.astype(BF16)
            for h in range(N_HEADS):
                parts[h] = parts[h] + jnp.dot(pb[h * rows_per_head:(h + 1) * rows_per_head], vh[h],
                                              preferred_element_type=F32)
        m_ref[...] = m_new
        l_ref[...] = l_new
        acc_ref[...] = jnp.concatenate(parts, axis=0)

    qbd = qbd_ref[...]
    scores, values = [], []
    for kr, vr in zip(k_refs, v_refs):
        scores.append(jnp.dot(qbd, kr[...].astype(BF16), preferred_element_type=F32))
        values.append([vr[pl.ds(h, LANES, stride=N_HEADS), :].astype(BF16) for h in range(N_HEADS)])
    update(scores, values)

    @pl.when(step == n_steps - 1)
    def _new_tokens():
        s = jnp.dot(qbd, ktn_ref[...], preferred_element_type=F32)
        tok = lax.broadcasted_iota(jnp.int32, s.shape, 0) % SUBLANES
        col = lax.broadcasted_iota(jnp.int32, s.shape, 1)
        s = jnp.where((col <= tok) & (col < n_new), s, -jnp.inf)
        update([s], [[vn_ref[h] for h in range(N_HEADS)]])
        lam = _lambda_full(lq_ref[...], lam_init)
        acc = acc_ref[...]
        l = l_ref[...]
        g = g_ref[...]
        for h in range(N_HEADS):
            r0 = h * rows_per_head
            o1 = acc[r0:r0 + SUBLANES] / l[r0:r0 + SUBLANES]
            o2 = acc[r0 + SUBLANES:r0 + 2 * SUBLANES] / l[r0 + SUBLANES:r0 + 2 * SUBLANES]
            o = o1 - lam * o2
            o_ref[:, h * HEAD_W:(h + 1) * HEAD_W] = _rms(o, g, SUBLN_EPS) * (1.0 - lam_init)


def _attn_kernel(fi_ref, fj_ref, pt_ref, lq_ref, g_ref, q_ref, kt_ref, v_ref, qbd_ref, ktn_ref, vn_ref, *rest,
                 n_pairs, n_paged, n_steps, n_new, lam_init):
    del pt_ref
    k_refs = rest[:PAGES_PER_STEP]
    v_refs = rest[PAGES_PER_STEP:2 * PAGES_PER_STEP]
    of_ref, op_ref, qm_ref, fm_ref, fl_ref, facc_ref, pm_ref, pl_ref, pacc_ref = rest[2 * PAGES_PER_STEP:]
    t = pl.program_id(1)
    gi = pl.program_id(0) * pl.num_programs(1) + t

    @pl.when(t < n_pairs)
    def _flash():
        _flash_step(fi_ref[t], fj_ref[t], lq_ref, g_ref, q_ref, kt_ref, v_ref, of_ref, qm_ref, fm_ref, fl_ref,
                    facc_ref, lam_init)

    @pl.when(gi < n_paged)
    def _paged():
        _paged_step(lax.rem(gi, n_steps), lq_ref, g_ref, qbd_ref, ktn_ref, vn_ref, k_refs, v_refs, op_ref,
                    pm_ref, pl_ref, pacc_ref, n_steps, n_new, lam_init)


def _attention(q, kt, v, page_table, qbd, ktn, vn, kcache, vcache, lq, subln_g, layer, n_new, lam_init, tq):
    b, s, _ = q.shape
    nq = s // tq
    nb, n_pages = page_table.shape
    n_steps = n_pages // PAGES_PER_STEP
    page_rows = kcache.shape[2]
    pairs = [(i, j) for i in range(nq) for j in range(i + 1)]
    n_pairs = len(pairs)
    n_paged = nb * n_steps
    n_t = max(n_pairs, -(-n_paged // b))
    pairs = pairs + [pairs[-1]] * (n_t - n_pairs)
    fi = jnp.asarray(np.array([p[0] for p in pairs], np.int32))
    fj = jnp.asarray(np.array([p[1] for p in pairs], np.int32))

    def paged_pos(bb, t):
        step = jnp.minimum(bb * n_t + t, n_paged - 1)
        return step // n_steps, lax.rem(step, n_steps)

    def page_map(jj):
        def index_map(bb, t, fi_, fj_, pt):
            entry, st = paged_pos(bb, t)
            return (layer, pt[entry * n_pages + st * PAGES_PER_STEP + jj], 0, 0)
        return index_map

    const = lambda bb, t, fi_, fj_, pt: (0, 0)
    per_b3 = lambda bb, t, fi_, fj_, pt: (paged_pos(bb, t)[0], 0, 0)
    per_b4 = lambda bb, t, fi_, fj_, pt: (paged_pos(bb, t)[0], 0, 0, 0)
    q_map = lambda bb, t, fi_, fj_, pt: (bb, fi_[t], 0)
    kt_map = lambda bb, t, fi_, fj_, pt: (bb, 0, fj_[t])
    v_map = lambda bb, t, fi_, fj_, pt: (bb, fj_[t], 0)
    page_specs = [pl.BlockSpec((None, None, page_rows, LANES), page_map(jj)) for jj in range(PAGES_PER_STEP)]
    grid_spec = pltpu.PrefetchScalarGridSpec(
        num_scalar_prefetch=3,
        grid=(b, n_t),
        in_specs=[pl.BlockSpec((4, QK_DIM), const), pl.BlockSpec((1, HEAD_W), const),
                  pl.BlockSpec((None, tq, ATT_W), q_map), pl.BlockSpec((None, ATT_W, tq), kt_map),
                  pl.BlockSpec((None, tq, ATT_W), v_map),
                  pl.BlockSpec((None, Q_ROWS, ATT_W), per_b3), pl.BlockSpec((None, ATT_W, LANES), per_b3),
                  pl.BlockSpec((None, N_HEADS, LANES, HEAD_W), per_b4)] + page_specs + page_specs,
        out_specs=[pl.BlockSpec((None, tq, ATT_W), q_map), pl.BlockSpec((None, SUBLANES, ATT_W), per_b3)],
        scratch_shapes=[pltpu.VMEM((2 * N_HEADS, tq, HEAD_W), BF16), pltpu.VMEM((2 * N_HEADS, tq, LANES), F32),
                        pltpu.VMEM((2 * N_HEADS, tq, LANES), F32), pltpu.VMEM((2 * N_HEADS, tq, HEAD_W), F32),
                        pltpu.VMEM((Q_ROWS, 1), F32), pltpu.VMEM((Q_ROWS, 1), F32),
                        pltpu.VMEM((Q_ROWS, HEAD_W), F32)],
    )
    return pl.pallas_call(
        functools.partial(_attn_kernel, n_pairs=n_pairs, n_paged=n_paged, n_steps=n_steps, n_new=n_new,
                          lam_init=lam_init),
        grid_spec=grid_spec,
        out_shape=[jax.ShapeDtypeStruct((b, s, ATT_W), BF16), jax.ShapeDtypeStruct((nb, SUBLANES, ATT_W), F32)],
        compiler_params=_cparams(("arbitrary", "arbitrary")),
        name="diff_attn",
    )(fi, fj, page_table.reshape(-1), lq, subln_g, q, kt, v, qbd, ktn, vn,
      *([kcache] * PAGES_PER_STEP), *([vcache] * PAGES_PER_STEP))


def _conv_kernel(ug_ref, hist_ref, w_ref, b_ref, lg_ref, lb_ref, c_ref, nb_ref, full_ref, shift_ref):
    t = pl.program_id(1)
    tt = ug_ref.shape[0]
    h0 = CONV_PAD - CONV_HIST

    @pl.when(t == 0)
    def _first():
        full_ref[h0:CONV_PAD, :] = hist_ref[...]

    @pl.when(t > 0)
    def _carry():
        full_ref[h0:CONV_PAD, :] = full_ref[tt + h0:tt + CONV_PAD, :]

    ug = ug_ref[...]
    full_ref[CONV_PAD:CONV_PAD + tt, :] = ug[:, :CONV_CH] * _sigmoid(ug[:, CONV_CH:])
    _conv_taps(full_ref, shift_ref, w_ref, b_ref, lg_ref, lb_ref, c_ref, tt)

    @pl.when(t == pl.num_programs(1) - 1)
    def _tail():
        nb_ref[...] = full_ref[tt + h0:tt + CONV_PAD, :]


def _conv_branch(ug, hist, w, b, lg, lb, tt):
    bx, sx, _ = ug.shape
    row = lambda b_, t: (b_, t, 0)
    per_b = lambda b_, t: (b_, 0, 0)
    const = lambda b_, t: (0, 0)
    return pl.pallas_call(
        _conv_kernel,
        grid=(bx, sx // tt),
        in_specs=[pl.BlockSpec((None, tt, 2 * CONV_CH), row), pl.BlockSpec((None, CONV_HIST, CONV_CH), per_b),
                  pl.BlockSpec((CONV_TAPS, CONV_CH), const), pl.BlockSpec((1, CONV_CH), const),
                  pl.BlockSpec((1, CONV_CH), const), pl.BlockSpec((1, CONV_CH), const)],
        out_specs=[pl.BlockSpec((None, tt, CONV_CH), row), pl.BlockSpec((None, CONV_HIST, CONV_CH), per_b)],
        out_shape=[jax.ShapeDtypeStruct((bx, sx, CONV_CH), BF16), jax.ShapeDtypeStruct((bx, CONV_HIST, CONV_CH), F32)],
        scratch_shapes=[pltpu.VMEM((CONV_PAD + tt, CONV_CH), F32),
                        pltpu.VMEM((SUBLANES - 1, CONV_PAD - SUBLANES + tt, CONV_CH), F32)],
        compiler_params=_cparams(("parallel", "arbitrary")),
        name="conv_branch",
    )(ug, hist, w, b, lg, lb)


def _outproj_kernel(*refs, widths):
    a_refs = refs[:len(widths)]
    w_ref, g_ref, h_ref, o_ref = refs[len(widths):]
    acc = None
    r0 = 0
    for a_ref, width in zip(a_refs, widths):
        part = jnp.dot(a_ref[...].astype(BF16), w_ref[r0:r0 + width, :], preferred_element_type=F32)
        acc = part if acc is None else acc + part
        r0 += width
    o_ref[...] = h_ref[...] + _rms(acc, g_ref[...], RMS_EPS)


def _outproj(a_list, a_specs, w_bf, g, h3, tm):
    bx, sx, d = h3.shape
    widths = tuple(spec.block_shape[-1] for spec in a_specs)
    row = lambda b, i: (b, i, 0)
    const = lambda b, i: (0, 0)
    return pl.pallas_call(
        functools.partial(_outproj_kernel, widths=widths),
        grid=(bx, sx // tm),
        in_specs=list(a_specs) + [pl.BlockSpec(w_bf.shape, const), pl.BlockSpec((1, d), const),
                                  pl.BlockSpec((None, tm, d), row)],
        out_specs=pl.BlockSpec((None, tm, d), row),
        out_shape=jax.ShapeDtypeStruct((bx, sx, d), F32),
        compiler_params=_cparams(("parallel", "parallel")),
        name="mixer_outproj",
    )(*a_list, w_bf, g, h3)


MLP_TM = 1024
MLP_TK = 1024


def _mlp_kernel(x_ref, gpre_ref, wu_ref, wd_ref, gpost_ref, o_ref, xn_ref, acc_ref):
    k = pl.program_id(1)

    @pl.when(k == 0)
    def _init():
        xn_ref[...] = _rms(x_ref[...], gpre_ref[...], RMS_EPS).astype(BF16)
        acc_ref[...] = jnp.zeros(acc_ref.shape, F32)

    up = jnp.dot(xn_ref[...], wu_ref[...], preferred_element_type=F32)
    act = jnp.square(jnp.maximum(up, 0.0)).astype(BF16)
    acc_ref[...] += jnp.dot(act, wd_ref[...], preferred_element_type=F32)

    @pl.when(k == pl.num_programs(1) - 1)
    def _done():
        o_ref[...] = x_ref[...] + _rms(acc_ref[...], gpost_ref[...], RMS_EPS)


def _mlp(x2, gpre, wu_bf, wd_bf, gpost, layer, tm, tk):
    m, d = x2.shape
    dff = wu_bf.shape[2]
    return pl.pallas_call(
        _mlp_kernel,
        grid=(m // tm, dff // tk),
        in_specs=[pl.BlockSpec((tm, d), lambda i, k: (i, 0)), pl.BlockSpec((1, d), lambda i, k: (0, 0)),
                  pl.BlockSpec((None, d, tk), lambda i, k: (layer, 0, k)),
                  pl.BlockSpec((None, tk, d), lambda i, k: (layer, k, 0)),
                  pl.BlockSpec((1, d), lambda i, k: (0, 0))],
        out_specs=pl.BlockSpec((tm, d), lambda i, k: (i, 0)),
        out_shape=jax.ShapeDtypeStruct((m, d), F32),
        scratch_shapes=[pltpu.VMEM((tm, d), BF16), pltpu.VMEM((tm, d), F32)],
        compiler_params=_cparams(("parallel", "arbitrary")),
        name="sq_relu_mlp",
    )(x2, gpre, wu_bf, wd_bf, gpost)


def _odd_inproj_kernel(x_ref, g_ref, w_ref, u_ref):
    xn = _rms(x_ref[...], g_ref[...], RMS_EPS).astype(BF16)
    u_ref[...] = jnp.dot(xn, w_ref[...], preferred_element_type=F32)


def _odd_inproj(x3, g, w_bf, tm):
    bx, sx, d = x3.shape
    c = w_bf.shape[1]
    out = pl.pallas_call(
        _odd_inproj_kernel,
        grid=(bx, sx // tm),
        in_specs=[pl.BlockSpec((None, tm, d), lambda b, i: (b, i, 0)), pl.BlockSpec((1, d), lambda b, i: (0, 0)),
                  pl.BlockSpec((d, c), lambda b, i: (0, 0))],
        out_specs=pl.BlockSpec((tm, c), lambda b, i: (i, b)),
        out_shape=jax.ShapeDtypeStruct((sx, bx * c), F32),
        compiler_params=_cparams(("parallel", "parallel")),
        name="odd_inproj",
    )(x3, g, w_bf)
    return out.reshape(sx * bx, c)


def _s5_prep_kernel(lr_ref, li_ref, ldt_ref, br_ref, bi_ref, abr_ref, abi_ref, wre_ref, wim_ref):
    lr = lr_ref[...]
    li = li_ref[...]
    dt = jnp.exp(ldt_ref[...])
    mag = jnp.exp(lr * dt)
    ang = li * dt
    abr = mag * jnp.cos(ang)
    abi = mag * jnp.sin(ang)
    den = lr * lr + li * li
    nr = abr - 1.0
    ni = abi
    fr = (nr * lr + ni * li) / den
    fi = (ni * lr - nr * li) / den
    abr_ref[...] = abr
    abi_ref[...] = abi
    for j in range(lr.shape[0]):
        frj = fr[j:j + 1, :]
        fij = fi[j:j + 1, :]
        br = br_ref[j]
        bi = bi_ref[j]
        wr = frj * br - fij * bi
        wi = frj * bi + fij * br
        wre_ref[j, :LANES, :] = wr.astype(BF16)
        wim_ref[j, :LANES, :] = wi.astype(BF16)
        arj = abr[j:j + 1, :]
        aij = abi[j:j + 1, :]
        wre_ref[j, LANES:, :] = (arj * wr - aij * wi).astype(BF16)
        wim_ref[j, LANES:, :] = (arj * wi + aij * wr).astype(BF16)


def _s5_prep(lr, li, ldt, br_bd, bi_bd):
    nt = lr.shape[0]
    vm = pl.BlockSpec(memory_space=pltpu.VMEM)
    return pl.pallas_call(
        _s5_prep_kernel,
        in_specs=[vm] * 5,
        out_specs=[vm] * 4,
        out_shape=[jax.ShapeDtypeStruct((nt, STATE_TILE), F32), jax.ShapeDtypeStruct((nt, STATE_TILE), F32),
                   jax.ShapeDtypeStruct((nt, 2 * LANES, STATE_TILE), BF16),
                   jax.ShapeDtypeStruct((nt, 2 * LANES, STATE_TILE), BF16)],
        name="s5_discretise",
    )(lr, li, ldt, br_bd, bi_bd)


SCAN_LANES = 512


def _scan_interleaved_group(bur_ref, bui_ref, cr_ref, ci_ref, abr_ref, abi_ref, g):
    rc = bur_ref.shape[0]
    upper = lax.broadcasted_iota(jnp.int32, (SUBLANES, SCAN_LANES), 0) >= 4
    sl = slice(g * SCAN_LANES, (g + 1) * SCAN_LANES)
    ar = jnp.broadcast_to(abr_ref[:, sl], (SUBLANES, SCAN_LANES))
    ai = jnp.broadcast_to(abi_ref[:, sl], (SUBLANES, SCAN_LANES))
    a2r = ar * ar - ai * ai
    a2i = ar * ai + ai * ar
    acr = jnp.where(upper, a2r, ar)
    aci = jnp.where(upper, a2i, ai)

    def body(r, carry):
        pr, pi = carry
        rows = pl.ds(pl.multiple_of(r * SUBLANES, SUBLANES), SUBLANES)
        x1r = bur_ref[rows, sl]
        x1i = bui_ref[rows, sl]
        x2r = x1r + (acr * pr - aci * pi)
        x2i = x1i + (acr * pi + aci * pr)
        bur_ref[rows, sl] = x2r
        bui_ref[rows, sl] = x2i
        npr = jnp.where(upper, x2r, pltpu.roll(x2r, 4, 0))
        npi = jnp.where(upper, x2i, pltpu.roll(x2i, 4, 0))
        return npr, npi

    pr, pi = lax.fori_loop(0, rc // SUBLANES, body, (cr_ref[:, sl], ci_ref[:, sl]), unroll=True)
    cr_ref[:, sl] = pr
    ci_ref[:, sl] = pi


def _scan_dense_group(bur_ref, bui_ref, cr_ref, ci_ref, abr_ref, abi_ref, g, bsz):
    rc = bur_ref.shape[0]
    sl = slice(g * SCAN_LANES, (g + 1) * SCAN_LANES)
    ar = abr_ref[:, sl]
    ai = abi_ref[:, sl]
    pr = cr_ref[:, sl]
    pi = ci_ref[:, sl]
    for t in range(rc // bsz):
        rows = slice(t * bsz, (t + 1) * bsz)
        nr = bur_ref[rows, sl] + (ar * pr - ai * pi)
        ni = bui_ref[rows, sl] + (ar * pi + ai * pr)
        bur_ref[rows, sl] = nr
        bui_ref[rows, sl] = ni
        pr, pi = nr, ni
    cr_ref[:, sl] = pr
    ci_ref[:, sl] = pi


def _s5_block(u, abr_ref, abi_ref, wre_ref, wim_ref, cre_ref, cim_ref, d_ref, wg_ref,
              bur_ref, bui_ref, cr_ref, ci_ref, bsz):
    n_tiles = wre_ref.shape[0]
    ub = u.astype(BF16)
    if bsz == 4:
        row = lax.broadcasted_iota(jnp.int32, u.shape, 0)
        ub_prev = jnp.where((row & 4) != 0, pltpu.roll(u, 4, 0), 0.0).astype(BF16)
    ys = []
    for j in range(n_tiles):
        cols = slice(j * LANES, (j + 1) * LANES)
        states = slice(j * STATE_TILE, (j + 1) * STATE_TILE)
        if bsz == 4:
            lhs = jnp.concatenate([ub[:, cols], ub_prev[:, cols]], axis=-1)
            bur_ref[:, states] = jnp.dot(lhs, wre_ref[j], preferred_element_type=F32)
            bui_ref[:, states] = jnp.dot(lhs, wim_ref[j], preferred_element_type=F32)
            _scan_interleaved_group(bur_ref, bui_ref, cr_ref, ci_ref, abr_ref, abi_ref, j)
        else:
            bur_ref[:, states] = jnp.dot(ub[:, cols], wre_ref[j, :LANES, :], preferred_element_type=F32)
            bui_ref[:, states] = jnp.dot(ub[:, cols], wim_ref[j, :LANES, :], preferred_element_type=F32)
            _scan_dense_group(bur_ref, bui_ref, cr_ref, ci_ref, abr_ref, abi_ref, j, bsz)
        hr = bur_ref[:, states].astype(BF16)
        hi = bui_ref[:, states].astype(BF16)
        ys.append(jnp.dot(hr, cre_ref[j], preferred_element_type=F32)
                  - jnp.dot(hi, cim_ref[j], preferred_element_type=F32))
    y = jnp.concatenate(ys, axis=-1) + d_ref[...] * u
    z = 0.5 * y * (1.0 + lax.erf(y * math.sqrt(0.5)))
    gate = jnp.dot(z.astype(BF16), wg_ref[...], preferred_element_type=F32)
    return z * _sigmoid(gate)


def _s5_kernel(u_ref, h0r_ref, h0i_ref, abr_ref, abi_ref, wre_ref, wim_ref, cre_ref, cim_ref, d_ref, wg_ref,
               z_ref, hr_ref, hi_ref, bur_ref, bui_ref, cr_ref, ci_ref, *, bsz):
    step = pl.program_id(0)

    @pl.when(step == 0)
    def _init():
        cr_ref[...] = h0r_ref[...]
        ci_ref[...] = h0i_ref[...]

    zg = _s5_block(u_ref[...], abr_ref, abi_ref, wre_ref, wim_ref, cre_ref, cim_ref, d_ref, wg_ref,
                   bur_ref, bui_ref, cr_ref, ci_ref, bsz)
    z_ref[...] = zg.astype(BF16)

    @pl.when(step == pl.num_programs(0) - 1)
    def _state():
        hr_ref[...] = cr_ref[...]
        hi_ref[...] = ci_ref[...]


def _s5_core(u_tm, h0r, h0i, abr, abi, wre, wim, cre, cim, d_skip, wg_bf, bsz, rc):
    rows, c = u_tm.shape
    n_state = abr.shape[1]
    carry_rows = h0r.shape[0]
    const2 = lambda s: (0, 0)
    const3 = lambda s: (0, 0, 0)
    return pl.pallas_call(
        functools.partial(_s5_kernel, bsz=bsz),
        grid=(rows // rc,),
        in_specs=[pl.BlockSpec((rc, c), lambda s: (s, 0)),
                  pl.BlockSpec((carry_rows, n_state), const2), pl.BlockSpec((carry_rows, n_state), const2),
                  pl.BlockSpec((1, n_state), const2), pl.BlockSpec((1, n_state), const2),
                  pl.BlockSpec(wre.shape, const3), pl.BlockSpec(wim.shape, const3),
                  pl.BlockSpec(cre.shape, const3), pl.BlockSpec(cim.shape, const3),
                  pl.BlockSpec((1, c), const2), pl.BlockSpec(wg_bf.shape, const2)],
        out_specs=[pl.BlockSpec((rc, c), lambda s: (s, 0)),
                   pl.BlockSpec((carry_rows, n_state), const2), pl.BlockSpec((carry_rows, n_state), const2)],
        out_shape=[jax.ShapeDtypeStruct((rows, c), BF16), jax.ShapeDtypeStruct((carry_rows, n_state), F32),
                   jax.ShapeDtypeStruct((carry_rows, n_state), F32)],
        scratch_shapes=[pltpu.VMEM((rc, n_state), F32), pltpu.VMEM((rc, n_state), F32),
                        pltpu.VMEM((carry_rows, n_state), F32), pltpu.VMEM((carry_rows, n_state), F32)],
        compiler_params=_cparams(("arbitrary",)),
        name="s5_core",
    )(u_tm, h0r, h0i, abr, abi, wre, wim, cre, cim, d_skip, wg_bf)


ODD_BATCH = 4
ODD_TIME_CHUNK = 128


def _odd_fused_kernel(x_ref, gpre_ref, win_ref, h0r_ref, h0i_ref, abr_ref, abi_ref, wre_ref, wim_ref, cre_ref,
                      cim_ref, d_ref, wg_ref, wout_ref, gpost_ref, o_ref, hr_ref, hi_ref,
                      stage_ref, bur_ref, bui_ref, cr_ref, ci_ref):
    step = pl.program_id(0)
    tc = x_ref.shape[1]
    n_lane_tiles = stage_ref.shape[0]

    @pl.when(step == 0)
    def _init():
        cr_ref[...] = h0r_ref[...]
        ci_ref[...] = h0i_ref[...]

    for b in range(ODD_BATCH):
        xn = _rms(x_ref[b], gpre_ref[...], RMS_EPS)
        for j in range(n_lane_tiles):
            stage_ref[j, pl.ds(b, tc, stride=ODD_BATCH), :] = xn[:, j * LANES:(j + 1) * LANES]
    xn_tm = jnp.concatenate([stage_ref[j] for j in range(n_lane_tiles)], axis=-1).astype(BF16)
    u = jnp.dot(xn_tm, win_ref[...], preferred_element_type=F32)
    zg = _s5_block(u, abr_ref, abi_ref, wre_ref, wim_ref, cre_ref, cim_ref, d_ref, wg_ref,
                   bur_ref, bui_ref, cr_ref, ci_ref, ODD_BATCH)
    mix = jnp.dot(zg.astype(BF16), wout_ref[...], preferred_element_type=F32)
    normed = _rms(mix, gpost_ref[...], RMS_EPS)
    for j in range(n_lane_tiles):
        stage_ref[j] = normed[:, j * LANES:(j + 1) * LANES]
    for b in range(ODD_BATCH):
        back = jnp.concatenate([stage_ref[j, pl.ds(b, tc, stride=ODD_BATCH), :] for j in range(n_lane_tiles)], axis=-1)
        o_ref[b] = x_ref[b] + back

    @pl.when(step == pl.num_programs(0) - 1)
    def _state():
        hr_ref[...] = cr_ref[...]
        hi_ref[...] = ci_ref[...]


def _odd_fused(x3, h0r, h0i, ow, tc):
    b, s, d = x3.shape
    assert b == ODD_BATCH
    n_state = ow["abr"].shape[1]
    const2 = lambda i: (0, 0)
    const3 = lambda i: (0, 0, 0)
    full = lambda a: pl.BlockSpec(a.shape, const2 if a.ndim == 2 else const3)
    rows = ODD_BATCH * tc
    return pl.pallas_call(
        _odd_fused_kernel,
        grid=(s // tc,),
        in_specs=[pl.BlockSpec((b, tc, d), lambda i: (0, i, 0)), full(ow["g_pre"]), full(ow["w_in"]),
                  full(h0r), full(h0i), full(ow["abr"]), full(ow["abi"]), full(ow["wre"]), full(ow["wim"]),
                  full(ow["cre"]), full(ow["cim"]), full(ow["d"]), full(ow["w_gate"]), full(ow["w_out"]),
                  full(ow["g_post"])],
        out_specs=[pl.BlockSpec((b, tc, d), lambda i: (0, i, 0)), full(h0r), full(h0i)],
        out_shape=[jax.ShapeDtypeStruct((b, s, d), F32), jax.ShapeDtypeStruct(h0r.shape, F32),
                   jax.ShapeDtypeStruct(h0i.shape, F32)],
        scratch_shapes=[pltpu.VMEM((d // LANES, rows, LANES), F32),
                        pltpu.VMEM((rows, n_state), F32), pltpu.VMEM((rows, n_state), F32),
                        pltpu.VMEM(h0r.shape, F32), pltpu.VMEM(h0i.shape, F32)],
        compiler_params=_cparams(("arbitrary",)),
        name="odd_mixer_fused",
    )(x3, ow["g_pre"], ow["w_in"], h0r, h0i, ow["abr"], ow["abi"], ow["wre"], ow["wim"], ow["cre"], ow["cim"],
      ow["d"], ow["w_gate"], ow["w_out"], ow["g_post"])


def _block_diag_tiles(x, rows_first):
    nt, ng, a, b = x.shape
    eye = jnp.eye(ng, dtype=jnp.bool_)
    out = jnp.where(eye[None, :, None, :, None], x[:, :, :, None, :], jnp.zeros((), x.dtype))
    del rows_first
    return out.reshape(nt, ng * a, ng * b)


def _tile(n):
    for cand in (512, 256, 128, 64, 32, 16, 8):
        if n % cand == 0:
            return cand
    return n


def _even_layer(hp, hs, layer, idx, ew, k_stack, v_stack, kcache, vcache, page_table, state_conv):
    b, s, d = hp.shape
    nb, n_new, _ = hs.shape
    m = nb * n_new
    tm = _tile(s)
    lam_init = _lambda_init(layer)
    row = lambda b_, i: (b_, i, 0)
    q, ktb, vb, conv_p, buf_p, k_stack, v_stack = _even_front(hp, ew, k_stack, v_stack, tm)
    qs, kts, _, vs, _, ugs = _even_inproj(hs.reshape(1, m, d), ew["g_pre"], ew["w_in"], ew["wkt"], _tile(m))
    q5 = qs.reshape(nb, n_new, N_HEADS, 2, QK_DIM).transpose(0, 2, 3, 1, 4)
    q5 = jnp.pad(q5, ((0, 0), (0, 0), (0, 0), (0, SUBLANES - n_new), (0, 0)))
    eye = jnp.eye(2 * N_HEADS, dtype=jnp.bool_).reshape(N_HEADS, 2, 1, N_HEADS, 2, 1)
    qbd = jnp.where(eye[None], q5[:, :, :, :, None, None, :], jnp.zeros((), BF16)).reshape(nb, Q_ROWS, ATT_W)
    ktn = kts.reshape(ATT_W, nb, n_new).transpose(1, 0, 2)
    ktn = jnp.pad(ktn, ((0, 0), (0, 0), (0, LANES - n_new))).astype(BF16)
    vn = vs.reshape(nb, n_new, N_HEADS, HEAD_W).transpose(0, 2, 1, 3)
    vn = jnp.pad(vn, ((0, 0), (0, 0), (0, LANES - n_new), (0, 0))).astype(BF16)
    attn_p, attn_s = _attention(q, ktb, vb, page_table, qbd, ktn, vn, kcache, vcache, ew["lq"], ew["subln_g"],
                                idx, n_new, lam_init, tm)
    specs = [pl.BlockSpec((None, tm, ATT_W), row), pl.BlockSpec((None, tm, CONV_CH), row)]
    hp = _outproj([attn_p, conv_p], specs, ew["w_out"], ew["g_post"], hp, tm)
    conv_s, buf_s = _conv_branch(ugs.reshape(nb, n_new, 2 * CONV_CH), state_conv, ew["conv_w"], ew["conv_b"],
                                 ew["ln_g"], ew["ln_b"], n_new)
    tms = _tile(m)
    specs = [pl.BlockSpec((None, tms, ATT_W), row), pl.BlockSpec((None, tms, CONV_CH), row)]
    hs_new = _outproj([attn_s[:, :n_new, :].reshape(1, m, ATT_W), conv_s.reshape(1, m, CONV_CH)], specs,
                      ew["w_out"], ew["g_post"], hs.reshape(1, m, d), tms)
    k_new_s = kts.reshape(N_HEADS, 2, QK_DIM, nb, n_new).transpose(3, 4, 0, 1, 2)
    v_new_s = vs.reshape(nb, n_new, N_HEADS, HEAD_W)
    return hp, hs_new.reshape(nb, n_new, d), k_stack, v_stack, buf_p, k_new_s, v_new_s, buf_s


def _odd_layer_prompt(hp, ow):
    b, s, d = hp.shape
    n_state = ow["abr"].shape[1]
    zeros = jnp.zeros((SUBLANES, n_state), F32)
    hp, hr, hi = _odd_fused(hp, zeros, zeros, ow, min(s, ODD_TIME_CHUNK))
    return hp, hr[:b], hi[:b]


def _odd_layer_sample(hs, ow, h0r, h0i):
    nb, n_new, d = hs.shape
    m = nb * n_new
    u_tm = _odd_inproj(hs.transpose(1, 0, 2).reshape(1, m, d), ow["g_pre"], ow["w_in"], _tile(m))
    z, hr, hi = _s5_core(u_tm, h0r, h0i, ow["abr"], ow["abi"], ow["wre"], ow["wim"], ow["cre"], ow["cim"],
                         ow["d"], ow["w_gate"], nb, m)
    z = z.reshape(n_new, nb, d).transpose(1, 0, 2).reshape(1, m, d)
    tm = _tile(m)
    specs = [pl.BlockSpec((None, tm, d), lambda b_, i: (b_, i, 0))]
    out = _outproj([z], specs, ow["w_out"], ow["g_post"], hs.reshape(1, m, d), tm)
    return out.reshape(nb, n_new, d), hr, hi


def kernel(x_prompt, x_sample, cache_k, cache_v, page_table, state_conv, state_ssm_re, state_ssm_im, g_mix_pre, g_mix_post, g_ffn_pre, g_ffn_post, w_in_even, lambda_qk, subln_g, conv_w, conv_b, conv_ln_g, conv_ln_b, w_out_even, w_in_odd, ssm_a_re, ssm_a_im, ssm_b_re, ssm_b_im, ssm_c_re, ssm_c_im, ssm_d, ssm_log_dt, w_gate_odd, w_out_odd, w_ffn_up, w_ffn_down):
    depth, d = g_mix_pre.shape
    hp, hs = x_prompt, x_sample
    nb, n_new, _ = x_sample.shape
    n_layers_even, n_phys, page = cache_k.shape[:3]
    n_groups, n_p = ssm_a_re.shape[1:]
    n_tiles = n_groups // GROUPS_PER_TILE
    n_state = n_groups * n_p

    kcache = cache_k.transpose(0, 1, 3, 4, 5, 2).reshape(n_layers_even, n_phys, ATT_W, page)
    vcache = cache_v.reshape(n_layers_even, n_phys, page * N_HEADS, HEAD_W)

    row = lambda x: x.reshape(1, -1)
    wu_all = w_ffn_up.astype(BF16)
    wd_all = w_ffn_down.astype(BF16)
    outs = {k: [] for k in ("ks", "vs", "cp", "cs", "srp", "sip", "srs", "sis")}
    k_stack = v_stack = None
    for layer in range(depth):
        i = layer // 2
        if layer % 2 == 0:
            w_in = w_in_even[i].astype(BF16)
            ew = dict(g_pre=row(g_mix_pre[layer]), g_post=row(g_mix_post[layer]), w_in=w_in,
                      wkt=w_in[:, ATT_W:2 * ATT_W].T, lq=lambda_qk[i], subln_g=row(subln_g[i]),
                      conv_w=conv_w[i], conv_b=row(conv_b[i]), ln_g=row(conv_ln_g[i]), ln_b=row(conv_ln_b[i]),
                      w_out=w_out_even[i].astype(BF16))
            hp, hs, k_stack, v_stack, buf_p, k_new, v_new, buf_s = _even_layer(
                hp, hs, layer, i, ew, k_stack, v_stack, kcache, vcache, page_table, state_conv[i])
            outs["cp"].append(buf_p)
            outs["ks"].append(k_new); outs["vs"].append(v_new); outs["cs"].append(buf_s)
        else:
            lr = ssm_a_re[i].reshape(n_tiles, STATE_TILE)
            li = ssm_a_im[i].reshape(n_tiles, STATE_TILE)
            ldt = jnp.repeat(ssm_log_dt[i], n_p).reshape(n_tiles, STATE_TILE)
            b4 = lambda x: x.reshape(n_tiles, GROUPS_PER_TILE, n_p, GROUP_CH).transpose(0, 1, 3, 2)
            abr, abi, wre, wim = _s5_prep(lr, li, ldt, _block_diag_tiles(b4(ssm_b_re[i]), True),
                                          _block_diag_tiles(b4(ssm_b_im[i]), True))
            c4 = lambda x: x.reshape(n_tiles, GROUPS_PER_TILE, GROUP_CH, n_p).transpose(0, 1, 3, 2)
            ow = dict(g_pre=row(g_mix_pre[layer]), g_post=row(g_mix_post[layer]), w_in=w_in_odd[i].astype(BF16),
                      abr=abr.reshape(1, n_state), abi=abi.reshape(1, n_state), wre=wre, wim=wim,
                      cre=_block_diag_tiles(c4(ssm_c_re[i]), False).astype(BF16),
                      cim=_block_diag_tiles(c4(ssm_c_im[i]), False).astype(BF16),
                      d=row(ssm_d[i]), w_gate=w_gate_odd[i].astype(BF16), w_out=w_out_odd[i].astype(BF16))
            hp, sr, si = _odd_layer_prompt(hp, ow)
            outs["srp"].append(sr.reshape(-1, n_groups, n_p)); outs["sip"].append(si.reshape(-1, n_groups, n_p))
            hs, sr, si = _odd_layer_sample(hs, ow, state_ssm_re[i].reshape(nb, n_state),
                                           state_ssm_im[i].reshape(nb, n_state))
            outs["srs"].append(sr.reshape(nb, n_groups, n_p)); outs["sis"].append(si.reshape(nb, n_groups, n_p))
        gpre, gpost = row(g_ffn_pre[layer]), row(g_ffn_post[layer])
        mp = hp.shape[0] * hp.shape[1]
        hp = _mlp(hp.reshape(mp, d), gpre, wu_all, wd_all, gpost, layer, MLP_TM if mp % MLP_TM == 0 else _tile(mp),
                  MLP_TK).reshape(hp.shape)
        ms = nb * n_new
        hs = _mlp(hs.reshape(ms, d), gpre, wu_all, wd_all, gpost, layer, _tile(ms), MLP_TK).reshape(hs.shape)
    st = jnp.stack
    bp, sp = x_prompt.shape[:2]
    new_k_prompt = k_stack.reshape(n_layers_even, bp, N_HEADS, 2, QK_DIM, sp).transpose(0, 1, 5, 2, 3, 4)
    new_v_prompt = v_stack.reshape(n_layers_even, bp, sp, N_HEADS, HEAD_W)
    return (hp, hs, new_k_prompt, new_v_prompt, st(outs["ks"]), st(outs["vs"]), st(outs["cp"]), st(outs["cs"]),
            st(outs["srp"]), st(outs["sip"]), st(outs["srs"]), st(outs["sis"]))
```

```python
import functools
import math

import jax
import jax.numpy as jnp
import numpy as np
from jax import lax
from jax.experimental import pallas as pl
from jax.experimental.pallas import tpu as pltpu

F32 = jnp.float32
BF16 = jnp.bfloat16

RMS_EPS = 1e-6
SUBLN_EPS = 1e-5
LN_EPS = 1e-5
LOG2_E = math.log2(math.e)

LANES = 128
SUBLANES = 8
N_HEADS = 4
QK_DIM = 64
HEAD_W = 2 * QK_DIM
ATT_W = N_HEADS * HEAD_W
CONV_CH = 512
CONV_TAPS = 31
CONV_HIST = CONV_TAPS - 1
GROUP_CH = 16
STATE_P = 64
GROUPS_PER_TILE = LANES // GROUP_CH
STATE_TILE = GROUPS_PER_TILE * STATE_P
CONV_CHUNK = 32
CONV_PAD = 32
VMEM_LIMIT = 52 * 1024 * 1024


def _cparams(semantics):
    return pltpu.CompilerParams(dimension_semantics=semantics, vmem_limit_bytes=VMEM_LIMIT)


def _rms(x, g, eps):
    ms = jnp.mean(x * x, axis=-1, keepdims=True)
    return (x * lax.rsqrt(ms + eps)) * g


def _sigmoid(x):
    return 1.0 / (1.0 + jnp.exp(-x))


def _lambda_full(lq, lam_init):
    s1 = jnp.sum(lq[0:1, :] * lq[1:2, :], axis=-1, keepdims=True)
    s2 = jnp.sum(lq[2:3, :] * lq[3:4, :], axis=-1, keepdims=True)
    return jnp.exp(s1) - jnp.exp(s2) + lam_init


def _lambda_init(layer):
    return 0.8 - 0.6 * math.exp(-0.3 * layer)


def _even_inproj_kernel(x_ref, g_ref, w_ref, wkt_ref, q_ref, kt_ref, ktb_ref, v_ref, vb_ref, ug_ref):
    xn = _rms(x_ref[...], g_ref[...], RMS_EPS).astype(BF16)

    def proj(c0, width):
        return jnp.dot(xn, w_ref[:, c0:c0 + width], preferred_element_type=F32)

    q_ref[...] = (proj(0, ATT_W) * (QK_DIM ** -0.5 * LOG2_E)).astype(BF16)
    kt = lax.dot_general(wkt_ref[...], xn, (((1,), (1,)), ((), ())), preferred_element_type=F32)
    kt_ref[...] = kt
    ktb_ref[...] = kt.astype(BF16)
    v = proj(2 * ATT_W, ATT_W)
    v_ref[...] = v
    vb_ref[...] = v.astype(BF16)
    ug_ref[...] = proj(3 * ATT_W, 2 * CONV_CH)


def _even_inproj(x3, g, w_bf, wkt_bf, tm):
    bx, sx, d = x3.shape
    n_in = w_bf.shape[1]
    row = lambda b, i: (b, i, 0)
    col = lambda b, i: (b, 0, i)
    const = lambda b, i: (0, 0)
    return pl.pallas_call(
        _even_inproj_kernel,
        grid=(bx, sx // tm),
        in_specs=[pl.BlockSpec((None, tm, d), row), pl.BlockSpec((1, d), const),
                  pl.BlockSpec((d, n_in), const), pl.BlockSpec((ATT_W, d), const)],
        out_specs=[pl.BlockSpec((None, tm, ATT_W), row), pl.BlockSpec((None, ATT_W, tm), col),
                   pl.BlockSpec((None, ATT_W, tm), col), pl.BlockSpec((None, tm, ATT_W), row),
                   pl.BlockSpec((None, tm, ATT_W), row), pl.BlockSpec((None, tm, 2 * CONV_CH), row)],
        out_shape=[jax.ShapeDtypeStruct((bx, sx, ATT_W), BF16), jax.ShapeDtypeStruct((bx, ATT_W, sx), F32),
                   jax.ShapeDtypeStruct((bx, ATT_W, sx), BF16), jax.ShapeDtypeStruct((bx, sx, ATT_W), F32),
                   jax.ShapeDtypeStruct((bx, sx, ATT_W), BF16), jax.ShapeDtypeStruct((bx, sx, 2 * CONV_CH), F32)],
        compiler_params=_cparams(("parallel", "parallel")),
        name="even_inproj",
    )(x3, g, w_bf, wkt_bf)


def _conv_taps(full_ref, shift_ref, w_ref, b_ref, lg_ref, lb_ref, c_ref, tt):
    h0 = CONV_PAD - CONV_HIST
    n_shift = shift_ref.shape[1]
    for phase in range(1, SUBLANES):
        shift_ref[phase - 1] = full_ref[phase:phase + n_shift, :]
    chunk = min(tt, CONV_CHUNK)
    for r0 in range(0, tt, chunk):
        acc = jnp.zeros((chunk, CONV_CH), F32)
        for tap in range(CONV_TAPS):
            tile_row, phase = divmod(h0 + tap, SUBLANES)
            start = tile_row * SUBLANES + r0
            if phase == 0:
                window = full_ref[start:start + chunk, :]
            else:
                window = shift_ref[phase - 1, start:start + chunk, :]
            acc = acc + w_ref[tap:tap + 1, :] * window
        c = acc + b_ref[...]
        mu = jnp.mean(c, axis=-1, keepdims=True)
        xc = c - mu
        y = xc * lax.rsqrt(jnp.mean(xc * xc, axis=-1, keepdims=True) + LN_EPS)
        y = y * lg_ref[...] + lb_ref[...]
        c_ref[r0:r0 + chunk, :] = (y * _sigmoid(y)).astype(BF16)


def _even_front_kernel(*refs, n_prev):
    x_ref, g_ref, w_ref, wkt_ref, cw_ref, cb_ref, lg_ref, lb_ref = refs[:8]
    prev_refs = refs[8:8 + 2 * (n_prev > 0)]
    q_ref, ktb_ref, vb_ref, c_ref, nb_ref, kt_ref, v_ref, full_ref, shift_ref = refs[8 + len(prev_refs):]
    t = pl.program_id(1)
    tt = x_ref.shape[0]
    h0 = CONV_PAD - CONV_HIST
    xn = _rms(x_ref[...], g_ref[...], RMS_EPS).astype(BF16)

    def proj(c0, width):
        return jnp.dot(xn, w_ref[:, c0:c0 + width], preferred_element_type=F32)

    q_ref[...] = (proj(0, ATT_W) * (QK_DIM ** -0.5 * LOG2_E)).astype(BF16)
    kt = lax.dot_general(wkt_ref[...], xn, (((1,), (1,)), ((), ())), preferred_element_type=F32)
    for layer in range(n_prev):
        kt_ref[layer] = prev_refs[0][layer]
        v_ref[layer] = prev_refs[1][layer]
    kt_ref[n_prev] = kt
    ktb_ref[...] = kt.astype(BF16)
    v = proj(2 * ATT_W, ATT_W)
    vb_ref[...] = v.astype(BF16)
    for h in range(N_HEADS):
        v_ref[n_prev, pl.ds(h, tt, stride=N_HEADS), :] = v[:, h * HEAD_W:(h + 1) * HEAD_W]

    @pl.when(t == 0)
    def _first():
        full_ref[h0:CONV_PAD, :] = jnp.zeros((CONV_HIST, CONV_CH), F32)

    @pl.when(t > 0)
    def _carry():
        full_ref[h0:CONV_PAD, :] = full_ref[tt + h0:tt + CONV_PAD, :]

    full_ref[CONV_PAD:CONV_PAD + tt, :] = proj(3 * ATT_W, CONV_CH) * _sigmoid(proj(3 * ATT_W + CONV_CH, CONV_CH))
    _conv_taps(full_ref, shift_ref, cw_ref, cb_ref, lg_ref, lb_ref, c_ref, tt)

    @pl.when(t == pl.num_programs(1) - 1)
    def _tail():
        nb_ref[...] = full_ref[tt + h0:tt + CONV_PAD, :]


def _even_front(x3, ew, k_prev, v_prev, tt):
    b, s, d = x3.shape
    w_bf = ew["w_in"]
    n_prev = 0 if k_prev is None else k_prev.shape[0]
    row = lambda b_, t: (b_, t, 0)
    col = lambda b_, t: (b_, 0, t)
    per_b = lambda b_, t: (b_, 0, 0)
    const = lambda b_, t: (0, 0)
    k_block = lambda n: pl.BlockSpec((n, None, ATT_W, tt), lambda b_, t: (0, b_, 0, t))
    v_block = lambda n: pl.BlockSpec((n, None, tt * N_HEADS, HEAD_W), lambda b_, t: (0, b_, t, 0))
    prev = [] if n_prev == 0 else [k_prev, v_prev]
    prev_specs = [] if n_prev == 0 else [k_block(n_prev), v_block(n_prev)]
    return pl.pallas_call(
        functools.partial(_even_front_kernel, n_prev=n_prev),
        grid=(b, s // tt),
        in_specs=[pl.BlockSpec((None, tt, d), row), pl.BlockSpec((1, d), const),
                  pl.BlockSpec(w_bf.shape, const), pl.BlockSpec((ATT_W, d), const),
                  pl.BlockSpec((CONV_TAPS, CONV_CH), const), pl.BlockSpec((1, CONV_CH), const),
                  pl.BlockSpec((1, CONV_CH), const), pl.BlockSpec((1, CONV_CH), const)] + prev_specs,
        out_specs=[pl.BlockSpec((None, tt, ATT_W), row), pl.BlockSpec((None, ATT_W, tt), col),
                   pl.BlockSpec((None, tt, ATT_W), row), pl.BlockSpec((None, tt, CONV_CH), row),
                   pl.BlockSpec((None, CONV_HIST, CONV_CH), per_b), k_block(n_prev + 1), v_block(n_prev + 1)],
        out_shape=[jax.ShapeDtypeStruct((b, s, ATT_W), BF16), jax.ShapeDtypeStruct((b, ATT_W, s), BF16),
                   jax.ShapeDtypeStruct((b, s, ATT_W), BF16), jax.ShapeDtypeStruct((b, s, CONV_CH), BF16),
                   jax.ShapeDtypeStruct((b, CONV_HIST, CONV_CH), F32),
                   jax.ShapeDtypeStruct((n_prev + 1, b, ATT_W, s), F32),
                   jax.ShapeDtypeStruct((n_prev + 1, b, s * N_HEADS, HEAD_W), F32)],
        scratch_shapes=[pltpu.VMEM((CONV_PAD + tt, CONV_CH), F32),
                        pltpu.VMEM((SUBLANES - 1, CONV_PAD - SUBLANES + tt, CONV_CH), F32)],
        compiler_params=_cparams(("parallel", "arbitrary")),
        name="even_front",
    )(x3, ew["g_pre"], w_bf, ew["wkt"], ew["conv_w"], ew["conv_b"], ew["ln_g"], ew["ln_b"], *prev)


PAGES_PER_STEP = 16
Q_ROWS = N_HEADS * 2 * SUBLANES


def _flash_step(i, j, lq_ref, g_ref, q_ref, kt_ref, v_ref, o_ref, qm_ref, m_ref, l_ref, acc_ref, lam_init):
    heads = [slice(h * HEAD_W, (h + 1) * HEAD_W) for h in range(N_HEADS)]

    @pl.when(j == 0)
    def _init():
        q = q_ref[...]
        lane = lax.broadcasted_iota(jnp.int32, (q.shape[0], HEAD_W), 1)
        first = jnp.where(lane < QK_DIM, 1.0, 0.0).astype(BF16)
        for h, hs in enumerate(heads):
            qm_ref[2 * h] = q[:, hs] * first
            qm_ref[2 * h + 1] = q[:, hs] * (1.0 - first)
        m_ref[...] = jnp.full(m_ref.shape, -jnp.inf, F32)
        l_ref[...] = jnp.zeros(l_ref.shape, F32)
        acc_ref[...] = jnp.zeros(acc_ref.shape, F32)

    def step(masked):
        n_blk = kt_ref.shape[1] // LANES
        for h, hs in enumerate(heads):
            kt = kt_ref[hs, :]
            v = v_ref[:, hs]
            for c in range(2):
                hc = 2 * h + c
                s = jnp.dot(qm_ref[hc], kt, preferred_element_type=F32)
                if masked:
                    row = lax.broadcasted_iota(jnp.int32, s.shape, 0)
                    col = lax.broadcasted_iota(jnp.int32, s.shape, 1)
                    s = jnp.where(col <= row, s, -jnp.inf)
                blocks = [s[:, b * LANES:(b + 1) * LANES] for b in range(n_blk)]
                mx = functools.reduce(jnp.maximum, blocks)
                m_prev = m_ref[hc]
                m_new = jnp.maximum(m_prev, jnp.max(mx, axis=-1, keepdims=True))
                alpha = jnp.exp2(m_prev - m_new)
                ps = [jnp.exp2(blk - m_new) for blk in blocks]
                l_ref[hc] = alpha * l_ref[hc] + functools.reduce(jnp.add, ps)
                p = jnp.concatenate([x.astype(BF16) for x in ps], axis=-1)
                acc_ref[hc] = alpha * acc_ref[hc] + jnp.dot(p, v, preferred_element_type=F32)
                m_ref[hc] = m_new

    @pl.when(j < i)
    def _full():
        step(False)

    @pl.when(j == i)
    def _diag():
        step(True)
        lam = _lambda_full(lq_ref[...], lam_init)
        for h, hs in enumerate(heads):
            l0 = jnp.sum(l_ref[2 * h], axis=-1, keepdims=True)
            l1 = jnp.sum(l_ref[2 * h + 1], axis=-1, keepdims=True)
            o = acc_ref[2 * h] / l0 - lam * (acc_ref[2 * h + 1] / l1)
            o_ref[:, hs] = (_rms(o, g_ref[...], SUBLN_EPS) * (1.0 - lam_init)).astype(BF16)


def _paged_step(step, lq_ref, g_ref, qbd_ref, ktn_ref, vn_ref, k_refs, v_refs, o_ref, m_ref, l_ref, acc_ref,
                n_steps, n_new, lam_init):
    rows_per_head = 2 * SUBLANES

    @pl.when(step == 0)
    def _init():
        m_ref[...] = jnp.full(m_ref.shape, -jnp.inf, F32)
        l_ref[...] = jnp.zeros(l_ref.shape, F32)
        acc_ref[...] = jnp.zeros(acc_ref.shape, F32)

    def update(scores, values):
        m_prev = m_ref[...]
        m_new = m_prev
        for s in scores:
            m_new = jnp.maximum(m_new, jnp.max(s, axis=-1, keepdims=True))
        alpha = jnp.exp2(m_prev - m_new)
        l_new = alpha * l_ref[...]
        acc = alpha * acc_ref[...]
        parts = [acc[h * rows_per_head:(h + 1) * rows_per_head] for h in range(N_HEADS)]
        for s, vh in zip(scores, values):
            p = jnp.exp2(s - m_new)
            l_new = l_new + jnp.sum(p, axis=-1, keepdims=True)
            pb = p.astype(BF16)
            for h in range(N_HEADS):
                parts[h] = parts[h] + jnp.dot(pb[h * rows_per_head:(h + 1) * rows_per_head], vh[h],
                                              preferred_element_type=F32)
        m_ref[...] = m_new
        l_ref[...] = l_new
        acc_ref[...] = jnp.concatenate(parts, axis=0)

    qbd = qbd_ref[...]
    scores, values = [], []
    for kr, vr in zip(k_refs, v_refs):
        scores.append(jnp.dot(qbd, kr[...].astype(BF16), preferred_element_type=F32))
        values.append([vr[pl.ds(h, LANES, stride=N_HEADS), :].astype(BF16) for h in range(N_HEADS)])
    update(scores, values)

    @pl.when(step == n_steps - 1)
    def _new_tokens():
        s = jnp.dot(qbd, ktn_ref[...], preferred_element_type=F32)
        tok = lax.broadcasted_iota(jnp.int32, s.shape, 0) % SUBLANES
        col = lax.broadcasted_iota(jnp.int32, s.shape, 1)
        s = jnp.where((col <= tok) & (col < n_new), s, -jnp.inf)
        update([s], [[vn_ref[h] for h in range(N_HEADS)]])
        lam = _lambda_full(lq_ref[...], lam_init)
        acc = acc_ref[...]
        l = l_ref[...]
        g = g_ref[...]
        for h in range(N_HEADS):
            r0 = h * rows_per_head
            o1 = acc[r0:r0 + SUBLANES] / l[r0:r0 + SUBLANES]
            o2 = acc[r0 + SUBLANES:r0 + 2 * SUBLANES] / l[r0 + SUBLANES:r0 + 2 * SUBLANES]
            o = o1 - lam * o2
            o_ref[:, h * HEAD_W:(h + 1) * HEAD_W] = _rms(o, g, SUBLN_EPS) * (1.0 - lam_init)


def _attn_kernel(fi_ref, fj_ref, pt_ref, lq_ref, g_ref, q_ref, kt_ref, v_ref, qbd_ref, ktn_ref, vn_ref, *rest,
                 n_pairs, n_paged, n_steps, n_new, lam_init):
    del pt_ref
    k_refs = rest[:PAGES_PER_STEP]
    v_refs = rest[PAGES_PER_STEP:2 * PAGES_PER_STEP]
    of_ref, op_ref, qm_ref, fm_ref, fl_ref, facc_ref, pm_ref, pl_ref, pacc_ref = rest[2 * PAGES_PER_STEP:]
    t = pl.program_id(1)
    gi = pl.program_id(0) * pl.num_programs(1) + t

    @pl.when(t < n_pairs)
    def _flash():
        _flash_step(fi_ref[t], fj_ref[t], lq_ref, g_ref, q_ref, kt_ref, v_ref, of_ref, qm_ref, fm_ref, fl_ref,
                    facc_ref, lam_init)

    @pl.when(gi < n_paged)
    def _paged():
        _paged_step(lax.rem(gi, n_steps), lq_ref, g_ref, qbd_ref, ktn_ref, vn_ref, k_refs, v_refs, op_ref,
                    pm_ref, pl_ref, pacc_ref, n_steps, n_new, lam_init)


def _attention(q, kt, v, page_table, qbd, ktn, vn, kcache, vcache, lq, subln_g, layer, n_new, lam_init, tq):
    b, s, _ = q.shape
    nq = s // tq
    nb, n_pages = page_table.shape
    n_steps = n_pages // PAGES_PER_STEP
    page_rows = kcache.shape[2]
    pairs = [(i, j) for i in range(nq) for j in range(i + 1)]
    n_pairs = len(pairs)
    n_paged = nb * n_steps
    n_t = max(n_pairs, -(-n_paged // b))
    pairs = pairs + [pairs[-1]] * (n_t - n_pairs)
    fi = jnp.asarray(np.array([p[0] for p in pairs], np.int32))
    fj = jnp.asarray(np.array([p[1] for p in pairs], np.int32))

    def paged_pos(bb, t):
        step = jnp.minimum(bb * n_t + t, n_paged - 1)
        return step // n_steps, lax.rem(step, n_steps)

    def page_map(jj):
        def index_map(bb, t, fi_, fj_, pt):
            entry, st = paged_pos(bb, t)
            return (layer, pt[entry * n_pages + st * PAGES_PER_STEP + jj], 0, 0)
        return index_map

    const = lambda bb, t, fi_, fj_, pt: (0, 0)
    per_b3 = lambda bb, t, fi_, fj_, pt: (paged_pos(bb, t)[0], 0, 0)
    per_b4 = lambda bb, t, fi_, fj_, pt: (paged_pos(bb, t)[0], 0, 0, 0)
    q_map = lambda bb, t, fi_, fj_, pt: (bb, fi_[t], 0)
    kt_map = lambda bb, t, fi_, fj_, pt: (bb, 0, fj_[t])
    v_map = lambda bb, t, fi_, fj_, pt: (bb, fj_[t], 0)
    page_specs = [pl.BlockSpec((None, None, page_rows, LANES), page_map(jj)) for jj in range(PAGES_PER_STEP)]
    grid_spec = pltpu.PrefetchScalarGridSpec(
        num_scalar_prefetch=3,
        grid=(b, n_t),
        in_specs=[pl.BlockSpec((4, QK_DIM), const), pl.BlockSpec((1, HEAD_W), const),
                  pl.BlockSpec((None, tq, ATT_W), q_map), pl.BlockSpec((None, ATT_W, tq), kt_map),
                  pl.BlockSpec((None, tq, ATT_W), v_map),
                  pl.BlockSpec((None, Q_ROWS, ATT_W), per_b3), pl.BlockSpec((None, ATT_W, LANES), per_b3),
                  pl.BlockSpec((None, N_HEADS, LANES, HEAD_W), per_b4)] + page_specs + page_specs,
        out_specs=[pl.BlockSpec((None, tq, ATT_W), q_map), pl.BlockSpec((None, SUBLANES, ATT_W), per_b3)],
        scratch_shapes=[pltpu.VMEM((2 * N_HEADS, tq, HEAD_W), BF16), pltpu.VMEM((2 * N_HEADS, tq, LANES), F32),
                        pltpu.VMEM((2 * N_HEADS, tq, LANES), F32), pltpu.VMEM((2 * N_HEADS, tq, HEAD_W), F32),
                        pltpu.VMEM((Q_ROWS, 1), F32), pltpu.VMEM((Q_ROWS, 1), F32),
                        pltpu.VMEM((Q_ROWS, HEAD_W), F32)],
    )
    return pl.pallas_call(
        functools.partial(_attn_kernel, n_pairs=n_pairs, n_paged=n_paged, n_steps=n_steps, n_new=n_new,
                          lam_init=lam_init),
        grid_spec=grid_spec,
        out_shape=[jax.ShapeDtypeStruct((b, s, ATT_W), BF16), jax.ShapeDtypeStruct((nb, SUBLANES, ATT_W), F32)],
        compiler_params=_cparams(("arbitrary", "arbitrary")),
        name="diff_attn",
    )(fi, fj, page_table.reshape(-1), lq, subln_g, q, kt, v, qbd, ktn, vn,
      *([kcache] * PAGES_PER_STEP), *([vcache] * PAGES_PER_STEP))


def _conv_kernel(ug_ref, hist_ref, w_ref, b_ref, lg_ref, lb_ref, c_ref, nb_ref, full_ref, shift_ref):
    t = pl.program_id(1)
    tt = ug_ref.shape[0]
    h0 = CONV_PAD - CONV_HIST

    @pl.when(t == 0)
    def _first():
        full_ref[h0:CONV_PAD, :] = hist_ref[...]

    @pl.when(t > 0)
    def _carry():
        full_ref[h0:CONV_PAD, :] = full_ref[tt + h0:tt + CONV_PAD, :]

    ug = ug_ref[...]
    full_ref[CONV_PAD:CONV_PAD + tt, :] = ug[:, :CONV_CH] * _sigmoid(ug[:, CONV_CH:])
    _conv_taps(full_ref, shift_ref, w_ref, b_ref, lg_ref, lb_ref, c_ref, tt)

    @pl.when(t == pl.num_programs(1) - 1)
    def _tail():
        nb_ref[...] = full_ref[tt + h0:tt + CONV_PAD, :]


def _conv_branch(ug, hist, w, b, lg, lb, tt):
    bx, sx, _ = ug.shape
    row = lambda b_, t: (b_, t, 0)
    per_b = lambda b_, t: (b_, 0, 0)
    const = lambda b_, t: (0, 0)
    return pl.pallas_call(
        _conv_kernel,
        grid=(bx, sx // tt),
        in_specs=[pl.BlockSpec((None, tt, 2 * CONV_CH), row), pl.BlockSpec((None, CONV_HIST, CONV_CH), per_b),
                  pl.BlockSpec((CONV_TAPS, CONV_CH), const), pl.BlockSpec((1, CONV_CH), const),
                  pl.BlockSpec((1, CONV_CH), const), pl.BlockSpec((1, CONV_CH), const)],
        out_specs=[pl.BlockSpec((None, tt, CONV_CH), row), pl.BlockSpec((None, CONV_HIST, CONV_CH), per_b)],
        out_shape=[jax.ShapeDtypeStruct((bx, sx, CONV_CH), BF16), jax.ShapeDtypeStruct((bx, CONV_HIST, CONV_CH), F32)],
        scratch_shapes=[pltpu.VMEM((CONV_PAD + tt, CONV_CH), F32),
                        pltpu.VMEM((SUBLANES - 1, CONV_PAD - SUBLANES + tt, CONV_CH), F32)],
        compiler_params=_cparams(("parallel", "arbitrary")),
        name="conv_branch",
    )(ug, hist, w, b, lg, lb)


def _outproj_kernel(*refs, widths):
    a_refs = refs[:len(widths)]
    w_ref, g_ref, h_ref, o_ref = refs[len(widths):]
    acc = None
    r0 = 0
    for a_ref, width in zip(a_refs, widths):
        part = jnp.dot(a_ref[...].astype(BF16), w_ref[r0:r0 + width, :], preferred_element_type=F32)
        acc = part if acc is None else acc + part
        r0 += width
    o_ref[...] = h_ref[...] + _rms(acc, g_ref[...], RMS_EPS)


def _outproj(a_list, a_specs, w_bf, g, h3, tm):
    bx, sx, d = h3.shape
    widths = tuple(spec.block_shape[-1] for spec in a_specs)
    row = lambda b, i: (b, i, 0)
    const = lambda b, i: (0, 0)
    return pl.pallas_call(
        functools.partial(_outproj_kernel, widths=widths),
        grid=(bx, sx // tm),
        in_specs=list(a_specs) + [pl.BlockSpec(w_bf.shape, const), pl.BlockSpec((1, d), const),
                                  pl.BlockSpec((None, tm, d), row)],
        out_specs=pl.BlockSpec((None, tm, d), row),
        out_shape=jax.ShapeDtypeStruct((bx, sx, d), F32),
        compiler_params=_cparams(("parallel", "parallel")),
        name="mixer_outproj",
    )(*a_list, w_bf, g, h3)


MLP_TM = 1024
MLP_TK = 1024


def _mlp_kernel(x_ref, gpre_ref, wu_ref, wd_ref, gpost_ref, o_ref, xn_ref, acc_ref):
    k = pl.program_id(1)

    @pl.when(k == 0)
    def _init():
        xn_ref[...] = _rms(x_ref[...], gpre_ref[...], RMS_EPS).astype(BF16)
        acc_ref[...] = jnp.zeros(acc_ref.shape, F32)

    up = jnp.dot(xn_ref[...], wu_ref[...].astype(BF16), preferred_element_type=F32)
    act = jnp.square(jnp.maximum(up, 0.0)).astype(BF16)
    acc_ref[...] += jnp.dot(act, wd_ref[...].astype(BF16), preferred_element_type=F32)

    @pl.when(k == pl.num_programs(1) - 1)
    def _done():
        o_ref[...] = x_ref[...] + _rms(acc_ref[...], gpost_ref[...], RMS_EPS)


def _mlp(x2, gpre, wu, wd, gpost, layer, tm, tk):
    m, d = x2.shape
    dff = wu.shape[2]
    return pl.pallas_call(
        _mlp_kernel,
        grid=(m // tm, dff // tk),
        in_specs=[pl.BlockSpec((tm, d), lambda i, k: (i, 0)), pl.BlockSpec((1, d), lambda i, k: (0, 0)),
                  pl.BlockSpec((None, d, tk), lambda i, k: (layer, 0, k)),
                  pl.BlockSpec((None, tk, d), lambda i, k: (layer, k, 0)),
                  pl.BlockSpec((1, d), lambda i, k: (0, 0))],
        out_specs=pl.BlockSpec((tm, d), lambda i, k: (i, 0)),
        out_shape=jax.ShapeDtypeStruct((m, d), F32),
        scratch_shapes=[pltpu.VMEM((tm, d), BF16), pltpu.VMEM((tm, d), F32)],
        compiler_params=_cparams(("parallel", "arbitrary")),
        name="sq_relu_mlp",
    )(x2, gpre, wu, wd, gpost)


def _odd_inproj_kernel(x_ref, g_ref, w_ref, u_ref):
    xn = _rms(x_ref[...], g_ref[...], RMS_EPS).astype(BF16)
    u_ref[...] = jnp.dot(xn, w_ref[...], preferred_element_type=F32)


def _odd_inproj(x3, g, w_bf, tm):
    bx, sx, d = x3.shape
    c = w_bf.shape[1]
    out = pl.pallas_call(
        _odd_inproj_kernel,
        grid=(bx, sx // tm),
        in_specs=[pl.BlockSpec((None, tm, d), lambda b, i: (b, i, 0)), pl.BlockSpec((1, d), lambda b, i: (0, 0)),
                  pl.BlockSpec((d, c), lambda b, i: (0, 0))],
        out_specs=pl.BlockSpec((tm, c), lambda b, i: (i, b)),
        out_shape=jax.ShapeDtypeStruct((sx, bx * c), F32),
        compiler_params=_cparams(("parallel", "parallel")),
        name="odd_inproj",
    )(x3, g, w_bf)
    return out.reshape(sx * bx, c)


def _s5_prep_kernel(lr_ref, li_ref, ldt_ref, br_ref, bi_ref, abr_ref, abi_ref, wre_ref, wim_ref):
    lr = lr_ref[...]
    li = li_ref[...]
    dt = jnp.exp(ldt_ref[...])
    mag = jnp.exp(lr * dt)
    ang = li * dt
    abr = mag * jnp.cos(ang)
    abi = mag * jnp.sin(ang)
    den = lr * lr + li * li
    nr = abr - 1.0
    ni = abi
    fr = (nr * lr + ni * li) / den
    fi = (ni * lr - nr * li) / den
    abr_ref[...] = abr
    abi_ref[...] = abi
    for j in range(lr.shape[0]):
        frj = fr[j:j + 1, :]
        fij = fi[j:j + 1, :]
        br = br_ref[j]
        bi = bi_ref[j]
        wr = frj * br - fij * bi
        wi = frj * bi + fij * br
        wre_ref[j, :LANES, :] = wr.astype(BF16)
        wim_ref[j, :LANES, :] = wi.astype(BF16)
        arj = abr[j:j + 1, :]
        aij = abi[j:j + 1, :]
        wre_ref[j, LANES:, :] = (arj * wr - aij * wi).astype(BF16)
        wim_ref[j, LANES:, :] = (arj * wi + aij * wr).astype(BF16)


def _s5_prep(lr, li, ldt, br_bd, bi_bd):
    nt = lr.shape[0]
    vm = pl.BlockSpec(memory_space=pltpu.VMEM)
    return pl.pallas_call(
        _s5_prep_kernel,
        in_specs=[vm] * 5,
        out_specs=[vm] * 4,
        out_shape=[jax.ShapeDtypeStruct((nt, STATE_TILE), F32), jax.ShapeDtypeStruct((nt, STATE_TILE), F32),
                   jax.ShapeDtypeStruct((nt, 2 * LANES, STATE_TILE), BF16),
                   jax.ShapeDtypeStruct((nt, 2 * LANES, STATE_TILE), BF16)],
        name="s5_discretise",
    )(lr, li, ldt, br_bd, bi_bd)


SCAN_LANES = 512


def _scan_interleaved_group(bur_ref, bui_ref, cr_ref, ci_ref, abr_ref, abi_ref, g):
    rc = bur_ref.shape[0]
    upper = lax.broadcasted_iota(jnp.int32, (SUBLANES, SCAN_LANES), 0) >= 4
    sl = slice(g * SCAN_LANES, (g + 1) * SCAN_LANES)
    ar = jnp.broadcast_to(abr_ref[:, sl], (SUBLANES, SCAN_LANES))
    ai = jnp.broadcast_to(abi_ref[:, sl], (SUBLANES, SCAN_LANES))
    a2r = ar * ar - ai * ai
    a2i = ar * ai + ai * ar
    acr = jnp.where(upper, a2r, ar)
    aci = jnp.where(upper, a2i, ai)

    def body(r, carry):
        pr, pi = carry
        rows = pl.ds(pl.multiple_of(r * SUBLANES, SUBLANES), SUBLANES)
        x1r = bur_ref[rows, sl]
        x1i = bui_ref[rows, sl]
        x2r = x1r + (acr * pr - aci * pi)
        x2i = x1i + (acr * pi + aci * pr)
        bur_ref[rows, sl] = x2r
        bui_ref[rows, sl] = x2i
        npr = jnp.where(upper, x2r, pltpu.roll(x2r, 4, 0))
        npi = jnp.where(upper, x2i, pltpu.roll(x2i, 4, 0))
        return npr, npi

    pr, pi = lax.fori_loop(0, rc // SUBLANES, body, (cr_ref[:, sl], ci_ref[:, sl]), unroll=True)
    cr_ref[:, sl] = pr
    ci_ref[:, sl] = pi


def _scan_dense_group(bur_ref, bui_ref, cr_ref, ci_ref, abr_ref, abi_ref, g, bsz):
    rc = bur_ref.shape[0]
    sl = slice(g * SCAN_LANES, (g + 1) * SCAN_LANES)
    ar = abr_ref[:, sl]
    ai = abi_ref[:, sl]
    pr = cr_ref[:, sl]
    pi = ci_ref[:, sl]
    for t in range(rc // bsz):
        rows = slice(t * bsz, (t + 1) * bsz)
        nr = bur_ref[rows, sl] + (ar * pr - ai * pi)
        ni = bui_ref[rows, sl] + (ar * pi + ai * pr)
        bur_ref[rows, sl] = nr
        bui_ref[rows, sl] = ni
        pr, pi = nr, ni
    cr_ref[:, sl] = pr
    ci_ref[:, sl] = pi


def _s5_block(u, abr_ref, abi_ref, wre_ref, wim_ref, cre_ref, cim_ref, d_ref, wg_ref,
              bur_ref, bui_ref, cr_ref, ci_ref, bsz):
    n_tiles = wre_ref.shape[0]
    ub = u.astype(BF16)
    if bsz == 4:
        row = lax.broadcasted_iota(jnp.int32, u.shape, 0)
        ub_prev = jnp.where((row & 4) != 0, pltpu.roll(u, 4, 0), 0.0).astype(BF16)
    ys = []
    for j in range(n_tiles):
        cols = slice(j * LANES, (j + 1) * LANES)
        states = slice(j * STATE_TILE, (j + 1) * STATE_TILE)
        if bsz == 4:
            lhs = jnp.concatenate([ub[:, cols], ub_prev[:, cols]], axis=-1)
            bur_ref[:, states] = jnp.dot(lhs, wre_ref[j], preferred_element_type=F32)
            bui_ref[:, states] = jnp.dot(lhs, wim_ref[j], preferred_element_type=F32)
            _scan_interleaved_group(bur_ref, bui_ref, cr_ref, ci_ref, abr_ref, abi_ref, j)
        else:
            bur_ref[:, states] = jnp.dot(ub[:, cols], wre_ref[j, :LANES, :], preferred_element_type=F32)
            bui_ref[:, states] = jnp.dot(ub[:, cols], wim_ref[j, :LANES, :], preferred_element_type=F32)
            _scan_dense_group(bur_ref, bui_ref, cr_ref, ci_ref, abr_ref, abi_ref, j, bsz)
        hr = bur_ref[:, states].astype(BF16)
        hi = bui_ref[:, states].astype(BF16)
        ys.append(jnp.dot(hr, cre_ref[j], preferred_element_type=F32)
                  - jnp.dot(hi, cim_ref[j], preferred_element_type=F32))
    y = jnp.concatenate(ys, axis=-1) + d_ref[...] * u
    z = 0.5 * y * (1.0 + lax.erf(y * math.sqrt(0.5)))
    gate = jnp.dot(z.astype(BF16), wg_ref[...], preferred_element_type=F32)
    return z * _sigmoid(gate)


def _s5_kernel(u_ref, h0r_ref, h0i_ref, abr_ref, abi_ref, wre_ref, wim_ref, cre_ref, cim_ref, d_ref, wg_ref,
               z_ref, hr_ref, hi_ref, bur_ref, bui_ref, cr_ref, ci_ref, *, bsz):
    step = pl.program_id(0)

    @pl.when(step == 0)
    def _init():
        cr_ref[...] = h0r_ref[...]
        ci_ref[...] = h0i_ref[...]

    zg = _s5_block(u_ref[...], abr_ref, abi_ref, wre_ref, wim_ref, cre_ref, cim_ref, d_ref, wg_ref,
                   bur_ref, bui_ref, cr_ref, ci_ref, bsz)
    z_ref[...] = zg.astype(BF16)

    @pl.when(step == pl.num_programs(0) - 1)
    def _state():
        hr_ref[...] = cr_ref[...]
        hi_ref[...] = ci_ref[...]


def _s5_core(u_tm, h0r, h0i, abr, abi, wre, wim, cre, cim, d_skip, wg_bf, bsz, rc):
    rows, c = u_tm.shape
    n_state = abr.shape[1]
    carry_rows = h0r.shape[0]
    const2 = lambda s: (0, 0)
    const3 = lambda s: (0, 0, 0)
    return pl.pallas_call(
        functools.partial(_s5_kernel, bsz=bsz),
        grid=(rows // rc,),
        in_specs=[pl.BlockSpec((rc, c), lambda s: (s, 0)),
                  pl.BlockSpec((carry_rows, n_state), const2), pl.BlockSpec((carry_rows, n_state), const2),
                  pl.BlockSpec((1, n_state), const2), pl.BlockSpec((1, n_state), const2),
                  pl.BlockSpec(wre.shape, const3), pl.BlockSpec(wim.shape, const3),
                  pl.BlockSpec(cre.shape, const3), pl.BlockSpec(cim.shape, const3),
                  pl.BlockSpec((1, c), const2), pl.BlockSpec(wg_bf.shape, const2)],
        out_specs=[pl.BlockSpec((rc, c), lambda s: (s, 0)),
                   pl.BlockSpec((carry_rows, n_state), const2), pl.BlockSpec((carry_rows, n_state), const2)],
        out_shape=[jax.ShapeDtypeStruct((rows, c), BF16), jax.ShapeDtypeStruct((carry_rows, n_state), F32),
                   jax.ShapeDtypeStruct((carry_rows, n_state), F32)],
        scratch_shapes=[pltpu.VMEM((rc, n_state), F32), pltpu.VMEM((rc, n_state), F32),
                        pltpu.VMEM((carry_rows, n_state), F32), pltpu.VMEM((carry_rows, n_state), F32)],
        compiler_params=_cparams(("arbitrary",)),
        name="s5_core",
    )(u_tm, h0r, h0i, abr, abi, wre, wim, cre, cim, d_skip, wg_bf)


ODD_BATCH = 4
ODD_TIME_CHUNK = 128


def _odd_fused_kernel(x_ref, gpre_ref, win_ref, h0r_ref, h0i_ref, abr_ref, abi_ref, wre_ref, wim_ref, cre_ref,
                      cim_ref, d_ref, wg_ref, wout_ref, gpost_ref, o_ref, hr_ref, hi_ref,
                      stage_ref, bur_ref, bui_ref, cr_ref, ci_ref):
    step = pl.program_id(0)
    tc = x_ref.shape[1]
    n_lane_tiles = stage_ref.shape[0]

    @pl.when(step == 0)
    def _init():
        cr_ref[...] = h0r_ref[...]
        ci_ref[...] = h0i_ref[...]

    for b in range(ODD_BATCH):
        xn = _rms(x_ref[b], gpre_ref[...], RMS_EPS)
        for j in range(n_lane_tiles):
            stage_ref[j, pl.ds(b, tc, stride=ODD_BATCH), :] = xn[:, j * LANES:(j + 1) * LANES]
    xn_tm = jnp.concatenate([stage_ref[j] for j in range(n_lane_tiles)], axis=-1).astype(BF16)
    u = jnp.dot(xn_tm, win_ref[...], preferred_element_type=F32)
    zg = _s5_block(u, abr_ref, abi_ref, wre_ref, wim_ref, cre_ref, cim_ref, d_ref, wg_ref,
                   bur_ref, bui_ref, cr_ref, ci_ref, ODD_BATCH)
    mix = jnp.dot(zg.astype(BF16), wout_ref[...], preferred_element_type=F32)
    normed = _rms(mix, gpost_ref[...], RMS_EPS)
    for j in range(n_lane_tiles):
        stage_ref[j] = normed[:, j * LANES:(j + 1) * LANES]
    for b in range(ODD_BATCH):
        back = jnp.concatenate([stage_ref[j, pl.ds(b, tc, stride=ODD_BATCH), :] for j in range(n_lane_tiles)], axis=-1)
        o_ref[b] = x_ref[b] + back

    @pl.when(step == pl.num_programs(0) - 1)
    def _state():
        hr_ref[...] = cr_ref[...]
        hi_ref[...] = ci_ref[...]


def _odd_fused(x3, h0r, h0i, ow, tc):
    b, s, d = x3.shape
    assert b == ODD_BATCH
    n_state = ow["abr"].shape[1]
    const2 = lambda i: (0, 0)
    const3 = lambda i: (0, 0, 0)
    full = lambda a: pl.BlockSpec(a.shape, const2 if a.ndim == 2 else const3)
    rows = ODD_BATCH * tc
    return pl.pallas_call(
        _odd_fused_kernel,
        grid=(s // tc,),
        in_specs=[pl.BlockSpec((b, tc, d), lambda i: (0, i, 0)), full(ow["g_pre"]), full(ow["w_in"]),
                  full(h0r), full(h0i), full(ow["abr"]), full(ow["abi"]), full(ow["wre"]), full(ow["wim"]),
                  full(ow["cre"]), full(ow["cim"]), full(ow["d"]), full(ow["w_gate"]), full(ow["w_out"]),
                  full(ow["g_post"])],
        out_specs=[pl.BlockSpec((b, tc, d), lambda i: (0, i, 0)), full(h0r), full(h0i)],
        out_shape=[jax.ShapeDtypeStruct((b, s, d), F32), jax.ShapeDtypeStruct(h0r.shape, F32),
                   jax.ShapeDtypeStruct(h0i.shape, F32)],
        scratch_shapes=[pltpu.VMEM((d // LANES, rows, LANES), F32),
                        pltpu.VMEM((rows, n_state), F32), pltpu.VMEM((rows, n_state), F32),
                        pltpu.VMEM(h0r.shape, F32), pltpu.VMEM(h0i.shape, F32)],
        compiler_params=_cparams(("arbitrary",)),
        name="odd_mixer_fused",
    )(x3, ow["g_pre"], ow["w_in"], h0r, h0i, ow["abr"], ow["abi"], ow["wre"], ow["wim"], ow["cre"], ow["cim"],
      ow["d"], ow["w_gate"], ow["w_out"], ow["g_post"])


def _block_diag_tiles(x, rows_first):
    nt, ng, a, b = x.shape
    eye = jnp.eye(ng, dtype=jnp.bool_)
    out = jnp.where(eye[None, :, None, :, None], x[:, :, :, None, :], jnp.zeros((), x.dtype))
    del rows_first
    return out.reshape(nt, ng * a, ng * b)


def _tile(n):
    for cand in (512, 256, 128, 64, 32, 16, 8):
        if n % cand == 0:
            return cand
    return n


def _even_layer(hp, hs, layer, idx, ew, k_stack, v_stack, kcache, vcache, page_table, state_conv):
    b, s, d = hp.shape
    nb, n_new, _ = hs.shape
    m = nb * n_new
    tm = _tile(s)
    lam_init = _lambda_init(layer)
    row = lambda b_, i: (b_, i, 0)
    q, ktb, vb, conv_p, buf_p, k_stack, v_stack = _even_front(hp, ew, k_stack, v_stack, tm)
    qs, kts, _, vs, _, ugs = _even_inproj(hs.reshape(1, m, d), ew["g_pre"], ew["w_in"], ew["wkt"], _tile(m))
    q5 = qs.reshape(nb, n_new, N_HEADS, 2, QK_DIM).transpose(0, 2, 3, 1, 4)
    q5 = jnp.pad(q5, ((0, 0), (0, 0), (0, 0), (0, SUBLANES - n_new), (0, 0)))
    eye = jnp.eye(2 * N_HEADS, dtype=jnp.bool_).reshape(N_HEADS, 2, 1, N_HEADS, 2, 1)
    qbd = jnp.where(eye[None], q5[:, :, :, :, None, None, :], jnp.zeros((), BF16)).reshape(nb, Q_ROWS, ATT_W)
    ktn = kts.reshape(ATT_W, nb, n_new).transpose(1, 0, 2)
    ktn = jnp.pad(ktn, ((0, 0), (0, 0), (0, LANES - n_new))).astype(BF16)
    vn = vs.reshape(nb, n_new, N_HEADS, HEAD_W).transpose(0, 2, 1, 3)
    vn = jnp.pad(vn, ((0, 0), (0, 0), (0, LANES - n_new), (0, 0))).astype(BF16)
    attn_p, attn_s = _attention(q, ktb, vb, page_table, qbd, ktn, vn, kcache, vcache, ew["lq"], ew["subln_g"],
                                idx, n_new, lam_init, tm)
    specs = [pl.BlockSpec((None, tm, ATT_W), row), pl.BlockSpec((None, tm, CONV_CH), row)]
    hp = _outproj([attn_p, conv_p], specs, ew["w_out"], ew["g_post"], hp, tm)
    conv_s, buf_s = _conv_branch(ugs.reshape(nb, n_new, 2 * CONV_CH), state_conv, ew["conv_w"], ew["conv_b"],
                                 ew["ln_g"], ew["ln_b"], n_new)
    tms = _tile(m)
    specs = [pl.BlockSpec((None, tms, ATT_W), row), pl.BlockSpec((None, tms, CONV_CH), row)]
    hs_new = _outproj([attn_s[:, :n_new, :].reshape(1, m, ATT_W), conv_s.reshape(1, m, CONV_CH)], specs,
                      ew["w_out"], ew["g_post"], hs.reshape(1, m, d), tms)
    k_new_s = kts.reshape(N_HEADS, 2, QK_DIM, nb, n_new).transpose(3, 4, 0, 1, 2)
    v_new_s = vs.reshape(nb, n_new, N_HEADS, HEAD_W)
    return hp, hs_new.reshape(nb, n_new, d), k_stack, v_stack, buf_p, k_new_s, v_new_s, buf_s


def _odd_layer_prompt(hp, ow):
    b, s, d = hp.shape
    n_state = ow["abr"].shape[1]
    zeros = jnp.zeros((SUBLANES, n_state), F32)
    hp, hr, hi = _odd_fused(hp, zeros, zeros, ow, min(s, ODD_TIME_CHUNK))
    return hp, hr[:b], hi[:b]


def _odd_layer_sample(hs, ow, h0r, h0i):
    nb, n_new, d = hs.shape
    m = nb * n_new
    u_tm = _odd_inproj(hs.transpose(1, 0, 2).reshape(1, m, d), ow["g_pre"], ow["w_in"], _tile(m))
    z, hr, hi = _s5_core(u_tm, h0r, h0i, ow["abr"], ow["abi"], ow["wre"], ow["wim"], ow["cre"], ow["cim"],
                         ow["d"], ow["w_gate"], nb, m)
    z = z.reshape(n_new, nb, d).transpose(1, 0, 2).reshape(1, m, d)
    tm = _tile(m)
    specs = [pl.BlockSpec((None, tm, d), lambda b_, i: (b_, i, 0))]
    out = _outproj([z], specs, ow["w_out"], ow["g_post"], hs.reshape(1, m, d), tm)
    return out.reshape(nb, n_new, d), hr, hi


def kernel(x_prompt, x_sample, cache_k, cache_v, page_table, state_conv, state_ssm_re, state_ssm_im, g_mix_pre, g_mix_post, g_ffn_pre, g_ffn_post, w_in_even, lambda_qk, subln_g, conv_w, conv_b, conv_ln_g, conv_ln_b, w_out_even, w_in_odd, ssm_a_re, ssm_a_im, ssm_b_re, ssm_b_im, ssm_c_re, ssm_c_im, ssm_d, ssm_log_dt, w_gate_odd, w_out_odd, w_ffn_up, w_ffn_down):
    depth, d = g_mix_pre.shape
    hp, hs = x_prompt, x_sample
    nb, n_new, _ = x_sample.shape
    n_layers_even, n_phys, page = cache_k.shape[:3]
    n_groups, n_p = ssm_a_re.shape[1:]
    n_tiles = n_groups // GROUPS_PER_TILE
    n_state = n_groups * n_p

    kcache = cache_k.transpose(0, 1, 3, 4, 5, 2).reshape(n_layers_even, n_phys, ATT_W, page)
    vcache = cache_v.reshape(n_layers_even, n_phys, page * N_HEADS, HEAD_W)

    row = lambda x: x.reshape(1, -1)
    outs = {k: [] for k in ("ks", "vs", "cp", "cs", "srp", "sip", "srs", "sis")}
    k_stack = v_stack = None
    for layer in range(depth):
        i = layer // 2
        if layer % 2 == 0:
            w_in = w_in_even[i].astype(BF16)
            ew = dict(g_pre=row(g_mix_pre[layer]), g_post=row(g_mix_post[layer]), w_in=w_in,
                      wkt=w_in[:, ATT_W:2 * ATT_W].T, lq=lambda_qk[i], subln_g=row(subln_g[i]),
                      conv_w=conv_w[i], conv_b=row(conv_b[i]), ln_g=row(conv_ln_g[i]), ln_b=row(conv_ln_b[i]),
                      w_out=w_out_even[i].astype(BF16))
            hp, hs, k_stack, v_stack, buf_p, k_new, v_new, buf_s = _even_layer(
                hp, hs, layer, i, ew, k_stack, v_stack, kcache, vcache, page_table, state_conv[i])
            outs["cp"].append(buf_p)
            outs["ks"].append(k_new); outs["vs"].append(v_new); outs["cs"].append(buf_s)
        else:
            lr = ssm_a_re[i].reshape(n_tiles, STATE_TILE)
            li = ssm_a_im[i].reshape(n_tiles, STATE_TILE)
            ldt = jnp.repeat(ssm_log_dt[i], n_p).reshape(n_tiles, STATE_TILE)
            b4 = lambda x: x.reshape(n_tiles, GROUPS_PER_TILE, n_p, GROUP_CH).transpose(0, 1, 3, 2)
            abr, abi, wre, wim = _s5_prep(lr, li, ldt, _block_diag_tiles(b4(ssm_b_re[i]), True),
                                          _block_diag_tiles(b4(ssm_b_im[i]), True))
            c4 = lambda x: x.reshape(n_tiles, GROUPS_PER_TILE, GROUP_CH, n_p).transpose(0, 1, 3, 2)
            ow = dict(g_pre=row(g_mix_pre[layer]), g_post=row(g_mix_post[layer]), w_in=w_in_odd[i].astype(BF16),
                      abr=abr.reshape(1, n_state), abi=abi.reshape(1, n_state), wre=wre, wim=wim,
                      cre=_block_diag_tiles(c4(ssm_c_re[i]), False).astype(BF16),
                      cim=_block_diag_tiles(c4(ssm_c_im[i]), False).astype(BF16),
                      d=row(ssm_d[i]), w_gate=w_gate_odd[i].astype(BF16), w_out=w_out_odd[i].astype(BF16))
            hp, sr, si = _odd_layer_prompt(hp, ow)
            outs["srp"].append(sr.reshape(-1, n_groups, n_p)); outs["sip"].append(si.reshape(-1, n_groups, n_p))
            hs, sr, si = _odd_layer_sample(hs, ow, state_ssm_re[i].reshape(nb, n_state),
                                           state_ssm_im[i].reshape(nb, n_state))
            outs["srs"].append(sr.reshape(nb, n_groups, n_p)); outs["sis"].append(si.reshape(nb, n_groups, n_p))
        gpre, gpost = row(g_ffn_pre[layer]), row(g_ffn_post[layer])
        mp = hp.shape[0] * hp.shape[1]
        hp = _mlp(hp.reshape(mp, d), gpre, w_ffn_up, w_ffn_down, gpost, layer,
                  MLP_TM if mp % MLP_TM == 0 else _tile(mp), MLP_TK).reshape(hp.shape)
        ms = nb * n_new
        hs = _mlp(hs.reshape(ms, d), gpre, w_ffn_up, w_ffn_down, gpost, layer, _tile(ms), MLP_TK).reshape(hs.shape)
    st = jnp.stack
    bp, sp = x_prompt.shape[:2]
    new_k_prompt = k_stack.reshape(n_layers_even, bp, N_HEADS, 2, QK_DIM, sp).transpose(0, 1, 5, 2, 3, 4)
    new_v_prompt = v_stack.reshape(n_layers_even, bp, sp, N_HEADS, HEAD_W)
    return (hp, hs, new_k_prompt, new_v_prompt, st(outs["ks"]), st(outs["vs"]), st(outs["cp"]), st(outs["cs"]),
            st(outs["srp"]), st(outs["sip"]), st(outs["srs"]), st(outs["sis"]))
```

```python
import functools
import math

import jax
import jax.numpy as jnp
import numpy as np
from jax import lax
from jax.experimental import pallas as pl
from jax.experimental.pallas import tpu as pltpu

F32 = jnp.float32
BF16 = jnp.bfloat16

RMS_EPS = 1e-6
SUBLN_EPS = 1e-5
LN_EPS = 1e-5
LOG2_E = math.log2(math.e)

LANES = 128
SUBLANES = 8
N_HEADS = 4
QK_DIM = 64
HEAD_W = 2 * QK_DIM
ATT_W = N_HEADS * HEAD_W
CONV_CH = 512
CONV_TAPS = 31
CONV_HIST = CONV_TAPS - 1
GROUP_CH = 16
STATE_P = 64
GROUPS_PER_TILE = LANES // GROUP_CH
STATE_TILE = GROUPS_PER_TILE * STATE_P
CONV_CHUNK = 64
CONV_PAD = 32
VMEM_LIMIT = 52 * 1024 * 1024


def _cparams(semantics):
    return pltpu.CompilerParams(dimension_semantics=semantics, vmem_limit_bytes=VMEM_LIMIT)


def _rms(x, g, eps):
    ms = jnp.mean(x * x, axis=-1, keepdims=True)
    return (x * lax.rsqrt(ms + eps)) * g


def _sigmoid(x):
    return 1.0 / (1.0 + jnp.exp(-x))


def _lambda_full(lq, lam_init):
    s1 = jnp.sum(lq[0:1, :] * lq[1:2, :], axis=-1, keepdims=True)
    s2 = jnp.sum(lq[2:3, :] * lq[3:4, :], axis=-1, keepdims=True)
    return jnp.exp(s1) - jnp.exp(s2) + lam_init


def _lambda_init(layer):
    return 0.8 - 0.6 * math.exp(-0.3 * layer)


def _even_inproj_kernel(x_ref, g_ref, w_ref, wkt_ref, q_ref, kt_ref, ktb_ref, v_ref, vb_ref, ug_ref):
    xn = _rms(x_ref[...], g_ref[...], RMS_EPS).astype(BF16)

    def proj(c0, width):
        return jnp.dot(xn, w_ref[:, c0:c0 + width], preferred_element_type=F32)

    q_ref[...] = (proj(0, ATT_W) * (QK_DIM ** -0.5 * LOG2_E)).astype(BF16)
    kt = lax.dot_general(wkt_ref[...], xn, (((1,), (1,)), ((), ())), preferred_element_type=F32)
    kt_ref[...] = kt
    ktb_ref[...] = kt.astype(BF16)
    v = proj(2 * ATT_W, ATT_W)
    v_ref[...] = v
    vb_ref[...] = v.astype(BF16)
    ug_ref[...] = proj(3 * ATT_W, 2 * CONV_CH)


def _even_inproj(x3, g, w_bf, wkt_bf, tm):
    bx, sx, d = x3.shape
    n_in = w_bf.shape[1]
    row = lambda b, i: (b, i, 0)
    col = lambda b, i: (b, 0, i)
    const = lambda b, i: (0, 0)
    return pl.pallas_call(
        _even_inproj_kernel,
        grid=(bx, sx // tm),
        in_specs=[pl.BlockSpec((None, tm, d), row), pl.BlockSpec((1, d), const),
                  pl.BlockSpec((d, n_in), const), pl.BlockSpec((ATT_W, d), const)],
        out_specs=[pl.BlockSpec((None, tm, ATT_W), row), pl.BlockSpec((None, ATT_W, tm), col),
                   pl.BlockSpec((None, ATT_W, tm), col), pl.BlockSpec((None, tm, ATT_W), row),
                   pl.BlockSpec((None, tm, ATT_W), row), pl.BlockSpec((None, tm, 2 * CONV_CH), row)],
        out_shape=[jax.ShapeDtypeStruct((bx, sx, ATT_W), BF16), jax.ShapeDtypeStruct((bx, ATT_W, sx), F32),
                   jax.ShapeDtypeStruct((bx, ATT_W, sx), BF16), jax.ShapeDtypeStruct((bx, sx, ATT_W), F32),
                   jax.ShapeDtypeStruct((bx, sx, ATT_W), BF16), jax.ShapeDtypeStruct((bx, sx, 2 * CONV_CH), F32)],
        compiler_params=_cparams(("parallel", "parallel")),
        name="even_inproj",
    )(x3, g, w_bf, wkt_bf)


def _conv_taps(full_ref, shift_ref, w_ref, b_ref, lg_ref, lb_ref, c_ref, tt):
    h0 = CONV_PAD - CONV_HIST
    n_shift = shift_ref.shape[1]
    for phase in range(1, SUBLANES):
        shift_ref[phase - 1] = full_ref[phase:phase + n_shift, :]
    chunk = min(tt, CONV_CHUNK)
    for r0 in range(0, tt, chunk):
        acc = jnp.zeros((chunk, CONV_CH), F32)
        for tap in range(CONV_TAPS):
            tile_row, phase = divmod(h0 + tap, SUBLANES)
            start = tile_row * SUBLANES + r0
            if phase == 0:
                window = full_ref[start:start + chunk, :]
            else:
                window = shift_ref[phase - 1, start:start + chunk, :]
            acc = acc + w_ref[tap:tap + 1, :] * window
        c = acc + b_ref[...]
        mu = jnp.mean(c, axis=-1, keepdims=True)
        xc = c - mu
        y = xc * lax.rsqrt(jnp.mean(xc * xc, axis=-1, keepdims=True) + LN_EPS)
        y = y * lg_ref[...] + lb_ref[...]
        c_ref[r0:r0 + chunk, :] = (y * _sigmoid(y)).astype(BF16)


def _even_front_kernel(*refs, n_prev):
    x_ref, g_ref, w_ref, wkt_ref, cw_ref, cb_ref, lg_ref, lb_ref = refs[:8]
    prev_refs = refs[8:8 + 2 * (n_prev > 0)]
    q_ref, ktb_ref, vb_ref, c_ref, nb_ref, kt_ref, v_ref, full_ref, shift_ref = refs[8 + len(prev_refs):]
    t = pl.program_id(1)
    tt = x_ref.shape[0]
    h0 = CONV_PAD - CONV_HIST
    xn = _rms(x_ref[...], g_ref[...], RMS_EPS).astype(BF16)

    def proj(c0, width):
        return jnp.dot(xn, w_ref[:, c0:c0 + width], preferred_element_type=F32)

    q_ref[...] = (proj(0, ATT_W) * (QK_DIM ** -0.5 * LOG2_E)).astype(BF16)
    kt = lax.dot_general(wkt_ref[...], xn, (((1,), (1,)), ((), ())), preferred_element_type=F32)
    for layer in range(n_prev):
        kt_ref[layer] = prev_refs[0][layer]
        v_ref[layer] = prev_refs[1][layer]
    kt_ref[n_prev] = kt
    ktb_ref[...] = kt.astype(BF16)
    v = proj(2 * ATT_W, ATT_W)
    vb_ref[...] = v.astype(BF16)
    for h in range(N_HEADS):
        v_ref[n_prev, pl.ds(h, tt, stride=N_HEADS), :] = v[:, h * HEAD_W:(h + 1) * HEAD_W]

    @pl.when(t == 0)
    def _first():
        full_ref[h0:CONV_PAD, :] = jnp.zeros((CONV_HIST, CONV_CH), F32)

    @pl.when(t > 0)
    def _carry():
        full_ref[h0:CONV_PAD, :] = full_ref[tt + h0:tt + CONV_PAD, :]

    full_ref[CONV_PAD:CONV_PAD + tt, :] = proj(3 * ATT_W, CONV_CH) * _sigmoid(proj(3 * ATT_W + CONV_CH, CONV_CH))
    _conv_taps(full_ref, shift_ref, cw_ref, cb_ref, lg_ref, lb_ref, c_ref, tt)

    @pl.when(t == pl.num_programs(1) - 1)
    def _tail():
        nb_ref[...] = full_ref[tt + h0:tt + CONV_PAD, :]


def _even_front(x3, ew, k_prev, v_prev, tt):
    b, s, d = x3.shape
    w_bf = ew["w_in"]
    n_prev = 0 if k_prev is None else k_prev.shape[0]
    row = lambda b_, t: (b_, t, 0)
    col = lambda b_, t: (b_, 0, t)
    per_b = lambda b_, t: (b_, 0, 0)
    const = lambda b_, t: (0, 0)
    k_block = lambda n: pl.BlockSpec((n, None, ATT_W, tt), lambda b_, t: (0, b_, 0, t))
    v_block = lambda n: pl.BlockSpec((n, None, tt * N_HEADS, HEAD_W), lambda b_, t: (0, b_, t, 0))
    prev = [] if n_prev == 0 else [k_prev, v_prev]
    prev_specs = [] if n_prev == 0 else [k_block(n_prev), v_block(n_prev)]
    return pl.pallas_call(
        functools.partial(_even_front_kernel, n_prev=n_prev),
        grid=(b, s // tt),
        in_specs=[pl.BlockSpec((None, tt, d), row), pl.BlockSpec((1, d), const),
                  pl.BlockSpec(w_bf.shape, const), pl.BlockSpec((ATT_W, d), const),
                  pl.BlockSpec((CONV_TAPS, CONV_CH), const), pl.BlockSpec((1, CONV_CH), const),
                  pl.BlockSpec((1, CONV_CH), const), pl.BlockSpec((1, CONV_CH), const)] + prev_specs,
        out_specs=[pl.BlockSpec((None, tt, ATT_W), row), pl.BlockSpec((None, ATT_W, tt), col),
                   pl.BlockSpec((None, tt, ATT_W), row), pl.BlockSpec((None, tt, CONV_CH), row),
                   pl.BlockSpec((None, CONV_HIST, CONV_CH), per_b), k_block(n_prev + 1), v_block(n_prev + 1)],
        out_shape=[jax.ShapeDtypeStruct((b, s, ATT_W), BF16), jax.ShapeDtypeStruct((b, ATT_W, s), BF16),
                   jax.ShapeDtypeStruct((b, s, ATT_W), BF16), jax.ShapeDtypeStruct((b, s, CONV_CH), BF16),
                   jax.ShapeDtypeStruct((b, CONV_HIST, CONV_CH), F32),
                   jax.ShapeDtypeStruct((n_prev + 1, b, ATT_W, s), F32),
                   jax.ShapeDtypeStruct((n_prev + 1, b, s * N_HEADS, HEAD_W), F32)],
        scratch_shapes=[pltpu.VMEM((CONV_PAD + tt, CONV_CH), F32),
                        pltpu.VMEM((SUBLANES - 1, CONV_PAD - SUBLANES + tt, CONV_CH), F32)],
        compiler_params=_cparams(("parallel", "arbitrary")),
        name="even_front",
    )(x3, ew["g_pre"], w_bf, ew["wkt"], ew["conv_w"], ew["conv_b"], ew["ln_g"], ew["ln_b"], *prev)


PAGES_PER_STEP = 16
Q_ROWS = N_HEADS * 2 * SUBLANES


def _flash_step(i, j, lq_ref, g_ref, q_ref, kt_ref, v_ref, o_ref, qm_ref, m_ref, l_ref, acc_ref, lam_init):
    heads = [slice(h * HEAD_W, (h + 1) * HEAD_W) for h in range(N_HEADS)]

    @pl.when(j == 0)
    def _init():
        q = q_ref[...]
        lane = lax.broadcasted_iota(jnp.int32, (q.shape[0], HEAD_W), 1)
        first = jnp.where(lane < QK_DIM, 1.0, 0.0).astype(BF16)
        for h, hs in enumerate(heads):
            qm_ref[2 * h] = q[:, hs] * first
            qm_ref[2 * h + 1] = q[:, hs] * (1.0 - first)
        m_ref[...] = jnp.full(m_ref.shape, -jnp.inf, F32)
        l_ref[...] = jnp.zeros(l_ref.shape, F32)
        acc_ref[...] = jnp.zeros(acc_ref.shape, F32)

    def step(masked):
        n_blk = kt_ref.shape[1] // LANES
        for h, hs in enumerate(heads):
            kt = kt_ref[hs, :]
            v = v_ref[:, hs]
            for c in range(2):
                hc = 2 * h + c
                s = jnp.dot(qm_ref[hc], kt, preferred_element_type=F32)
                if masked:
                    row = lax.broadcasted_iota(jnp.int32, s.shape, 0)
                    col = lax.broadcasted_iota(jnp.int32, s.shape, 1)
                    s = jnp.where(col <= row, s, -jnp.inf)
                blocks = [s[:, b * LANES:(b + 1) * LANES] for b in range(n_blk)]
                mx = functools.reduce(jnp.maximum, blocks)
                m_prev = m_ref[hc]
                m_new = jnp.maximum(m_prev, jnp.max(mx, axis=-1, keepdims=True))
                alpha = jnp.exp2(m_prev - m_new)
                ps = [jnp.exp2(blk - m_new) for blk in blocks]
                l_ref[hc] = alpha * l_ref[hc] + functools.reduce(jnp.add, ps)
                p = jnp.concatenate([x.astype(BF16) for x in ps], axis=-1)
                acc_ref[hc] = alpha * acc_ref[hc] + jnp.dot(p, v, preferred_element_type=F32)
                m_ref[hc] = m_new

    @pl.when(j < i)
    def _full():
        step(False)

    @pl.when(j == i)
    def _diag():
        step(True)
        lam = _lambda_full(lq_ref[...], lam_init)
        for h, hs in enumerate(heads):
            l0 = jnp.sum(l_ref[2 * h], axis=-1, keepdims=True)
            l1 = jnp.sum(l_ref[2 * h + 1], axis=-1, keepdims=True)
            o = acc_ref[2 * h] / l0 - lam * (acc_ref[2 * h + 1] / l1)
            o_ref[:, hs] = (_rms(o, g_ref[...], SUBLN_EPS) * (1.0 - lam_init)).astype(BF16)


def _paged_step(step, lq_ref, g_ref, qbd_ref, ktn_ref, vn_ref, k_refs, v_refs, o_ref, m_ref, l_ref, acc_ref,
                n_steps, n_new, lam_init):
    rows_per_head = 2 * SUBLANES

    @pl.when(step == 0)
    def _init():
        m_ref[...] = jnp.full(m_ref.shape, -jnp.inf, F32)
        l_ref[...] = jnp.zeros(l_ref.shape, F32)
        acc_ref[...] = jnp.zeros(acc_ref.shape, F32)

    def update(scores, values):
        m_prev = m_ref[...]
        m_new = m_prev
        for s in scores:
            m_new = jnp.maximum(m_new, jnp.max(s, axis=-1, keepdims=True))
        alpha = jnp.exp2(m_prev - m_new)
        l_new = alpha * l_ref[...]
        acc = alpha * acc_ref[...]
        parts = [acc[h * rows_per_head:(h + 1) * rows_per_head] for h in range(N_HEADS)]
        for s, vh in zip(scores, values):
            p = jnp.exp2(s - m_new)
            l_new = l_new + jnp.sum(p, axis=-1, keepdims=True)
            pb = p.astype(BF16)
            for h in range(N_HEADS):
                parts[h] = parts[h] + jnp.dot(pb[h * rows_per_head:(h + 1) * rows_per_head], vh[h],
                                              preferred_element_type=F32)
        m_ref[...] = m_new
        l_ref[...] = l_new
        acc_ref[...] = jnp.concatenate(parts, axis=0)

    qbd = qbd_ref[...]
    scores, values = [], []
    for kr, vr in zip(k_refs, v_refs):
        scores.append(jnp.dot(qbd, kr[...].astype(BF16), preferred_element_type=F32))
        values.append([vr[pl.ds(h, LANES, stride=N_HEADS), :].astype(BF16) for h in range(N_HEADS)])
    update(scores, values)

    @pl.when(step == n_steps - 1)
    def _new_tokens():
        s = jnp.dot(qbd, ktn_ref[...], preferred_element_type=F32)
        tok = lax.broadcasted_iota(jnp.int32, s.shape, 0) % SUBLANES
        col = lax.broadcasted_iota(jnp.int32, s.shape, 1)
        s = jnp.where((col <= tok) & (col < n_new), s, -jnp.inf)
        update([s], [[vn_ref[h] for h in range(N_HEADS)]])
        lam = _lambda_full(lq_ref[...], lam_init)
        acc = acc_ref[...]
        l = l_ref[...]
        g = g_ref[...]
        for h in range(N_HEADS):
            r0 = h * rows_per_head
            o1 = acc[r0:r0 + SUBLANES] / l[r0:r0 + SUBLANES]
            o2 = acc[r0 + SUBLANES:r0 + 2 * SUBLANES] / l[r0 + SUBLANES:r0 + 2 * SUBLANES]
            o = o1 - lam * o2
            o_ref[:, h * HEAD_W:(h + 1) * HEAD_W] = _rms(o, g, SUBLN_EPS) * (1.0 - lam_init)


def _attn_kernel(fi_ref, fj_ref, pt_ref, lq_ref, g_ref, q_ref, kt_ref, v_ref, qbd_ref, ktn_ref, vn_ref, *rest,
                 n_pairs, n_paged, n_steps, n_new, lam_init):
    del pt_ref
    k_refs = rest[:PAGES_PER_STEP]
    v_refs = rest[PAGES_PER_STEP:2 * PAGES_PER_STEP]
    of_ref, op_ref, qm_ref, fm_ref, fl_ref, facc_ref, pm_ref, pl_ref, pacc_ref = rest[2 * PAGES_PER_STEP:]
    t = pl.program_id(1)
    gi = pl.program_id(0) * pl.num_programs(1) + t

    @pl.when(t < n_pairs)
    def _flash():
        _flash_step(fi_ref[t], fj_ref[t], lq_ref, g_ref, q_ref, kt_ref, v_ref, of_ref, qm_ref, fm_ref, fl_ref,
                    facc_ref, lam_init)

    @pl.when(gi < n_paged)
    def _paged():
        _paged_step(lax.rem(gi, n_steps), lq_ref, g_ref, qbd_ref, ktn_ref, vn_ref, k_refs, v_refs, op_ref,
                    pm_ref, pl_ref, pacc_ref, n_steps, n_new, lam_init)


def _attention(q, kt, v, page_table, qbd, ktn, vn, kcache, vcache, lq, subln_g, layer, n_new, lam_init, tq):
    b, s, _ = q.shape
    nq = s // tq
    nb, n_pages = page_table.shape
    n_steps = n_pages // PAGES_PER_STEP
    page_rows = kcache.shape[2]
    pairs = [(i, j) for i in range(nq) for j in range(i + 1)]
    n_pairs = len(pairs)
    n_paged = nb * n_steps
    n_t = max(n_pairs, -(-n_paged // b))
    pairs = pairs + [pairs[-1]] * (n_t - n_pairs)
    fi = jnp.asarray(np.array([p[0] for p in pairs], np.int32))
    fj = jnp.asarray(np.array([p[1] for p in pairs], np.int32))

    def paged_pos(bb, t):
        step = jnp.minimum(bb * n_t + t, n_paged - 1)
        return step // n_steps, lax.rem(step, n_steps)

    def page_map(jj):
        def index_map(bb, t, fi_, fj_, pt):
            entry, st = paged_pos(bb, t)
            return (layer, pt[entry * n_pages + st * PAGES_PER_STEP + jj], 0, 0)
        return index_map

    const = lambda bb, t, fi_, fj_, pt: (0, 0)
    per_b3 = lambda bb, t, fi_, fj_, pt: (paged_pos(bb, t)[0], 0, 0)
    per_b4 = lambda bb, t, fi_, fj_, pt: (paged_pos(bb, t)[0], 0, 0, 0)
    q_map = lambda bb, t, fi_, fj_, pt: (bb, fi_[t], 0)
    kt_map = lambda bb, t, fi_, fj_, pt: (bb, 0, fj_[t])
    v_map = lambda bb, t, fi_, fj_, pt: (bb, fj_[t], 0)
    page_specs = [pl.BlockSpec((None, None, page_rows, LANES), page_map(jj)) for jj in range(PAGES_PER_STEP)]
    grid_spec = pltpu.PrefetchScalarGridSpec(
        num_scalar_prefetch=3,
        grid=(b, n_t),
        in_specs=[pl.BlockSpec((4, QK_DIM), const), pl.BlockSpec((1, HEAD_W), const),
                  pl.BlockSpec((None, tq, ATT_W), q_map), pl.BlockSpec((None, ATT_W, tq), kt_map),
                  pl.BlockSpec((None, tq, ATT_W), v_map),
                  pl.BlockSpec((None, Q_ROWS, ATT_W), per_b3), pl.BlockSpec((None, ATT_W, LANES), per_b3),
                  pl.BlockSpec((None, N_HEADS, LANES, HEAD_W), per_b4)] + page_specs + page_specs,
        out_specs=[pl.BlockSpec((None, tq, ATT_W), q_map), pl.BlockSpec((None, SUBLANES, ATT_W), per_b3)],
        scratch_shapes=[pltpu.VMEM((2 * N_HEADS, tq, HEAD_W), BF16), pltpu.VMEM((2 * N_HEADS, tq, LANES), F32),
                        pltpu.VMEM((2 * N_HEADS, tq, LANES), F32), pltpu.VMEM((2 * N_HEADS, tq, HEAD_W), F32),
                        pltpu.VMEM((Q_ROWS, 1), F32), pltpu.VMEM((Q_ROWS, 1), F32),
                        pltpu.VMEM((Q_ROWS, HEAD_W), F32)],
    )
    return pl.pallas_call(
        functools.partial(_attn_kernel, n_pairs=n_pairs, n_paged=n_paged, n_steps=n_steps, n_new=n_new,
                          lam_init=lam_init),
        grid_spec=grid_spec,
        out_shape=[jax.ShapeDtypeStruct((b, s, ATT_W), BF16), jax.ShapeDtypeStruct((nb, SUBLANES, ATT_W), F32)],
        compiler_params=_cparams(("arbitrary", "arbitrary")),
        name="diff_attn",
    )(fi, fj, page_table.reshape(-1), lq, subln_g, q, kt, v, qbd, ktn, vn,
      *([kcache] * PAGES_PER_STEP), *([vcache] * PAGES_PER_STEP))


def _conv_kernel(ug_ref, hist_ref, w_ref, b_ref, lg_ref, lb_ref, c_ref, nb_ref, full_ref, shift_ref):
    t = pl.program_id(1)
    tt = ug_ref.shape[0]
    h0 = CONV_PAD - CONV_HIST

    @pl.when(t == 0)
    def _first():
        full_ref[h0:CONV_PAD, :] = hist_ref[...]

    @pl.when(t > 0)
    def _carry():
        full_ref[h0:CONV_PAD, :] = full_ref[tt + h0:tt + CONV_PAD, :]

    ug = ug_ref[...]
    full_ref[CONV_PAD:CONV_PAD + tt, :] = ug[:, :CONV_CH] * _sigmoid(ug[:, CONV_CH:])
    _conv_taps(full_ref, shift_ref, w_ref, b_ref, lg_ref, lb_ref, c_ref, tt)

    @pl.when(t == pl.num_programs(1) - 1)
    def _tail():
        nb_ref[...] = full_ref[tt + h0:tt + CONV_PAD, :]


def _conv_branch(ug, hist, w, b, lg, lb, tt):
    bx, sx, _ = ug.shape
    row = lambda b_, t: (b_, t, 0)
    per_b = lambda b_, t: (b_, 0, 0)
    const = lambda b_, t: (0, 0)
    return pl.pallas_call(
        _conv_kernel,
        grid=(bx, sx // tt),
        in_specs=[pl.BlockSpec((None, tt, 2 * CONV_CH), row), pl.BlockSpec((None, CONV_HIST, CONV_CH), per_b),
                  pl.BlockSpec((CONV_TAPS, CONV_CH), const), pl.BlockSpec((1, CONV_CH), const),
                  pl.BlockSpec((1, CONV_CH), const), pl.BlockSpec((1, CONV_CH), const)],
        out_specs=[pl.BlockSpec((None, tt, CONV_CH), row), pl.BlockSpec((None, CONV_HIST, CONV_CH), per_b)],
        out_shape=[jax.ShapeDtypeStruct((bx, sx, CONV_CH), BF16), jax.ShapeDtypeStruct((bx, CONV_HIST, CONV_CH), F32)],
        scratch_shapes=[pltpu.VMEM((CONV_PAD + tt, CONV_CH), F32),
                        pltpu.VMEM((SUBLANES - 1, CONV_PAD - SUBLANES + tt, CONV_CH), F32)],
        compiler_params=_cparams(("parallel", "arbitrary")),
        name="conv_branch",
    )(ug, hist, w, b, lg, lb)


def _outproj_kernel(*refs, widths):
    a_refs = refs[:len(widths)]
    w_ref, g_ref, h_ref, o_ref = refs[len(widths):]
    acc = None
    r0 = 0
    for a_ref, width in zip(a_refs, widths):
        part = jnp.dot(a_ref[...].astype(BF16), w_ref[r0:r0 + width, :], preferred_element_type=F32)
        acc = part if acc is None else acc + part
        r0 += width
    o_ref[...] = h_ref[...] + _rms(acc, g_ref[...], RMS_EPS)


def _outproj(a_list, a_specs, w_bf, g, h3, tm):
    bx, sx, d = h3.shape
    widths = tuple(spec.block_shape[-1] for spec in a_specs)
    row = lambda b, i: (b, i, 0)
    const = lambda b, i: (0, 0)
    return pl.pallas_call(
        functools.partial(_outproj_kernel, widths=widths),
        grid=(bx, sx // tm),
        in_specs=list(a_specs) + [pl.BlockSpec(w_bf.shape, const), pl.BlockSpec((1, d), const),
                                  pl.BlockSpec((None, tm, d), row)],
        out_specs=pl.BlockSpec((None, tm, d), row),
        out_shape=jax.ShapeDtypeStruct((bx, sx, d), F32),
        compiler_params=_cparams(("parallel", "parallel")),
        name="mixer_outproj",
    )(*a_list, w_bf, g, h3)


MLP_TM = 1024
MLP_TK = 1024


def _mlp_kernel(x_ref, gpre_ref, wu_ref, wd_ref, gpost_ref, o_ref, xn_ref, acc_ref):
    k = pl.program_id(1)

    @pl.when(k == 0)
    def _init():
        xn_ref[...] = _rms(x_ref[...], gpre_ref[...], RMS_EPS).astype(BF16)
        acc_ref[...] = jnp.zeros(acc_ref.shape, F32)

    up = jnp.dot(xn_ref[...], wu_ref[...].astype(BF16), preferred_element_type=F32)
    act = jnp.square(jnp.maximum(up, 0.0)).astype(BF16)
    acc_ref[...] += jnp.dot(act, wd_ref[...].astype(BF16), preferred_element_type=F32)

    @pl.when(k == pl.num_programs(1) - 1)
    def _done():
        o_ref[...] = x_ref[...] + _rms(acc_ref[...], gpost_ref[...], RMS_EPS)


def _mlp(x2, gpre, wu, wd, gpost, layer, tm, tk):
    m, d = x2.shape
    dff = wu.shape[2]
    return pl.pallas_call(
        _mlp_kernel,
        grid=(m // tm, dff // tk),
        in_specs=[pl.BlockSpec((tm, d), lambda i, k: (i, 0)), pl.BlockSpec((1, d), lambda i, k: (0, 0)),
                  pl.BlockSpec((None, d, tk), lambda i, k: (layer, 0, k)),
                  pl.BlockSpec((None, tk, d), lambda i, k: (layer, k, 0)),
                  pl.BlockSpec((1, d), lambda i, k: (0, 0))],
        out_specs=pl.BlockSpec((tm, d), lambda i, k: (i, 0)),
        out_shape=jax.ShapeDtypeStruct((m, d), F32),
        scratch_shapes=[pltpu.VMEM((tm, d), BF16), pltpu.VMEM((tm, d), F32)],
        compiler_params=_cparams(("parallel", "arbitrary")),
        name="sq_relu_mlp",
    )(x2, gpre, wu, wd, gpost)


def _odd_inproj_kernel(x_ref, g_ref, w_ref, u_ref):
    xn = _rms(x_ref[...], g_ref[...], RMS_EPS).astype(BF16)
    u_ref[...] = jnp.dot(xn, w_ref[...], preferred_element_type=F32)


def _odd_inproj(x3, g, w_bf, tm):
    bx, sx, d = x3.shape
    c = w_bf.shape[1]
    out = pl.pallas_call(
        _odd_inproj_kernel,
        grid=(bx, sx // tm),
        in_specs=[pl.BlockSpec((None, tm, d), lambda b, i: (b, i, 0)), pl.BlockSpec((1, d), lambda b, i: (0, 0)),
                  pl.BlockSpec((d, c), lambda b, i: (0, 0))],
        out_specs=pl.BlockSpec((tm, c), lambda b, i: (i, b)),
        out_shape=jax.ShapeDtypeStruct((sx, bx * c), F32),
        compiler_params=_cparams(("parallel", "parallel")),
        name="odd_inproj",
    )(x3, g, w_bf)
    return out.reshape(sx * bx, c)


def _s5_prep_kernel(lr_ref, li_ref, ldt_ref, br_ref, bi_ref, abr_ref, abi_ref, wre_ref, wim_ref):
    lr = lr_ref[...]
    li = li_ref[...]
    dt = jnp.exp(ldt_ref[...])
    mag = jnp.exp(lr * dt)
    ang = li * dt
    abr = mag * jnp.cos(ang)
    abi = mag * jnp.sin(ang)
    den = lr * lr + li * li
    nr = abr - 1.0
    ni = abi
    fr = (nr * lr + ni * li) / den
    fi = (ni * lr - nr * li) / den
    abr_ref[...] = abr
    abi_ref[...] = abi
    for j in range(lr.shape[0]):
        frj = fr[j:j + 1, :]
        fij = fi[j:j + 1, :]
        br = br_ref[j]
        bi = bi_ref[j]
        wr = frj * br - fij * bi
        wi = frj * bi + fij * br
        wre_ref[j, :LANES, :] = wr.astype(BF16)
        wim_ref[j, :LANES, :] = wi.astype(BF16)
        arj = abr[j:j + 1, :]
        aij = abi[j:j + 1, :]
        wre_ref[j, LANES:, :] = (arj * wr - aij * wi).astype(BF16)
        wim_ref[j, LANES:, :] = (arj * wi + aij * wr).astype(BF16)


def _s5_prep(lr, li, ldt, br_bd, bi_bd):
    nt = lr.shape[0]
    vm = pl.BlockSpec(memory_space=pltpu.VMEM)
    return pl.pallas_call(
        _s5_prep_kernel,
        in_specs=[vm] * 5,
        out_specs=[vm] * 4,
        out_shape=[jax.ShapeDtypeStruct((nt, STATE_TILE), F32), jax.ShapeDtypeStruct((nt, STATE_TILE), F32),
                   jax.ShapeDtypeStruct((nt, 2 * LANES, STATE_TILE), BF16),
                   jax.ShapeDtypeStruct((nt, 2 * LANES, STATE_TILE), BF16)],
        name="s5_discretise",
    )(lr, li, ldt, br_bd, bi_bd)


SCAN_LANES = 512


def _scan_interleaved_group(bur_ref, bui_ref, cr_ref, ci_ref, abr_ref, abi_ref, g):
    rc = bur_ref.shape[0]
    upper = lax.broadcasted_iota(jnp.int32, (SUBLANES, SCAN_LANES), 0) >= 4
    sl = slice(g * SCAN_LANES, (g + 1) * SCAN_LANES)
    ar = jnp.broadcast_to(abr_ref[:, sl], (SUBLANES, SCAN_LANES))
    ai = jnp.broadcast_to(abi_ref[:, sl], (SUBLANES, SCAN_LANES))
    a2r = ar * ar - ai * ai
    a2i = ar * ai + ai * ar
    acr = jnp.where(upper, a2r, ar)
    aci = jnp.where(upper, a2i, ai)

    def body(r, carry):
        pr, pi = carry
        rows = pl.ds(pl.multiple_of(r * SUBLANES, SUBLANES), SUBLANES)
        x1r = bur_ref[rows, sl]
        x1i = bui_ref[rows, sl]
        x2r = x1r + (acr * pr - aci * pi)
        x2i = x1i + (acr * pi + aci * pr)
        bur_ref[rows, sl] = x2r
        bui_ref[rows, sl] = x2i
        npr = jnp.where(upper, x2r, pltpu.roll(x2r, 4, 0))
        npi = jnp.where(upper, x2i, pltpu.roll(x2i, 4, 0))
        return npr, npi

    pr, pi = lax.fori_loop(0, rc // SUBLANES, body, (cr_ref[:, sl], ci_ref[:, sl]), unroll=True)
    cr_ref[:, sl] = pr
    ci_ref[:, sl] = pi


def _scan_dense_group(bur_ref, bui_ref, cr_ref, ci_ref, abr_ref, abi_ref, g, bsz):
    rc = bur_ref.shape[0]
    sl = slice(g * SCAN_LANES, (g + 1) * SCAN_LANES)
    ar = abr_ref[:, sl]
    ai = abi_ref[:, sl]
    pr = cr_ref[:, sl]
    pi = ci_ref[:, sl]
    for t in range(rc // bsz):
        rows = slice(t * bsz, (t + 1) * bsz)
        nr = bur_ref[rows, sl] + (ar * pr - ai * pi)
        ni = bui_ref[rows, sl] + (ar * pi + ai * pr)
        bur_ref[rows, sl] = nr
        bui_ref[rows, sl] = ni
        pr, pi = nr, ni
    cr_ref[:, sl] = pr
    ci_ref[:, sl] = pi


def _s5_block(u, abr_ref, abi_ref, wre_ref, wim_ref, cre_ref, cim_ref, d_ref, wg_ref,
              bur_ref, bui_ref, cr_ref, ci_ref, bsz):
    n_tiles = wre_ref.shape[0]
    ub = u.astype(BF16)
    if bsz == 4:
        row = lax.broadcasted_iota(jnp.int32, u.shape, 0)
        ub_prev = jnp.where((row & 4) != 0, pltpu.roll(u, 4, 0), 0.0).astype(BF16)
    ys = []
    for j in range(n_tiles):
        cols = slice(j * LANES, (j + 1) * LANES)
        states = slice(j * STATE_TILE, (j + 1) * STATE_TILE)
        if bsz == 4:
            lhs = jnp.concatenate([ub[:, cols], ub_prev[:, cols]], axis=-1)
            bur_ref[:, states] = jnp.dot(lhs, wre_ref[j], preferred_element_type=F32)
            bui_ref[:, states] = jnp.dot(lhs, wim_ref[j], preferred_element_type=F32)
            _scan_interleaved_group(bur_ref, bui_ref, cr_ref, ci_ref, abr_ref, abi_ref, j)
        else:
            bur_ref[:, states] = jnp.dot(ub[:, cols], wre_ref[j, :LANES, :], preferred_element_type=F32)
            bui_ref[:, states] = jnp.dot(ub[:, cols], wim_ref[j, :LANES, :], preferred_element_type=F32)
            _scan_dense_group(bur_ref, bui_ref, cr_ref, ci_ref, abr_ref, abi_ref, j, bsz)
        hr = bur_ref[:, states].astype(BF16)
        hi = bui_ref[:, states].astype(BF16)
        ys.append(jnp.dot(hr, cre_ref[j], preferred_element_type=F32)
                  - jnp.dot(hi, cim_ref[j], preferred_element_type=F32))
    y = jnp.concatenate(ys, axis=-1) + d_ref[...] * u
    z = 0.5 * y * (1.0 + lax.erf(y * math.sqrt(0.5)))
    gate = jnp.dot(z.astype(BF16), wg_ref[...], preferred_element_type=F32)
    return z * _sigmoid(gate)


def _s5_kernel(u_ref, h0r_ref, h0i_ref, abr_ref, abi_ref, wre_ref, wim_ref, cre_ref, cim_ref, d_ref, wg_ref,
               z_ref, hr_ref, hi_ref, bur_ref, bui_ref, cr_ref, ci_ref, *, bsz):
    step = pl.program_id(0)

    @pl.when(step == 0)
    def _init():
        cr_ref[...] = h0r_ref[...]
        ci_ref[...] = h0i_ref[...]

    zg = _s5_block(u_ref[...], abr_ref, abi_ref, wre_ref, wim_ref, cre_ref, cim_ref, d_ref, wg_ref,
                   bur_ref, bui_ref, cr_ref, ci_ref, bsz)
    z_ref[...] = zg.astype(BF16)

    @pl.when(step == pl.num_programs(0) - 1)
    def _state():
        hr_ref[...] = cr_ref[...]
        hi_ref[...] = ci_ref[...]


def _s5_core(u_tm, h0r, h0i, abr, abi, wre, wim, cre, cim, d_skip, wg_bf, bsz, rc):
    rows, c = u_tm.shape
    n_state = abr.shape[1]
    carry_rows = h0r.shape[0]
    const2 = lambda s: (0, 0)
    const3 = lambda s: (0, 0, 0)
    return pl.pallas_call(
        functools.partial(_s5_kernel, bsz=bsz),
        grid=(rows // rc,),
        in_specs=[pl.BlockSpec((rc, c), lambda s: (s, 0)),
                  pl.BlockSpec((carry_rows, n_state), const2), pl.BlockSpec((carry_rows, n_state), const2),
                  pl.BlockSpec((1, n_state), const2), pl.BlockSpec((1, n_state), const2),
                  pl.BlockSpec(wre.shape, const3), pl.BlockSpec(wim.shape, const3),
                  pl.BlockSpec(cre.shape, const3), pl.BlockSpec(cim.shape, const3),
                  pl.BlockSpec((1, c), const2), pl.BlockSpec(wg_bf.shape, const2)],
        out_specs=[pl.BlockSpec((rc, c), lambda s: (s, 0)),
                   pl.BlockSpec((carry_rows, n_state), const2), pl.BlockSpec((carry_rows, n_state), const2)],
        out_shape=[jax.ShapeDtypeStruct((rows, c), BF16), jax.ShapeDtypeStruct((carry_rows, n_state), F32),
                   jax.ShapeDtypeStruct((carry_rows, n_state), F32)],
        scratch_shapes=[pltpu.VMEM((rc, n_state), F32), pltpu.VMEM((rc, n_state), F32),
                        pltpu.VMEM((carry_rows, n_state), F32), pltpu.VMEM((carry_rows, n_state), F32)],
        compiler_params=_cparams(("arbitrary",)),
        name="s5_core",
    )(u_tm, h0r, h0i, abr, abi, wre, wim, cre, cim, d_skip, wg_bf)


ODD_BATCH = 4
ODD_TIME_CHUNK = 128


def _odd_fused_kernel(x_ref, gpre_ref, win_ref, h0r_ref, h0i_ref, abr_ref, abi_ref, wre_ref, wim_ref, cre_ref,
                      cim_ref, d_ref, wg_ref, wout_ref, gpost_ref, o_ref, hr_ref, hi_ref,
                      stage_ref, bur_ref, bui_ref, cr_ref, ci_ref):
    step = pl.program_id(0)
    tc = x_ref.shape[1]
    n_lane_tiles = stage_ref.shape[0]

    @pl.when(step == 0)
    def _init():
        cr_ref[...] = h0r_ref[...]
        ci_ref[...] = h0i_ref[...]

    for b in range(ODD_BATCH):
        xn = _rms(x_ref[b], gpre_ref[...], RMS_EPS)
        for j in range(n_lane_tiles):
            stage_ref[j, pl.ds(b, tc, stride=ODD_BATCH), :] = xn[:, j * LANES:(j + 1) * LANES]
    xn_tm = jnp.concatenate([stage_ref[j] for j in range(n_lane_tiles)], axis=-1).astype(BF16)
    u = jnp.dot(xn_tm, win_ref[...], preferred_element_type=F32)
    zg = _s5_block(u, abr_ref, abi_ref, wre_ref, wim_ref, cre_ref, cim_ref, d_ref, wg_ref,
                   bur_ref, bui_ref, cr_ref, ci_ref, ODD_BATCH)
    mix = jnp.dot(zg.astype(BF16), wout_ref[...], preferred_element_type=F32)
    normed = _rms(mix, gpost_ref[...], RMS_EPS)
    for j in range(n_lane_tiles):
        stage_ref[j] = normed[:, j * LANES:(j + 1) * LANES]
    for b in range(ODD_BATCH):
        back = jnp.concatenate([stage_ref[j, pl.ds(b, tc, stride=ODD_BATCH), :] for j in range(n_lane_tiles)], axis=-1)
        o_ref[b] = x_ref[b] + back

    @pl.when(step == pl.num_programs(0) - 1)
    def _state():
        hr_ref[...] = cr_ref[...]
        hi_ref[...] = ci_ref[...]


def _odd_fused(x3, h0r, h0i, ow, tc):
    b, s, d = x3.shape
    assert b == ODD_BATCH
    n_state = ow["abr"].shape[1]
    const2 = lambda i: (0, 0)
    const3 = lambda i: (0, 0, 0)
    full = lambda a: pl.BlockSpec(a.shape, const2 if a.ndim == 2 else const3)
    rows = ODD_BATCH * tc
    return pl.pallas_call(
        _odd_fused_kernel,
        grid=(s // tc,),
        in_specs=[pl.BlockSpec((b, tc, d), lambda i: (0, i, 0)), full(ow["g_pre"]), full(ow["w_in"]),
                  full(h0r), full(h0i), full(ow["abr"]), full(ow["abi"]), full(ow["wre"]), full(ow["wim"]),
                  full(ow["cre"]), full(ow["cim"]), full(ow["d"]), full(ow["w_gate"]), full(ow["w_out"]),
                  full(ow["g_post"])],
        out_specs=[pl.BlockSpec((b, tc, d), lambda i: (0, i, 0)), full(h0r), full(h0i)],
        out_shape=[jax.ShapeDtypeStruct((b, s, d), F32), jax.ShapeDtypeStruct(h0r.shape, F32),
                   jax.ShapeDtypeStruct(h0i.shape, F32)],
        scratch_shapes=[pltpu.VMEM((d // LANES, rows, LANES), F32),
                        pltpu.VMEM((rows, n_state), F32), pltpu.VMEM((rows, n_state), F32),
                        pltpu.VMEM(h0r.shape, F32), pltpu.VMEM(h0i.shape, F32)],
        compiler_params=_cparams(("arbitrary",)),
        name="odd_mixer_fused",
    )(x3, ow["g_pre"], ow["w_in"], h0r, h0i, ow["abr"], ow["abi"], ow["wre"], ow["wim"], ow["cre"], ow["cim"],
      ow["d"], ow["w_gate"], ow["w_out"], ow["g_post"])


def _block_diag_tiles(x, rows_first):
    nt, ng, a, b = x.shape
    eye = jnp.eye(ng, dtype=jnp.bool_)
    out = jnp.where(eye[None, :, None, :, None], x[:, :, :, None, :], jnp.zeros((), x.dtype))
    del rows_first
    return out.reshape(nt, ng * a, ng * b)


def _tile(n):
    for cand in (512, 256, 128, 64, 32, 16, 8):
        if n % cand == 0:
            return cand
    return n


def _even_layer(hp, hs, layer, idx, ew, k_stack, v_stack, kcache, vcache, page_table, state_conv):
    b, s, d = hp.shape
    nb, n_new, _ = hs.shape
    m = nb * n_new
    tm = _tile(s)
    lam_init = _lambda_init(layer)
    row = lambda b_, i: (b_, i, 0)
    q, ktb, vb, conv_p, buf_p, k_stack, v_stack = _even_front(hp, ew, k_stack, v_stack, tm)
    qs, kts, _, vs, _, ugs = _even_inproj(hs.reshape(1, m, d), ew["g_pre"], ew["w_in"], ew["wkt"], _tile(m))
    q5 = qs.reshape(nb, n_new, N_HEADS, 2, QK_DIM).transpose(0, 2, 3, 1, 4)
    q5 = jnp.pad(q5, ((0, 0), (0, 0), (0, 0), (0, SUBLANES - n_new), (0, 0)))
    eye = jnp.eye(2 * N_HEADS, dtype=jnp.bool_).reshape(N_HEADS, 2, 1, N_HEADS, 2, 1)
    qbd = jnp.where(eye[None], q5[:, :, :, :, None, None, :], jnp.zeros((), BF16)).reshape(nb, Q_ROWS, ATT_W)
    ktn = kts.reshape(ATT_W, nb, n_new).transpose(1, 0, 2)
    ktn = jnp.pad(ktn, ((0, 0), (0, 0), (0, LANES - n_new))).astype(BF16)
    vn = vs.reshape(nb, n_new, N_HEADS, HEAD_W).transpose(0, 2, 1, 3)
    vn = jnp.pad(vn, ((0, 0), (0, 0), (0, LANES - n_new), (0, 0))).astype(BF16)
    attn_p, attn_s = _attention(q, ktb, vb, page_table, qbd, ktn, vn, kcache, vcache, ew["lq"], ew["subln_g"],
                                idx, n_new, lam_init, tm)
    specs = [pl.BlockSpec((None, tm, ATT_W), row), pl.BlockSpec((None, tm, CONV_CH), row)]
    hp = _outproj([attn_p, conv_p], specs, ew["w_out"], ew["g_post"], hp, tm)
    conv_s, buf_s = _conv_branch(ugs.reshape(nb, n_new, 2 * CONV_CH), state_conv, ew["conv_w"], ew["conv_b"],
                                 ew["ln_g"], ew["ln_b"], n_new)
    tms = _tile(m)
    specs = [pl.BlockSpec((None, tms, ATT_W), row), pl.BlockSpec((None, tms, CONV_CH), row)]
    hs_new = _outproj([attn_s[:, :n_new, :].reshape(1, m, ATT_W), conv_s.reshape(1, m, CONV_CH)], specs,
                      ew["w_out"], ew["g_post"], hs.reshape(1, m, d), tms)
    k_new_s = kts.reshape(N_HEADS, 2, QK_DIM, nb, n_new).transpose(3, 4, 0, 1, 2)
    v_new_s = vs.reshape(nb, n_new, N_HEADS, HEAD_W)
    return hp, hs_new.reshape(nb, n_new, d), k_stack, v_stack, buf_p, k_new_s, v_new_s, buf_s


def _odd_layer_prompt(hp, ow):
    b, s, d = hp.shape
    n_state = ow["abr"].shape[1]
    zeros = jnp.zeros((SUBLANES, n_state), F32)
    hp, hr, hi = _odd_fused(hp, zeros, zeros, ow, min(s, ODD_TIME_CHUNK))
    return hp, hr[:b], hi[:b]


def _odd_layer_sample(hs, ow, h0r, h0i):
    nb, n_new, d = hs.shape
    m = nb * n_new
    u_tm = _odd_inproj(hs.transpose(1, 0, 2).reshape(1, m, d), ow["g_pre"], ow["w_in"], _tile(m))
    z, hr, hi = _s5_core(u_tm, h0r, h0i, ow["abr"], ow["abi"], ow["wre"], ow["wim"], ow["cre"], ow["cim"],
                         ow["d"], ow["w_gate"], nb, m)
    z = z.reshape(n_new, nb, d).transpose(1, 0, 2).reshape(1, m, d)
    tm = _tile(m)
    specs = [pl.BlockSpec((None, tm, d), lambda b_, i: (b_, i, 0))]
    out = _outproj([z], specs, ow["w_out"], ow["g_post"], hs.reshape(1, m, d), tm)
    return out.reshape(nb, n_new, d), hr, hi


def kernel(x_prompt, x_sample, cache_k, cache_v, page_table, state_conv, state_ssm_re, state_ssm_im, g_mix_pre, g_mix_post, g_ffn_pre, g_ffn_post, w_in_even, lambda_qk, subln_g, conv_w, conv_b, conv_ln_g, conv_ln_b, w_out_even, w_in_odd, ssm_a_re, ssm_a_im, ssm_b_re, ssm_b_im, ssm_c_re, ssm_c_im, ssm_d, ssm_log_dt, w_gate_odd, w_out_odd, w_ffn_up, w_ffn_down):
    depth, d = g_mix_pre.shape
    hp, hs = x_prompt, x_sample
    nb, n_new, _ = x_sample.shape
    n_layers_even, n_phys, page = cache_k.shape[:3]
    n_groups, n_p = ssm_a_re.shape[1:]
    n_tiles = n_groups // GROUPS_PER_TILE
    n_state = n_groups * n_p

    kcache = cache_k.transpose(0, 1, 3, 4, 5, 2).reshape(n_layers_even, n_phys, ATT_W, page)
    vcache = cache_v.reshape(n_layers_even, n_phys, page * N_HEADS, HEAD_W)

    row = lambda x: x.reshape(1, -1)
    outs = {k: [] for k in ("ks", "vs", "cp", "cs", "srp", "sip", "srs", "sis")}
    k_stack = v_stack = None
    for layer in range(depth):
        i = layer // 2
        if layer % 2 == 0:
            w_in = w_in_even[i].astype(BF16)
            ew = dict(g_pre=row(g_mix_pre[layer]), g_post=row(g_mix_post[layer]), w_in=w_in,
                      wkt=w_in[:, ATT_W:2 * ATT_W].T, lq=lambda_qk[i], subln_g=row(subln_g[i]),
                      conv_w=conv_w[i], conv_b=row(conv_b[i]), ln_g=row(conv_ln_g[i]), ln_b=row(conv_ln_b[i]),
                      w_out=w_out_even[i].astype(BF16))
            hp, hs, k_stack, v_stack, buf_p, k_new, v_new, buf_s = _even_layer(
                hp, hs, layer, i, ew, k_stack, v_stack, kcache, vcache, page_table, state_conv[i])
            outs["cp"].append(buf_p)
            outs["ks"].append(k_new); outs["vs"].append(v_new); outs["cs"].append(buf_s)
        else:
            lr = ssm_a_re[i].reshape(n_tiles, STATE_TILE)
            li = ssm_a_im[i].reshape(n_tiles, STATE_TILE)
            ldt = jnp.repeat(ssm_log_dt[i], n_p).reshape(n_tiles, STATE_TILE)
            b4 = lambda x: x.reshape(n_tiles, GROUPS_PER_TILE, n_p, GROUP_CH).transpose(0, 1, 3, 2)
            abr, abi, wre, wim = _s5_prep(lr, li, ldt, _block_diag_tiles(b4(ssm_b_re[i]), True),
                                          _block_diag_tiles(b4(ssm_b_im[i]), True))
            c4 = lambda x: x.reshape(n_tiles, GROUPS_PER_TILE, GROUP_CH, n_p).transpose(0, 1, 3, 2)
            ow = dict(g_pre=row(g_mix_pre[layer]), g_post=row(g_mix_post[layer]), w_in=w_in_odd[i].astype(BF16),
                      abr=abr.reshape(1, n_state), abi=abi.reshape(1, n_state), wre=wre, wim=wim,
                      cre=_block_diag_tiles(c4(ssm_c_re[i]), False).astype(BF16),
                      cim=_block_diag_tiles(c4(ssm_c_im[i]), False).astype(BF16),
                      d=row(ssm_d[i]), w_gate=w_gate_odd[i].astype(BF16), w_out=w_out_odd[i].astype(BF16))
            hp, sr, si = _odd_layer_prompt(hp, ow)
            outs["srp"].append(sr.reshape(-1, n_groups, n_p)); outs["sip"].append(si.reshape(-1, n_groups, n_p))
            hs, sr, si = _odd_layer_sample(hs, ow, state_ssm_re[i].reshape(nb, n_state),
                                           state_ssm_im[i].reshape(nb, n_state))
            outs["srs"].append(sr.reshape(nb, n_groups, n_p)); outs["sis"].append(si.reshape(nb, n_groups, n_p))
        gpre, gpost = row(g_ffn_pre[layer]), row(g_ffn_post[layer])
        mp = hp.shape[0] * hp.shape[1]
        hp = _mlp(hp.reshape(mp, d), gpre, w_ffn_up, w_ffn_down, gpost, layer,
                  MLP_TM if mp % MLP_TM == 0 else _tile(mp), MLP_TK).reshape(hp.shape)
        ms = nb * n_new
        hs = _mlp(hs.reshape(ms, d), gpre, w_ffn_up, w_ffn_down, gpost, layer, _tile(ms), MLP_TK).reshape(hs.shape)
    st = jnp.stack
    bp, sp = x_prompt.shape[:2]
    new_k_prompt = k_stack.reshape(n_layers_even, bp, N_HEADS, 2, QK_DIM, sp).transpose(0, 1, 5, 2, 3, 4)
    new_v_prompt = v_stack.reshape(n_layers_even, bp, sp, N_HEADS, HEAD_W)
    return (hp, hs, new_k_prompt, new_v_prompt, st(outs["ks"]), st(outs["vs"]), st(outs["cp"]), st(outs["cs"]),
            st(outs["srp"]), st(outs["sip"]), st(outs["srs"]), st(outs["sis"]))
```
